```python
import math
import jax, jax.numpy as jnp
from jax import lax
import numpy as np

D_MODEL = 1024
BATCH = 4
SEQ = 4096
DEPTH = 2

GRID_W = 64
CTX_LEN = 256
N_MIXERS = 2
N_ATTN_LAYERS = (DEPTH + 1) // 2
N_SSM_LAYERS = DEPTH // 2
EPS = 1e-6

HEAD_DIM = 64
N_HEADS = D_MODEL // HEAD_DIM
N_KV_HEADS = N_HEADS // 4
Q_PER_KV = N_HEADS // N_KV_HEADS
Q_DIM = N_HEADS * HEAD_DIM
QKV_DIM = (N_HEADS + 2 * N_KV_HEADS) * HEAD_DIM
WINDOW = 128
BLOCK = 128
ROPE_FREQS = HEAD_DIM // 4
ROPE_BASE = 10000.0

D_INNER = 2 * D_MODEL
SSM_HEAD_DIM = 64
SSM_HEADS = D_INNER // SSM_HEAD_DIM
SSM_GROUPS = 8
HEADS_PER_GROUP = SSM_HEADS // SSM_GROUPS
D_STATE = 128
D_CONV = 3
CHUNK = 128
CONV_DIM = D_INNER + 2 * SSM_GROUPS * D_STATE
IN_PROJ_DIM = D_INNER + CONV_DIM + 2 * SSM_HEADS

D_FF = ((8 * D_MODEL // 3 + 255) // 256) * 256

kernel_name = "hybrid_swa_ssd_diffusion_trunk"

F32 = jnp.float32


def rmsnorm(x, g):
    x32 = x.astype(F32)
    y = x32 * lax.rsqrt(jnp.mean(x32 * x32, axis=-1, keepdims=True) + EPS)
    return (y * g.astype(F32)).astype(x.dtype)


def modulate(x, g, shift, scale):
    return rmsnorm(x, g) * (1 + scale) + shift


def swiglu(h, w_gate, w_up, w_down):
    return (jax.nn.silu(h @ w_gate) * (h @ w_up)) @ w_down


def axial_rope_tables(L):
    rows = L // GRID_W
    row = jnp.repeat(jnp.arange(rows, dtype=F32), GRID_W)
    col = jnp.tile(jnp.arange(GRID_W, dtype=F32), rows)
    inv = ROPE_BASE ** (-jnp.arange(ROPE_FREQS, dtype=F32) / ROPE_FREQS)
    ang = jnp.stack([row, col], axis=-1)[:, :, None] * inv
    return jnp.cos(ang), jnp.sin(ang)


def axial_rope(x, cos, sin):
    xs = x.reshape(x.shape[:-1] + (2, 2, ROPE_FREQS))
    x1, x2 = xs[..., 0, :], xs[..., 1, :]
    c, s = cos[None, :, None], sin[None, :, None]
    out = jnp.stack([x1 * c - x2 * s, x2 * c + x1 * s], axis=-2)
    return out.reshape(x.shape).astype(x.dtype)


def softmax_with_sink(logits, sink):
    full = jnp.concatenate([logits, jnp.broadcast_to(sink, logits.shape[:-1] + (1,))], axis=-1)
    return jax.nn.softmax(full, axis=-1)[..., :-1]


def attention_mixer(h, hc, w_qkv, w_o, sinks, cos, sin, with_ctx_out):
    b, L, _ = h.shape
    C = hc.shape[1]
    q, k, v = jnp.split(h @ w_qkv, [Q_DIM, Q_DIM + N_KV_HEADS * HEAD_DIM], axis=-1)
    q = axial_rope(q.reshape(b, L, N_HEADS, HEAD_DIM), cos, sin)
    k = axial_rope(k.reshape(b, L, N_KV_HEADS, HEAD_DIM), cos, sin)
    v = v.reshape(b, L, N_KV_HEADS, HEAD_DIM)
    q = q.reshape(b, L, N_KV_HEADS, Q_PER_KV, HEAD_DIM)
    kc, vc = jnp.split(hc @ w_qkv[:, Q_DIM:], 2, axis=-1)
    kc = kc.reshape(b, C, N_KV_HEADS, HEAD_DIM)
    vc = vc.reshape(b, C, N_KV_HEADS, HEAD_DIM)
    sink = sinks.astype(F32).reshape(1, N_KV_HEADS, Q_PER_KV, 1, 1)
    scale = HEAD_DIM ** -0.5
    pad = ((0, 0), (BLOCK, BLOCK), (0, 0), (0, 0))
    kp, vp = jnp.pad(k, pad), jnp.pad(v, pad)
    offs_q = jnp.arange(BLOCK)
    offs_k = jnp.arange(3 * BLOCK) - BLOCK

    def block(n):
        start = n * BLOCK
        qb = lax.dynamic_slice_in_dim(q, start, BLOCK, axis=1)
        kb = lax.dynamic_slice_in_dim(kp, start, 3 * BLOCK, axis=1)
        vb = lax.dynamic_slice_in_dim(vp, start, 3 * BLOCK, axis=1)
        q_pos = start + offs_q
        k_pos = start + offs_k
        valid = ((jnp.abs(k_pos[None, :] - q_pos[:, None]) <= WINDOW)
                 & (k_pos >= 0)[None, :] & (k_pos < L)[None, :])
        s_ctx = jnp.einsum('bqgrd,bkgd->bgrqk', qb, kc).astype(F32) * scale
        s_win = jnp.einsum('bqgrd,bkgd->bgrqk', qb, kb).astype(F32) * scale
        s_win = jnp.where(valid, s_win, -jnp.inf)
        p = softmax_with_sink(jnp.concatenate([s_ctx, s_win], axis=-1), sink).astype(v.dtype)
        return (jnp.einsum('bgrqk,bkgd->bqgrd', p[..., :C], vc)
                + jnp.einsum('bgrqk,bkgd->bqgrd', p[..., C:], vb))

    o = lax.map(block, jnp.arange(L // BLOCK))
    o = jnp.moveaxis(o, 0, 1).reshape(b, L, Q_DIM)
    y = o @ w_o
    if not with_ctx_out:
        return y, None
    qc = (hc @ w_qkv[:, :Q_DIM]).reshape(b, C, N_KV_HEADS, Q_PER_KV, HEAD_DIM)
    sc = jnp.einsum('bqgrd,bkgd->bgrqk', qc, kc).astype(F32) * scale
    pc = softmax_with_sink(sc, sink).astype(vc.dtype)
    oc = jnp.einsum('bgrqk,bkgd->bqgrd', pc, vc).reshape(b, C, Q_DIM)
    return y, oc @ w_o


def centred_depthwise_conv(u, w, bias):
    K = w.shape[0]
    out = lax.conv_general_dilated(u, w[:, None, :], window_strides=(1,),
                                   padding=[(K // 2, K // 2)],
                                   dimension_numbers=('NWC', 'WIO', 'NWC'),
                                   feature_group_count=u.shape[-1])
    return out + bias


def ssm_project(u, w_in, conv_w, conv_b, dt_bias):
    b, L, _ = u.shape
    z, xbc, dt = jnp.split(u @ w_in, [D_INNER, D_INNER + CONV_DIM], axis=-1)
    xbc = jax.nn.silu(centred_depthwise_conv(xbc, conv_w, conv_b))
    xs, Bm, Cm = jnp.split(xbc, [D_INNER, D_INNER + SSM_GROUPS * D_STATE], axis=-1)
    dt = jax.nn.softplus(dt.reshape(b, L, 2, SSM_HEADS).astype(F32) + dt_bias.astype(F32))
    return (z,
            xs.reshape(b, L, SSM_GROUPS, HEADS_PER_GROUP, SSM_HEAD_DIM),
            Bm.reshape(b, L, SSM_GROUPS, D_STATE),
            Cm.reshape(b, L, SSM_GROUPS, D_STATE),
            dt.reshape(b, L, 2, SSM_GROUPS, HEADS_PER_GROUP))


def ssd_chunked(xs, dt, A, Bm, Cm, h0):
    b, L, G, HG, P = xs.shape
    nc = L // CHUNK
    x = xs.astype(F32).reshape(b, nc, CHUNK, G, HG, P)
    dt = dt.reshape(b, nc, CHUNK, G, HG)
    Bc = Bm.astype(F32).reshape(b, nc, CHUNK, G, D_STATE)
    Cc = Cm.astype(F32).reshape(b, nc, CHUNK, G, D_STATE)
    cum = jnp.cumsum(jnp.moveaxis(dt * A, 2, -1), axis=-1)
    xdt = x * dt[..., None]
    tri = jnp.tril(jnp.ones((CHUNK, CHUNK), dtype=bool))
    decay = jnp.exp(jnp.where(tri, cum[..., :, None] - cum[..., None, :], -jnp.inf))
    cb = jnp.einsum('bcign,bcjgn->bcgij', Cc, Bc)
    y_diag = jnp.einsum('bcghij,bcjghp->bcighp', cb[:, :, :, None] * decay, xdt)
    to_end = jnp.exp(cum[..., -1:] - cum)
    states = jnp.einsum('bcghj,bcjgn,bcjghp->bcghpn', to_end, Bc, xdt)
    chunk_decay = jnp.exp(cum[..., -1])

    def step(hs, inp):
        s, d = inp
        return d[..., None, None] * hs + s, hs

    h_final, h_start = lax.scan(step, h0.astype(F32),
                                (jnp.moveaxis(states, 1, 0), jnp.moveaxis(chunk_decay, 1, 0)))
    h_start = jnp.moveaxis(h_start, 0, 1)
    y_off = jnp.einsum('bcign,bcghpn,bcghi->bcighp', Cc, h_start, jnp.exp(cum))
    return (y_diag + y_off).reshape(b, L, G, HG, P).astype(xs.dtype), h_final


def ssd_final_state(xs, dt, A, Bm):
    cum = jnp.cumsum(dt * A, axis=1)
    decay = jnp.exp(cum[:, -1:] - cum)
    return jnp.einsum('blgh,blgn,blghp->bghpn', decay * dt, Bm.astype(F32), xs.astype(F32))


def bidirectional_ssd(xs, dt, A, Bm, Cm, h0_fwd, h0_bwd):
    flip = lambda u: jnp.flip(u, axis=1)
    y_f, h_f = ssd_chunked(xs, dt[:, :, 0], A[0], Bm, Cm, h0_fwd)
    y_b, h_b = ssd_chunked(flip(xs), flip(dt[:, :, 1]), A[1], flip(Bm), flip(Cm), h0_bwd)
    return y_f + flip(y_b), h_f, h_b


def ssm_output(y, xs, z, D_skip, norm_g, w_out):
    b, L = y.shape[:2]
    y = y + D_skip.reshape(SSM_GROUPS, HEADS_PER_GROUP)[:, :, None] * xs
    y = y.reshape(b, L, SSM_GROUPS, D_INNER // SSM_GROUPS) * jax.nn.silu(z).reshape(b, L, SSM_GROUPS, -1)
    y = rmsnorm(y, norm_g.reshape(SSM_GROUPS, -1))
    return y.reshape(b, L, D_INNER) @ w_out


def ssm_mixer(h, hc, w_in, conv_w, conv_b, dt_bias, A_log, D_skip, norm_g, w_out, with_ctx_out):
    A = -jnp.exp(A_log.astype(F32)).reshape(2, SSM_GROUPS, HEADS_PER_GROUP)
    zc, xc, Bc, Cc, dtc = ssm_project(hc, w_in, conv_w, conv_b, dt_bias)
    if with_ctx_out:
        zeros = jnp.zeros((hc.shape[0], SSM_GROUPS, HEADS_PER_GROUP, SSM_HEAD_DIM, D_STATE), F32)
        yc, hc_f, hc_b = bidirectional_ssd(xc, dtc, A, Bc, Cc, zeros, zeros)
        out_c = ssm_output(yc, xc, zc, D_skip, norm_g, w_out)
    else:
        hc_f = ssd_final_state(xc, dtc[:, :, 0], A[0], Bc)
        hc_b = ssd_final_state(jnp.flip(xc, 1), jnp.flip(dtc[:, :, 1], 1), A[1], jnp.flip(Bc, 1))
        out_c = None
    z, xs, Bm, Cm, dt = ssm_project(h, w_in, conv_w, conv_b, dt_bias)
    y, _, _ = bidirectional_ssd(xs, dt, A, Bm, Cm, hc_f, hc_b)
    return ssm_output(y, xs, z, D_skip, norm_g, w_out), out_c


def setup_inputs(seed: int = 0) -> dict:
    key = jax.random.key(seed)
    ks = jax.random.split(key, 24)
    nrm = lambda k, shape, s: jax.random.normal(k, shape, F32) * s
    dt_init = jnp.exp(jax.random.uniform(ks[16], (N_SSM_LAYERS, 2, SSM_HEADS), F32,
                                         minval=math.log(1e-3), maxval=math.log(1e-1)))
    return {
        "x": nrm(ks[0], (BATCH, SEQ, D_MODEL), 1.0),
        "c": nrm(ks[1], (BATCH, D_MODEL), 1.0),
        "ctx": nrm(ks[2], (BATCH, CTX_LEN, D_MODEL), 1.0),
        "c_ctx": nrm(ks[3], (D_MODEL,), 1.0),
        "ada_w": nrm(ks[4], (DEPTH, D_MODEL, 6 * D_MODEL), 0.5 * D_MODEL ** -0.5),
        "ada_b": nrm(ks[5], (DEPTH, 6 * D_MODEL), 0.01),
        "norm_mix_g": 1.0 + nrm(ks[6], (DEPTH, D_MODEL), 0.05),
        "norm_ffn_g": 1.0 + nrm(ks[7], (DEPTH, D_MODEL), 0.05),
        "attn_w_qkv": nrm(ks[8], (N_ATTN_LAYERS, D_MODEL, QKV_DIM), D_MODEL ** -0.5),
        "attn_w_o": nrm(ks[9], (N_ATTN_LAYERS, Q_DIM, D_MODEL), Q_DIM ** -0.5),
        "attn_sinks": nrm(ks[10], (N_ATTN_LAYERS, N_HEADS), 1.0),
        "ssm_w_in": nrm(ks[11], (N_SSM_LAYERS, D_MODEL, IN_PROJ_DIM), D_MODEL ** -0.5),
        "ssm_conv_w": nrm(ks[12], (N_SSM_LAYERS, D_CONV, CONV_DIM), D_CONV ** -0.5),
        "ssm_conv_b": nrm(ks[13], (N_SSM_LAYERS, CONV_DIM), 0.01),
        "ssm_dt_bias": dt_init + jnp.log(-jnp.expm1(-dt_init)),
        "ssm_A_log": jnp.log(jax.random.uniform(ks[14], (N_SSM_LAYERS, 2, SSM_HEADS), F32, minval=1.0, maxval=16.0)),
        "ssm_D": 1.0 + nrm(ks[15], (N_SSM_LAYERS, SSM_HEADS), 0.1),
        "ssm_norm_g": 1.0 + nrm(ks[17], (N_SSM_LAYERS, D_INNER), 0.05),
        "ssm_w_out": nrm(ks[18], (N_SSM_LAYERS, D_INNER, D_MODEL), D_INNER ** -0.5),
        "ffn_w_gate": nrm(ks[19], (DEPTH, D_MODEL, D_FF), D_MODEL ** -0.5),
        "ffn_w_up": nrm(ks[20], (DEPTH, D_MODEL, D_FF), D_MODEL ** -0.5),
        "ffn_w_down": nrm(ks[21], (DEPTH, D_FF, D_MODEL), D_FF ** -0.5),
        "final_norm_g": 1.0 + nrm(ks[22], (D_MODEL,), 0.05),
    }


def reference(x, c, ctx, c_ctx, ada_w, ada_b, norm_mix_g, norm_ffn_g, attn_w_qkv, attn_w_o,
              attn_sinks, ssm_w_in, ssm_conv_w, ssm_conv_b, ssm_dt_bias, ssm_A_log, ssm_D,
              ssm_norm_g, ssm_w_out, ffn_w_gate, ffn_w_up, ffn_w_down, final_norm_g):
    L = x.shape[1]
    cos, sin = axial_rope_tables(L)
    for i in range(DEPTH):
        last = i == DEPTH - 1
        j = i // N_MIXERS
        mod = jax.nn.silu(c) @ ada_w[i] + ada_b[i]
        sh1, sc1, g1, sh2, sc2, g2 = jnp.split(mod[:, None, :], 6, axis=-1)
        mod_c = jax.nn.silu(c_ctx) @ ada_w[i] + ada_b[i]
        csh1, csc1, cg1, csh2, csc2, cg2 = jnp.split(mod_c, 6)
        h = modulate(x, norm_mix_g[i], sh1, sc1)
        hc = modulate(ctx, norm_mix_g[i], csh1, csc1)
        if i % N_MIXERS == 0:
            y, yc = attention_mixer(h, hc, attn_w_qkv[j], attn_w_o[j], attn_sinks[j],
                                    cos, sin, not last)
        else:
            y, yc = ssm_mixer(h, hc, ssm_w_in[j], ssm_conv_w[j], ssm_conv_b[j], ssm_dt_bias[j],
                              ssm_A_log[j], ssm_D[j], ssm_norm_g[j], ssm_w_out[j], not last)
        x = x + g1 * y
        x = x + g2 * swiglu(modulate(x, norm_ffn_g[i], sh2, sc2),
                            ffn_w_gate[i], ffn_w_up[i], ffn_w_down[i])
        if not last:
            ctx = ctx + cg1 * yc
            ctx = ctx + cg2 * swiglu(modulate(ctx, norm_ffn_g[i], csh2, csc2),
                                     ffn_w_gate[i], ffn_w_up[i], ffn_w_down[i])
    return rmsnorm(x, final_norm_g)
```

```python
import functools
import math

import jax
import jax.numpy as jnp
from jax import lax
from jax.experimental import pallas as pl
from jax.experimental.pallas import tpu as pltpu

F32 = jnp.float32
BF16 = jnp.bfloat16

D_MODEL = 1024
DEPTH = 2
GRID_W = 64
EPS = 1e-6

HEAD_DIM = 64
N_HEADS = D_MODEL // HEAD_DIM
N_KV_HEADS = N_HEADS // 4
Q_PER_KV = N_HEADS // N_KV_HEADS
Q_DIM = N_HEADS * HEAD_DIM
KV_DIM = N_KV_HEADS * HEAD_DIM
QKV_DIM = Q_DIM + 2 * KV_DIM
WINDOW = 128
BLOCK = 128
ROPE_FREQS = HEAD_DIM // 4
ROPE_BASE = 10000.0

D_INNER = 2 * D_MODEL
SSM_HEAD_DIM = 64
SSM_HEADS = D_INNER // SSM_HEAD_DIM
SSM_GROUPS = 8
HEADS_PER_GROUP = SSM_HEADS // SSM_GROUPS
GROUP_DIM = D_INNER // SSM_GROUPS
D_STATE = 128
CHUNK = 128
BC_DIM = SSM_GROUPS * D_STATE
CONV_DIM = D_INNER + 2 * BC_DIM
D_FF = ((8 * D_MODEL // 3 + 255) // 256) * 256

LANES = 128
HALO = 16
MOD_ROWS = 8
CTX_ROW = 4
VMEM_LIMIT = 48 * 1024 * 1024
NEG_INF = float("-inf")


def _const_spec(shape):
    nd = len(shape)
    return pl.BlockSpec(shape, lambda *_: (0,) * nd, pipeline_mode=pl.Buffered(1))


def _silu(v):
    return v * jax.nn.sigmoid(v)


def _rms_mod(x, g, scale, shift):
    ms = jnp.mean(x * x, axis=-1, keepdims=True)
    return (x * lax.rsqrt(ms + EPS) * g) * (1.0 + scale) + shift


def _mod_chunk(m, i):
    return m[:, i * D_MODEL:(i + 1) * D_MODEL]


def _adaln_kernel(c_ref, w_ref, b_ref, o_ref):
    s = _silu(c_ref[...])
    o_ref[0] = jnp.dot(s, w_ref[0], preferred_element_type=F32,
                       precision=lax.Precision.HIGHEST) + b_ref[0]


def _adaln(cc, ada_w, ada_b):
    tn = 1536
    n = 6 * D_MODEL
    return pl.pallas_call(
        _adaln_kernel,
        grid=(DEPTH, n // tn),
        in_specs=[
            pl.BlockSpec((MOD_ROWS, D_MODEL), lambda i, j: (0, 0)),
            pl.BlockSpec((1, D_MODEL, tn), lambda i, j: (i, 0, j)),
            pl.BlockSpec((1, 1, tn), lambda i, j: (i, 0, j)),
        ],
        out_specs=pl.BlockSpec((1, MOD_ROWS, tn), lambda i, j: (i, 0, j)),
        out_shape=jax.ShapeDtypeStruct((DEPTH, MOD_ROWS, n), F32),
        compiler_params=pltpu.CompilerParams(
            dimension_semantics=("arbitrary", "arbitrary"), vmem_limit_bytes=VMEM_LIMIT),
        name="adaln",
    )(cc, ada_w, ada_b.reshape(DEPTH, 1, n))


def _qkv_kernel(*refs, rope):
    if rope:
        x_ref, mod_ref, g_ref, w_ref, cos_ref, sa_ref, sb_ref, q_ref, k_ref, v_ref = refs
    else:
        x_ref, mod_ref, g_ref, w_ref, q_ref, k_ref, v_ref = refs
    m = mod_ref[0]
    h = _rms_mod(x_ref[0], g_ref[...], _mod_chunk(m, 1), _mod_chunk(m, 0)).astype(BF16)
    qkv = jnp.dot(h, w_ref[...], preferred_element_type=F32)
    scale = HEAD_DIM ** -0.5
    for blk in range((Q_DIM + KV_DIM) // LANES):
        t = qkv[:, blk * LANES:(blk + 1) * LANES]
        if rope:
            t = (t * cos_ref[...] + pltpu.roll(t, LANES - ROPE_FREQS, 1) * sa_ref[...]
                 + pltpu.roll(t, ROPE_FREQS, 1) * sb_ref[...])
        if blk < Q_DIM // LANES:
            q_ref[0, :, blk * LANES:(blk + 1) * LANES] = (t * scale).astype(BF16)
        else:
            kb = blk - Q_DIM // LANES
            k_ref[0, :, kb * LANES:(kb + 1) * LANES] = t.astype(BF16)
    v_ref[0] = qkv[:, Q_DIM + KV_DIM:].astype(BF16)


def _rope_tables(L):
    rows = L // GRID_W
    row = jnp.repeat(jnp.arange(rows, dtype=F32), GRID_W)
    col = jnp.tile(jnp.arange(GRID_W, dtype=F32), rows)
    inv = ROPE_BASE ** (-jnp.arange(ROPE_FREQS, dtype=F32) / ROPE_FREQS)
    ang_r = row[:, None] * inv
    ang_c = col[:, None] * inv
    zero = jnp.zeros_like(ang_r)

    def lanes(r_first, r_second, c_first, c_second):
        head = jnp.concatenate([r_first, r_second, c_first, c_second], axis=-1)
        return jnp.tile(head, (1, LANES // HEAD_DIM))

    cr, sr, cc, sc = jnp.cos(ang_r), jnp.sin(ang_r), jnp.cos(ang_c), jnp.sin(ang_c)
    return lanes(cr, cr, cc, cc), lanes(-sr, zero, -sc, zero), lanes(zero, sr, zero, sc)


def _qkv(x, mod, g, w, tables, mod_row):
    B, L, _ = x.shape
    tm = min(L, 512)
    rope = tables is not None
    in_specs = [
        pl.BlockSpec((1, tm, D_MODEL), lambda b, i: (b, i, 0)),
        pl.BlockSpec((1, 1, 6 * D_MODEL), lambda b, i: (mod_row(b), 0, 0)),
        _const_spec((1, D_MODEL)),
        _const_spec((D_MODEL, QKV_DIM)),
    ]
    args = [x, mod, g, w]
    if rope:
        in_specs += [pl.BlockSpec((tm, LANES), lambda b, i: (i, 0))] * 3
        args += list(tables)
    return pl.pallas_call(
        functools.partial(_qkv_kernel, rope=rope),
        grid=(B, L // tm),
        in_specs=in_specs,
        out_specs=[
            pl.BlockSpec((1, tm, Q_DIM), lambda b, i: (b, i, 0)),
            pl.BlockSpec((1, tm, KV_DIM), lambda b, i: (b, i, 0)),
            pl.BlockSpec((1, tm, KV_DIM), lambda b, i: (b, i, 0)),
        ],
        out_shape=[
            jax.ShapeDtypeStruct((B, L, Q_DIM), BF16),
            jax.ShapeDtypeStruct((B, L, KV_DIM), BF16),
            jax.ShapeDtypeStruct((B, L, KV_DIM), BF16),
        ],
        compiler_params=pltpu.CompilerParams(
            dimension_semantics=("arbitrary", "arbitrary"), vmem_limit_bytes=VMEM_LIMIT),
        name="qkv_rope" if rope else "qkv_ctx",
    )(*args)


def _attn_kernel(*refs, window, nq, n_ctx):
    if window:
        (sink_ref, q_ref, kp_ref, kc_ref, kn_ref, vp_ref, vc_ref, vn_ref,
         kx_ref, vx_ref, o_ref, kbuf, vbuf) = refs
    else:
        sink_ref, q_ref, kx_ref, vx_ref, o_ref, kbuf, vbuf = refs
    n_keys = n_ctx + (3 * BLOCK if window else 0)
    kbuf[0:n_ctx, :] = kx_ref[0]
    vbuf[0:n_ctx, :] = vx_ref[0]
    if window:
        for j, (kr, vr) in enumerate(((kp_ref, vp_ref), (kc_ref, vc_ref), (kn_ref, vn_ref))):
            kbuf[n_ctx + j * BLOCK:n_ctx + (j + 1) * BLOCK, :] = kr[0]
            vbuf[n_ctx + j * BLOCK:n_ctx + (j + 1) * BLOCK, :] = vr[0]
        n = pl.program_id(1)
        qi = lax.broadcasted_iota(jnp.int32, (BLOCK, 3 * BLOCK), 0)
        kj = lax.broadcasted_iota(jnp.int32, (BLOCK, 3 * BLOCK), 1)
        lo = jnp.where(n == 0, BLOCK, 0)
        hi = jnp.where(n == nq - 1, 2 * BLOCK, 3 * BLOCK)
        valid = (kj >= qi) & (kj <= qi + 2 * WINDOW) & (kj >= lo) & (kj < hi)
        bias = jnp.where(valid, 0.0, NEG_INF).astype(F32)
    for g in range(N_KV_HEADS):
        kg = kbuf[:, g * HEAD_DIM:(g + 1) * HEAD_DIM]
        vg = vbuf[:, g * HEAD_DIM:(g + 1) * HEAD_DIM]
        for r in range(Q_PER_KV):
            h = g * Q_PER_KV + r
            qh = q_ref[0, :, h * HEAD_DIM:(h + 1) * HEAD_DIM]
            s = lax.dot_general(qh, kg, (((1,), (1,)), ((), ())), preferred_element_type=F32)
            if window:
                s = jnp.concatenate([s[:, :n_ctx], s[:, n_ctx:] + bias], axis=1)
            sink = sink_ref[h]
            mx = jnp.maximum(jnp.max(s, axis=-1, keepdims=True), sink)
            p = jnp.exp(s - mx)
            denom = jnp.sum(p, axis=-1, keepdims=True) + jnp.exp(sink - mx)
            oh = jnp.dot(p.astype(BF16), vg, preferred_element_type=F32) / denom
            o_ref[0, :, h * HEAD_DIM:(h + 1) * HEAD_DIM] = oh.astype(BF16)
    del n_keys


def _attn(q, k, v, kx, vx, sinks):
    B, L, _ = q.shape
    C = kx.shape[1]
    nq = L // BLOCK
    window = k is not None
    smem = pl.BlockSpec(memory_space=pltpu.SMEM)
    q_spec = pl.BlockSpec((1, BLOCK, Q_DIM), lambda b, n: (b, n, 0))
    x_spec = pl.BlockSpec((1, C, KV_DIM), lambda b, n: (b, 0, 0))
    if window:
        prev = pl.BlockSpec((1, BLOCK, KV_DIM), lambda b, n: (b, jnp.maximum(n - 1, 0), 0))
        cur = pl.BlockSpec((1, BLOCK, KV_DIM), lambda b, n: (b, n, 0))
        nxt = pl.BlockSpec((1, BLOCK, KV_DIM), lambda b, n: (b, jnp.minimum(n + 1, nq - 1), 0))
        in_specs = [smem, q_spec, prev, cur, nxt, prev, cur, nxt, x_spec, x_spec]
        args = (sinks, q, k, k, k, v, v, v, kx, vx)
        n_keys = C + 3 * BLOCK
    else:
        in_specs = [smem, q_spec, x_spec, x_spec]
        args = (sinks, q, kx, vx)
        n_keys = C
    return pl.pallas_call(
        functools.partial(_attn_kernel, window=window, nq=nq, n_ctx=C),
        grid=(B, nq),
        in_specs=in_specs,
        out_specs=pl.BlockSpec((1, BLOCK, Q_DIM), lambda b, n: (b, n, 0)),
        out_shape=jax.ShapeDtypeStruct((B, L, Q_DIM), BF16),
        scratch_shapes=[pltpu.VMEM((n_keys, KV_DIM), BF16), pltpu.VMEM((n_keys, KV_DIM), BF16)],
        compiler_params=pltpu.CompilerParams(
            dimension_semantics=("arbitrary", "arbitrary"), vmem_limit_bytes=VMEM_LIMIT),
        name="attn_window" if window else "attn_ctx",
    )(*args)


def _mix_ffn_kernel(*refs, ssm, final):
    if ssm:
        (x_ref, mod_ref, yf_ref, yb_ref, z_ref, ng_ref, wo_ref, gf_ref, wg_ref, wu_ref, wd_ref) = refs[:11]
        rest = refs[11:]
    else:
        (x_ref, mod_ref, o_ref_in, wo_ref, gf_ref, wg_ref, wu_ref, wd_ref) = refs[:8]
        rest = refs[8:]
    if final:
        fg_ref, out_ref = rest
    else:
        (out_ref,) = rest
    m = mod_ref[0]
    if ssm:
        y = (yf_ref[0].astype(F32) + yb_ref[0].astype(F32)) * _silu(z_ref[0].astype(F32))
        parts = []
        for g in range(SSM_GROUPS):
            yg = y[:, g * GROUP_DIM:(g + 1) * GROUP_DIM]
            ms = jnp.mean(yg * yg, axis=-1, keepdims=True)
            parts.append((yg * lax.rsqrt(ms + EPS)
                          * ng_ref[:, g * GROUP_DIM:(g + 1) * GROUP_DIM]).astype(BF16))
        mixed = jnp.concatenate(parts, axis=1)
    else:
        mixed = o_ref_in[0]
    x1 = x_ref[0] + _mod_chunk(m, 2) * jnp.dot(mixed, wo_ref[...], preferred_element_type=F32)
    h2 = _rms_mod(x1, gf_ref[...], _mod_chunk(m, 4), _mod_chunk(m, 3)).astype(BF16)
    gate = jnp.dot(h2, wg_ref[...], preferred_element_type=F32)
    up = jnp.dot(h2, wu_ref[...], preferred_element_type=F32)
    act = (_silu(gate) * up).astype(BF16)
    x2 = x1 + _mod_chunk(m, 5) * jnp.dot(act, wd_ref[...], preferred_element_type=F32)
    if final:
        ms = jnp.mean(x2 * x2, axis=-1, keepdims=True)
        x2 = x2 * lax.rsqrt(ms + EPS) * fg_ref[...]
    out_ref[0] = x2


def _mix_ffn(x, mod, mod_row, mixer_in, w_mix, g_ffn, w_gate, w_up, w_down, ssm_norm_g=None,
             final_g=None):
    B, L, _ = x.shape
    tm = min(L, 256)
    ssm = ssm_norm_g is not None
    final = final_g is not None
    row = lambda width: pl.BlockSpec((1, tm, width), lambda b, i: (b, i, 0))
    in_specs = [row(D_MODEL), pl.BlockSpec((1, 1, 6 * D_MODEL), lambda b, i: (mod_row(b), 0, 0))]
    args = [x, mod]
    if ssm:
        in_specs += [row(D_INNER)] * 3 + [_const_spec((1, D_INNER))]
        args += list(mixer_in) + [ssm_norm_g]
    else:
        in_specs += [row(Q_DIM)]
        args += [mixer_in]
    in_specs += [_const_spec(w_mix.shape), _const_spec((1, D_MODEL)), _const_spec(w_gate.shape),
                 _const_spec(w_up.shape), _const_spec(w_down.shape)]
    args += [w_mix, g_ffn, w_gate, w_up, w_down]
    if final:
        in_specs += [_const_spec((1, D_MODEL))]
        args += [final_g]
    return pl.pallas_call(
        functools.partial(_mix_ffn_kernel, ssm=ssm, final=final),
        grid=(B, L // tm),
        in_specs=in_specs,
        out_specs=row(D_MODEL),
        out_shape=jax.ShapeDtypeStruct((B, L, D_MODEL), F32),
        compiler_params=pltpu.CompilerParams(
            dimension_semantics=("arbitrary", "arbitrary"), vmem_limit_bytes=VMEM_LIMIT),
        name="ssm_out_ffn" if ssm else "attn_out_ffn",
    )(*args)


def _inproj_kernel(x_ref, mod_ref, g_ref, wz_ref, wxbc_ref, wdt_ref, dtb_ref, z_ref, xbc_ref, dt_ref):
    m = mod_ref[0]
    h = _rms_mod(x_ref[0], g_ref[...], _mod_chunk(m, 1), _mod_chunk(m, 0)).astype(BF16)
    z_ref[0] = jnp.dot(h, wz_ref[...], preferred_element_type=F32).astype(BF16)
    xbc_ref[0] = jnp.dot(h, wxbc_ref[...], preferred_element_type=F32).astype(BF16)
    raw = jnp.dot(h, wdt_ref[...], preferred_element_type=F32) + dtb_ref[...]
    dt_ref[0] = jnp.maximum(raw, 0.0) + jnp.log1p(jnp.exp(-jnp.abs(raw)))


def _inproj(x, mod, mod_row, g, w_z, w_xbc, w_dt, dt_bias):
    B, L, _ = x.shape
    tm = min(L, 256)
    row = lambda width: pl.BlockSpec((1, tm, width), lambda b, i: (b, i, 0))
    return pl.pallas_call(
        _inproj_kernel,
        grid=(B, L // tm),
        in_specs=[row(D_MODEL),
                  pl.BlockSpec((1, 1, 6 * D_MODEL), lambda b, i: (mod_row(b), 0, 0)),
                  _const_spec((1, D_MODEL)), _const_spec(w_z.shape), _const_spec(w_xbc.shape),
                  _const_spec(w_dt.shape), _const_spec((1, 2 * LANES))],
        out_specs=[row(D_INNER), row(CONV_DIM), row(2 * LANES)],
        out_shape=[jax.ShapeDtypeStruct((B, L, D_INNER), BF16),
                   jax.ShapeDtypeStruct((B, L, CONV_DIM), BF16),
                   jax.ShapeDtypeStruct((B, L, 2 * LANES), F32)],
        compiler_params=pltpu.CompilerParams(
            dimension_semantics=("arbitrary", "arbitrary"), vmem_limit_bytes=VMEM_LIMIT),
        name="ssm_inproj",
    )(x, mod, g, w_z, w_xbc, w_dt, dt_bias)


def _split3(v):
    hi = v.astype(BF16)
    r1 = v - hi.astype(F32)
    mid = r1.astype(BF16)
    lo = (r1 - mid.astype(F32)).astype(BF16)
    return hi, mid, lo


def _ssd_kernel(*refs, reverse, nc, has_init, add_skip):
    (xm_ref, xp_ref, xn_ref, dt_ref, cw_ref, cb_ref, alog_ref, dskip_ref, e128_ref, e64_ref) = refs[:10]
    rest = refs[10:]
    if has_init:
        h0_ref, rest = rest[0], rest[1:]
    y_ref, hfin_ref, state, cbuf, xact, cbb, ce, tbuf = rest
    c = pl.program_id(1)
    cc = (nc - 1 - c) if reverse else c

    @pl.when(c == 0)
    def _():
        state[...] = h0_ref[0] if has_init else jnp.zeros(state.shape, F32)

    cbuf[0:HALO, :] = xp_ref[0].astype(F32) * jnp.where(cc > 0, 1.0, 0.0)
    cbuf[HALO:HALO + CHUNK, :] = xm_ref[0].astype(F32)
    cbuf[HALO + CHUNK:2 * HALO + CHUNK, :] = xn_ref[0].astype(F32) * jnp.where(cc < nc - 1, 1.0, 0.0)
    cstep = 512
    for j in range(CONV_DIM // cstep):
        sl = slice(j * cstep, (j + 1) * cstep)
        u = (cbuf[HALO - 1:HALO - 1 + CHUNK, sl] * cw_ref[0:1, sl]
             + cbuf[HALO:HALO + CHUNK, sl] * cw_ref[1:2, sl]
             + cbuf[HALO + 1:HALO + 1 + CHUNK, sl] * cw_ref[2:3, sl] + cb_ref[:, sl])
        xact[:, sl] = _silu(u)

    lane = lax.broadcasted_iota(jnp.int32, (CHUNK, LANES), 1)
    row = lax.broadcasted_iota(jnp.int32, (CHUNK, LANES), 0)
    dt = jnp.where(lane < SSM_HEADS, dt_ref[0], 0.0)
    a = dt * (-jnp.exp(alog_ref[0]))
    valid = (lane >= row) if reverse else (lane <= row)
    tri = jnp.where(valid, 1.0, 0.0).astype(BF16)
    a_hi, a_mid, a_lo = _split3(a)
    cum = (jnp.dot(tri, a_hi, preferred_element_type=F32)
           + jnp.dot(tri, a_mid, preferred_element_type=F32)
           + jnp.dot(tri, a_lo, preferred_element_type=F32))
    c_hi, c_mid, c_lo = _split3(cum)
    pieces = (c_hi.astype(F32) + pltpu.roll(c_mid.astype(F32), SSM_HEADS, 1)
              + pltpu.roll(c_lo.astype(F32), 2 * SSM_HEADS, 1)).astype(BF16)
    cbb[...] = jnp.dot(pieces, e128_ref[...], preferred_element_type=F32)
    ce[...] = jnp.dot(pieces, e64_ref[...], preferred_element_type=F32)
    tb = (cum + pltpu.roll(dt, SSM_HEADS, 1)).T
    tbuf[0:2 * SSM_HEADS, :] = tb[0:2 * SSM_HEADS, :]
    last = 0 if reverse else CHUNK - 1
    cum_t = tb[0:SSM_HEADS, :]
    total_t = cum_t[:, last:last + 1]
    tbuf[2 * SSM_HEADS:3 * SSM_HEADS, :] = jnp.exp(total_t - cum_t) * tb[SSM_HEADS:2 * SSM_HEADS, :]

    lane_g = lax.broadcasted_iota(jnp.int32, (CHUNK, GROUP_DIM), 1)
    for g in range(SSM_GROUPS):
        gs = slice(g * GROUP_DIM, (g + 1) * GROUP_DIM)
        b_g = xact[:, D_INNER + g * D_STATE:D_INNER + (g + 1) * D_STATE]
        c_g = xact[:, D_INNER + BC_DIM + g * D_STATE:D_INNER + BC_DIM + (g + 1) * D_STATE]
        c_bf = c_g.astype(BF16)
        cb_mat = lax.dot_general(c_bf, b_g.astype(BF16), (((1,), (1,)), ((), ())),
                                 preferred_element_type=F32)
        b_t = b_g.T
        x_g = xact[:, gs]
        y_g = (jnp.dot(c_bf, state[g].astype(BF16), preferred_element_type=F32)
               * jnp.exp(ce[:, gs]))
        if add_skip:
            y_g = y_g + dskip_ref[:, gs] * x_g
        st = jnp.zeros((D_STATE, GROUP_DIM), F32)
        for hp in range(HEADS_PER_GROUP // 2):
            lhs_y, lhs_s, rhs = [], [], []
            for hh in range(2):
                hl = 2 * hp + hh
                h = g * HEADS_PER_GROUP + hl
                diff = cbb[:, h * LANES:(h + 1) * LANES] - tbuf[h:h + 1, :]
                dec = jnp.exp(jnp.where(valid, diff, NEG_INF))
                lhs_y.append((dec * cb_mat * tbuf[SSM_HEADS + h:SSM_HEADS + h + 1, :]).astype(BF16))
                lhs_s.append((b_t * tbuf[2 * SSM_HEADS + h:2 * SSM_HEADS + h + 1, :]).astype(BF16))
                in_head = (lane_g >= hl * SSM_HEAD_DIM) & (lane_g < (hl + 1) * SSM_HEAD_DIM)
                rhs.append(jnp.where(in_head, x_g, 0.0).astype(BF16))
            rhs = jnp.concatenate(rhs, axis=0)
            y_g = y_g + jnp.dot(jnp.concatenate(lhs_y, axis=1), rhs, preferred_element_type=F32)
            st = st + jnp.dot(jnp.concatenate(lhs_s, axis=1), rhs, preferred_element_type=F32)
        y_ref[0, :, gs] = y_g.astype(BF16)
        state[g] = state[g] * jnp.exp(ce[last:last + 1, gs]) + st

    @pl.when(c == nc - 1)
    def _():
        hfin_ref[0] = state[...]


def _ssd(xbc, dt, conv_w, conv_b, a_log, d_skip, e128, e64, h0, direction):
    B, L, _ = xbc.shape
    nc = L // CHUNK
    reverse = direction == 1
    has_init = h0 is not None
    per = CHUNK // HALO
    chunk = (lambda c: nc - 1 - c) if reverse else (lambda c: c)
    st_shape = (SSM_GROUPS, D_STATE, GROUP_DIM)
    st_spec = pl.BlockSpec((1,) + st_shape, lambda b, c: (b, 0, 0, 0))
    in_specs = [
        pl.BlockSpec((1, CHUNK, CONV_DIM), lambda b, c: (b, chunk(c), 0)),
        pl.BlockSpec((1, HALO, CONV_DIM), lambda b, c: (b, jnp.maximum(chunk(c) * per - 1, 0), 0)),
        pl.BlockSpec((1, HALO, CONV_DIM),
                     lambda b, c: (b, jnp.minimum((chunk(c) + 1) * per, L // HALO - 1), 0)),
        pl.BlockSpec((1, CHUNK, LANES), lambda b, c: (b, chunk(c), direction)),
        _const_spec((8, CONV_DIM)), _const_spec((1, CONV_DIM)),
        pl.BlockSpec((1, 1, LANES), lambda b, c: (direction, 0, 0)),
        _const_spec((1, D_INNER)), _const_spec(e128.shape), _const_spec(e64.shape),
    ]
    args = [xbc, xbc, xbc, dt, conv_w, conv_b, a_log, d_skip, e128, e64]
    if has_init:
        in_specs.append(st_spec)
        args.append(h0)
    return pl.pallas_call(
        functools.partial(_ssd_kernel, reverse=reverse, nc=nc, has_init=has_init,
                          add_skip=not reverse),
        grid=(B, nc),
        in_specs=in_specs,
        out_specs=[pl.BlockSpec((1, CHUNK, D_INNER), lambda b, c: (b, chunk(c), 0)), st_spec],
        out_shape=[jax.ShapeDtypeStruct((B, L, D_INNER), BF16),
                   jax.ShapeDtypeStruct((B,) + st_shape, F32)],
        scratch_shapes=[
            pltpu.VMEM(st_shape, F32),
            pltpu.VMEM((CHUNK + 2 * HALO, CONV_DIM), F32),
            pltpu.VMEM((CHUNK, CONV_DIM), F32),
            pltpu.VMEM((CHUNK, SSM_HEADS * LANES), F32),
            pltpu.VMEM((CHUNK, D_INNER), F32),
            pltpu.VMEM((LANES, CHUNK), F32),
        ],
        compiler_params=pltpu.CompilerParams(
            dimension_semantics=("arbitrary", "arbitrary"), vmem_limit_bytes=VMEM_LIMIT),
        name="ssd_bwd" if reverse else "ssd_fwd",
    )(*args)


def _expand_matrix(width):
    k = jnp.arange(LANES)[:, None]
    col_head = (jnp.arange(SSM_HEADS * width) // width)[None, :]
    return ((k % SSM_HEADS == col_head) & (k < 3 * SSM_HEADS)).astype(BF16)


def kernel(x, c, ctx, c_ctx, ada_w, ada_b, norm_mix_g, norm_ffn_g, attn_w_qkv, attn_w_o, attn_sinks,
           ssm_w_in, ssm_conv_w, ssm_conv_b, ssm_dt_bias, ssm_A_log, ssm_D, ssm_norm_g, ssm_w_out,
           ffn_w_gate, ffn_w_up, ffn_w_down, final_norm_g):
    B, L, _ = x.shape
    assert x.shape == (B, L, D_MODEL) and B <= CTX_ROW and L % 512 == 0
    bf = lambda w: w.astype(BF16)
    row2 = lambda v: v.reshape(1, -1)

    cc = jnp.zeros((MOD_ROWS, D_MODEL), F32).at[:B].set(c).at[CTX_ROW].set(c_ctx)
    mod = _adaln(cc, ada_w, ada_b)
    mods = [mod[i].reshape(MOD_ROWS, 1, 6 * D_MODEL) for i in range(DEPTH)]
    x_row = lambda b: b
    ctx_row = lambda b: CTX_ROW

    w_qkv = bf(attn_w_qkv[0])
    g_mix = row2(norm_mix_g[0])
    q, k, v = _qkv(x, mods[0], g_mix, w_qkv, _rope_tables(L), x_row)
    qc, kc, vc = _qkv(ctx, mods[0], g_mix, w_qkv, None, ctx_row)
    sinks = attn_sinks[0].astype(F32)
    o = _attn(q, k, v, kc, vc, sinks)
    oc = _attn(qc, None, None, kc, vc, sinks)
    ffn0 = (bf(attn_w_o[0]), row2(norm_ffn_g[0]), bf(ffn_w_gate[0]), bf(ffn_w_up[0]), bf(ffn_w_down[0]))
    x = _mix_ffn(x, mods[0], x_row, o, *ffn0)
    ctx = _mix_ffn(ctx, mods[0], ctx_row, oc, *ffn0)

    w_in = ssm_w_in[0]
    w_z = bf(w_in[:, :D_INNER])
    w_xbc = bf(w_in[:, D_INNER:D_INNER + CONV_DIM])
    w_dt_raw = w_in[:, D_INNER + CONV_DIM:]
    w_dt = jnp.zeros((D_MODEL, 2 * LANES), F32)
    w_dt = w_dt.at[:, :SSM_HEADS].set(w_dt_raw[:, :SSM_HEADS])
    w_dt = bf(w_dt.at[:, LANES:LANES + SSM_HEADS].set(w_dt_raw[:, SSM_HEADS:]))
    dt_bias = jnp.zeros((2, LANES), F32).at[:, :SSM_HEADS].set(ssm_dt_bias[0]).reshape(1, 2 * LANES)
    a_log = jnp.zeros((2, 1, LANES), F32).at[:, 0, :SSM_HEADS].set(ssm_A_log[0])
    conv_w = jnp.zeros((8, CONV_DIM), F32).at[:3].set(ssm_conv_w[0])
    conv_b = row2(ssm_conv_b[0])
    d_skip = row2(jnp.repeat(ssm_D[0], SSM_HEAD_DIM))
    e128 = _expand_matrix(LANES)
    e64 = _expand_matrix(SSM_HEAD_DIM)
    g_mix1 = row2(norm_mix_g[1])

    z, xbc, dt = _inproj(x, mods[1], x_row, g_mix1, w_z, w_xbc, w_dt, dt_bias)
    _, xbc_c, dt_c = _inproj(ctx, mods[1], ctx_row, g_mix1, w_z, w_xbc, w_dt, dt_bias)
    ssd = functools.partial(_ssd, conv_w=conv_w, conv_b=conv_b, a_log=a_log, d_skip=d_skip,
                            e128=e128, e64=e64)
    _, h_f = ssd(xbc_c, dt_c, h0=None, direction=0)
    _, h_b = ssd(xbc_c, dt_c, h0=None, direction=1)
    y_f, _ = ssd(xbc, dt, h0=h_f, direction=0)
    y_b, _ = ssd(xbc, dt, h0=h_b, direction=1)
    return _mix_ffn(x, mods[1], x_row, (y_f, y_b, z), bf(ssm_w_out[0]), row2(norm_ffn_g[1]),
                    bf(ffn_w_gate[1]), bf(ffn_w_up[1]), bf(ffn_w_down[1]),
                    ssm_norm_g=row2(ssm_norm_g[0]), final_g=row2(final_norm_g))
```

```python
import functools
import math

import jax
import jax.numpy as jnp
from jax import lax
from jax.experimental import pallas as pl
from jax.experimental.pallas import tpu as pltpu

F32 = jnp.float32
BF16 = jnp.bfloat16

D_MODEL = 1024
DEPTH = 2
GRID_W = 64
EPS = 1e-6

HEAD_DIM = 64
N_HEADS = D_MODEL // HEAD_DIM
N_KV_HEADS = N_HEADS // 4
Q_PER_KV = N_HEADS // N_KV_HEADS
Q_DIM = N_HEADS * HEAD_DIM
KV_DIM = N_KV_HEADS * HEAD_DIM
QKV_DIM = Q_DIM + 2 * KV_DIM
WINDOW = 128
BLOCK = 128
ROPE_FREQS = HEAD_DIM // 4
ROPE_BASE = 10000.0

D_INNER = 2 * D_MODEL
SSM_HEAD_DIM = 64
SSM_HEADS = D_INNER // SSM_HEAD_DIM
SSM_GROUPS = 8
HEADS_PER_GROUP = SSM_HEADS // SSM_GROUPS
GROUP_DIM = D_INNER // SSM_GROUPS
D_STATE = 128
CHUNK = 128
BC_DIM = SSM_GROUPS * D_STATE
CONV_DIM = D_INNER + 2 * BC_DIM
D_FF = ((8 * D_MODEL // 3 + 255) // 256) * 256

LANES = 128
HALO = 16
MOD_ROWS = 8
CTX_ROW = 4
VMEM_LIMIT = 48 * 1024 * 1024
NEG_INF = float("-inf")


def _const_spec(shape):
    nd = len(shape)
    return pl.BlockSpec(shape, lambda *_: (0,) * nd, pipeline_mode=pl.Buffered(1))


def _silu(v):
    return v * jax.nn.sigmoid(v)


def _rms_mod(x, g, scale, shift):
    ms = jnp.mean(x * x, axis=-1, keepdims=True)
    return (x * lax.rsqrt(ms + EPS) * g) * (1.0 + scale) + shift


def _mod_chunk(m, i):
    return m[:, i * D_MODEL:(i + 1) * D_MODEL]


def _adaln_kernel(c_ref, w_ref, b_ref, o_ref):
    s = _silu(c_ref[...])
    o_ref[0] = jnp.dot(s, w_ref[0], preferred_element_type=F32,
                       precision=lax.Precision.HIGHEST) + b_ref[0]


def _adaln(cc, ada_w, ada_b):
    tn = 1536
    n = 6 * D_MODEL
    return pl.pallas_call(
        _adaln_kernel,
        grid=(DEPTH, n // tn),
        in_specs=[
            pl.BlockSpec((MOD_ROWS, D_MODEL), lambda i, j: (0, 0)),
            pl.BlockSpec((1, D_MODEL, tn), lambda i, j: (i, 0, j)),
            pl.BlockSpec((1, 1, tn), lambda i, j: (i, 0, j)),
        ],
        out_specs=pl.BlockSpec((1, MOD_ROWS, tn), lambda i, j: (i, 0, j)),
        out_shape=jax.ShapeDtypeStruct((DEPTH, MOD_ROWS, n), F32),
        compiler_params=pltpu.CompilerParams(
            dimension_semantics=("arbitrary", "arbitrary"), vmem_limit_bytes=VMEM_LIMIT),
        name="adaln",
    )(cc, ada_w, ada_b.reshape(DEPTH, 1, n))


_NT = (((1,), (1,)), ((), ()))
_TN = (((0,), (0,)), ((), ()))


def _qkv_kernel(*refs, rope):
    if rope:
        (x_ref, mod_ref, g_ref, wq_ref, wk_ref, wv_ref, cos_t_ref, sin_t_ref,
         cos_ref, sa_ref, sb_ref, qt_ref, k_ref, vt_ref) = refs
    else:
        x_ref, mod_ref, g_ref, wq_ref, wk_ref, wv_ref, qt_ref, k_ref, vt_ref = refs
    m = mod_ref[0]
    h = _rms_mod(x_ref[0], g_ref[...], _mod_chunk(m, 1), _mod_chunk(m, 0)).astype(BF16)
    q_t = lax.dot_general(wq_ref[...], h, _NT, preferred_element_type=F32)
    k = jnp.dot(h, wk_ref[...], preferred_element_type=F32)
    vt_ref[0] = lax.dot_general(wv_ref[...], h, _NT, preferred_element_type=F32).astype(BF16)
    scale = HEAD_DIM ** -0.5
    if rope:
        f = ROPE_FREQS
        for head in range(N_HEADS):
            for axis in range(2):
                r0 = head * HEAD_DIM + axis * 2 * f
                cs = cos_t_ref[axis * f:(axis + 1) * f, :]
                sn = sin_t_ref[axis * f:(axis + 1) * f, :]
                x1 = q_t[r0:r0 + f, :]
                x2 = q_t[r0 + f:r0 + 2 * f, :]
                qt_ref[0, r0:r0 + f, :] = ((x1 * cs - x2 * sn) * scale).astype(BF16)
                qt_ref[0, r0 + f:r0 + 2 * f, :] = ((x2 * cs + x1 * sn) * scale).astype(BF16)
        for blk in range(KV_DIM // LANES):
            t = k[:, blk * LANES:(blk + 1) * LANES]
            t = (t * cos_ref[...] + pltpu.roll(t, LANES - f, 1) * sa_ref[...]
                 + pltpu.roll(t, f, 1) * sb_ref[...])
            k_ref[0, :, blk * LANES:(blk + 1) * LANES] = t.astype(BF16)
    else:
        qt_ref[0] = (q_t * scale).astype(BF16)
        k_ref[0] = k.astype(BF16)


def _rope_tables(L):
    rows = L // GRID_W
    row = jnp.repeat(jnp.arange(rows, dtype=F32), GRID_W)
    col = jnp.tile(jnp.arange(GRID_W, dtype=F32), rows)
    inv = ROPE_BASE ** (-jnp.arange(ROPE_FREQS, dtype=F32) / ROPE_FREQS)
    ang_r = row[:, None] * inv
    ang_c = col[:, None] * inv
    zero = jnp.zeros_like(ang_r)

    def lanes(r_first, r_second, c_first, c_second):
        head = jnp.concatenate([r_first, r_second, c_first, c_second], axis=-1)
        return jnp.tile(head, (1, LANES // HEAD_DIM))

    cr, sr, cc, sc = jnp.cos(ang_r), jnp.sin(ang_r), jnp.cos(ang_c), jnp.sin(ang_c)
    cos_t = jnp.concatenate([cr, cc], axis=1).T
    sin_t = jnp.concatenate([sr, sc], axis=1).T
    return (cos_t, sin_t, lanes(cr, cr, cc, cc), lanes(-sr, zero, -sc, zero),
            lanes(zero, sr, zero, sc))


def _qkv(x, mod, g, w_q_t, w_k, w_v_t, tables, mod_row):
    B, L, _ = x.shape
    tm = min(L, 512)
    rope = tables is not None
    in_specs = [
        pl.BlockSpec((1, tm, D_MODEL), lambda b, i: (b, i, 0)),
        pl.BlockSpec((1, 1, 6 * D_MODEL), lambda b, i: (mod_row(b), 0, 0)),
        _const_spec((1, D_MODEL)),
        _const_spec(w_q_t.shape), _const_spec(w_k.shape), _const_spec(w_v_t.shape),
    ]
    args = [x, mod, g, w_q_t, w_k, w_v_t]
    if rope:
        in_specs += [pl.BlockSpec((2 * ROPE_FREQS, tm), lambda b, i: (0, i))] * 2
        in_specs += [pl.BlockSpec((tm, LANES), lambda b, i: (i, 0))] * 3
        args += list(tables)
    return pl.pallas_call(
        functools.partial(_qkv_kernel, rope=rope),
        grid=(B, L // tm),
        in_specs=in_specs,
        out_specs=[
            pl.BlockSpec((1, Q_DIM, tm), lambda b, i: (b, 0, i)),
            pl.BlockSpec((1, tm, KV_DIM), lambda b, i: (b, i, 0)),
            pl.BlockSpec((1, KV_DIM, tm), lambda b, i: (b, 0, i)),
        ],
        out_shape=[
            jax.ShapeDtypeStruct((B, Q_DIM, L), BF16),
            jax.ShapeDtypeStruct((B, L, KV_DIM), BF16),
            jax.ShapeDtypeStruct((B, KV_DIM, L), BF16),
        ],
        compiler_params=pltpu.CompilerParams(
            dimension_semantics=("arbitrary", "arbitrary"), vmem_limit_bytes=VMEM_LIMIT),
        name="qkv_rope" if rope else "qkv_ctx",
    )(*args)


def _attn_kernel(*refs, window, nq, n_ctx):
    if window:
        (sink_ref, qt_ref, kp_ref, kc_ref, kn_ref, vp_ref, vc_ref, vn_ref,
         kx_ref, vx_ref, ot_ref, kbuf, vtbuf) = refs
    else:
        sink_ref, qt_ref, kx_ref, vx_ref, ot_ref, kbuf, vtbuf = refs
    kbuf[0:n_ctx, :] = kx_ref[0]
    vtbuf[:, 0:n_ctx] = vx_ref[0]
    if window:
        for j, (kr, vr) in enumerate(((kp_ref, vp_ref), (kc_ref, vc_ref), (kn_ref, vn_ref))):
            kbuf[n_ctx + j * BLOCK:n_ctx + (j + 1) * BLOCK, :] = kr[0]
            vtbuf[:, n_ctx + j * BLOCK:n_ctx + (j + 1) * BLOCK] = vr[0]
        n = pl.program_id(1)
        kj = lax.broadcasted_iota(jnp.int32, (BLOCK, Q_PER_KV * BLOCK), 0)
        qi = lax.broadcasted_iota(jnp.int32, (BLOCK, Q_PER_KV * BLOCK), 1) % BLOCK
        bias_prev = jnp.where((kj >= qi) & (n > 0), 0.0, NEG_INF).astype(F32)
        bias_next = jnp.where((kj <= qi) & (n < nq - 1), 0.0, NEG_INF).astype(F32)
    lane = lax.broadcasted_iota(jnp.int32, (1, Q_PER_KV * BLOCK), 1)
    for g in range(N_KV_HEADS):
        cols = []
        for r in range(Q_PER_KV):
            h = g * Q_PER_KV + r
            parts = [qt_ref[0, h * HEAD_DIM:(h + 1) * HEAD_DIM, :]]
            if g > 0:
                parts.insert(0, jnp.zeros((g * HEAD_DIM, BLOCK), BF16))
            if g < N_KV_HEADS - 1:
                parts.append(jnp.zeros(((N_KV_HEADS - 1 - g) * HEAD_DIM, BLOCK), BF16))
            cols.append(jnp.concatenate(parts, axis=0) if len(parts) > 1 else parts[0])
        q_exp = jnp.concatenate(cols, axis=1)
        s = jnp.dot(kbuf[...], q_exp, preferred_element_type=F32)
        if window:
            s = jnp.concatenate([
                s[:n_ctx], s[n_ctx:n_ctx + BLOCK] + bias_prev,
                s[n_ctx + BLOCK:n_ctx + 2 * BLOCK], s[n_ctx + 2 * BLOCK:] + bias_next], axis=0)
        sink = jnp.full((1, Q_PER_KV * BLOCK), sink_ref[g * Q_PER_KV], F32)
        for r in range(1, Q_PER_KV):
            sink = jnp.where(lane >= r * BLOCK, sink_ref[g * Q_PER_KV + r], sink)
        mx = jnp.maximum(jnp.max(s, axis=0, keepdims=True), sink)
        p = jnp.exp(s - mx)
        denom = jnp.sum(p, axis=0, keepdims=True) + jnp.exp(sink - mx)
        o_t = jnp.dot(vtbuf[g * HEAD_DIM:(g + 1) * HEAD_DIM, :], p.astype(BF16),
                      preferred_element_type=F32) / denom
        for r in range(Q_PER_KV):
            h = g * Q_PER_KV + r
            ot_ref[0, h * HEAD_DIM:(h + 1) * HEAD_DIM, :] = o_t[:, r * BLOCK:(r + 1) * BLOCK].astype(BF16)


def _attn(q_t, k, v_t, kx, vx_t, sinks):
    B, _, L = q_t.shape
    C = kx.shape[1]
    nq = L // BLOCK
    window = k is not None
    smem = pl.BlockSpec(memory_space=pltpu.SMEM)
    q_spec = pl.BlockSpec((1, Q_DIM, BLOCK), lambda b, n: (b, 0, n))
    kx_spec = pl.BlockSpec((1, C, KV_DIM), lambda b, n: (b, 0, 0))
    vx_spec = pl.BlockSpec((1, KV_DIM, C), lambda b, n: (b, 0, 0))
    if window:
        prev = lambda n: jnp.maximum(n - 1, 0)
        nxt = lambda n: jnp.minimum(n + 1, nq - 1)
        k_spec = lambda f: pl.BlockSpec((1, BLOCK, KV_DIM), lambda b, n: (b, f(n), 0))
        v_spec = lambda f: pl.BlockSpec((1, KV_DIM, BLOCK), lambda b, n: (b, 0, f(n)))
        same = lambda n: n
        in_specs = [smem, q_spec, k_spec(prev), k_spec(same), k_spec(nxt),
                    v_spec(prev), v_spec(same), v_spec(nxt), kx_spec, vx_spec]
        args = (sinks, q_t, k, k, k, v_t, v_t, v_t, kx, vx_t)
        n_keys = C + 3 * BLOCK
    else:
        in_specs = [smem, q_spec, kx_spec, vx_spec]
        args = (sinks, q_t, kx, vx_t)
        n_keys = C
    return pl.pallas_call(
        functools.partial(_attn_kernel, window=window, nq=nq, n_ctx=C),
        grid=(B, nq),
        in_specs=in_specs,
        out_specs=pl.BlockSpec((1, Q_DIM, BLOCK), lambda b, n: (b, 0, n)),
        out_shape=jax.ShapeDtypeStruct((B, Q_DIM, L), BF16),
        scratch_shapes=[pltpu.VMEM((n_keys, KV_DIM), BF16), pltpu.VMEM((KV_DIM, n_keys), BF16)],
        compiler_params=pltpu.CompilerParams(
            dimension_semantics=("arbitrary", "arbitrary"), vmem_limit_bytes=VMEM_LIMIT),
        name="attn_window" if window else "attn_ctx",
    )(*args)


def _mix_ffn_kernel(*refs, ssm, final):
    if ssm:
        (x_ref, mod_ref, yf_ref, yb_ref, z_ref, ng_ref, wo_ref, gf_ref, wg_ref, wu_ref, wd_ref) = refs[:11]
        rest = refs[11:]
    else:
        (x_ref, mod_ref, o_ref_in, wo_ref, gf_ref, wg_ref, wu_ref, wd_ref) = refs[:8]
        rest = refs[8:]
    if final:
        fg_ref, out_ref = rest
    else:
        (out_ref,) = rest
    m = mod_ref[0]
    if ssm:
        y = (yf_ref[0].astype(F32) + yb_ref[0].astype(F32)) * _silu(z_ref[0].astype(F32))
        parts = []
        for g in range(SSM_GROUPS):
            yg = y[:, g * GROUP_DIM:(g + 1) * GROUP_DIM]
            ms = jnp.mean(yg * yg, axis=-1, keepdims=True)
            parts.append((yg * lax.rsqrt(ms + EPS)
                          * ng_ref[:, g * GROUP_DIM:(g + 1) * GROUP_DIM]).astype(BF16))
        mix = jnp.dot(jnp.concatenate(parts, axis=1), wo_ref[...], preferred_element_type=F32)
    else:
        mix = lax.dot_general(o_ref_in[0], wo_ref[...], _TN, preferred_element_type=F32)
    x1 = x_ref[0] + _mod_chunk(m, 2) * mix
    h2 = _rms_mod(x1, gf_ref[...], _mod_chunk(m, 4), _mod_chunk(m, 3)).astype(BF16)
    gate = jnp.dot(h2, wg_ref[...], preferred_element_type=F32)
    up = jnp.dot(h2, wu_ref[...], preferred_element_type=F32)
    act = (_silu(gate) * up).astype(BF16)
    x2 = x1 + _mod_chunk(m, 5) * jnp.dot(act, wd_ref[...], preferred_element_type=F32)
    if final:
        ms = jnp.mean(x2 * x2, axis=-1, keepdims=True)
        x2 = x2 * lax.rsqrt(ms + EPS) * fg_ref[...]
    out_ref[0] = x2


def _mix_ffn(x, mod, mod_row, mixer_in, w_mix, g_ffn, w_gate, w_up, w_down, ssm_norm_g=None,
             final_g=None):
    B, L, _ = x.shape
    tm = min(L, 256)
    ssm = ssm_norm_g is not None
    final = final_g is not None
    row = lambda width: pl.BlockSpec((1, tm, width), lambda b, i: (b, i, 0))
    in_specs = [row(D_MODEL), pl.BlockSpec((1, 1, 6 * D_MODEL), lambda b, i: (mod_row(b), 0, 0))]
    args = [x, mod]
    if ssm:
        in_specs += [row(D_INNER)] * 3 + [_const_spec((1, D_INNER))]
        args += list(mixer_in) + [ssm_norm_g]
    else:
        in_specs += [pl.BlockSpec((1, Q_DIM, tm), lambda b, i: (b, 0, i))]
        args += [mixer_in]
    in_specs += [_const_spec(w_mix.shape), _const_spec((1, D_MODEL)), _const_spec(w_gate.shape),
                 _const_spec(w_up.shape), _const_spec(w_down.shape)]
    args += [w_mix, g_ffn, w_gate, w_up, w_down]
    if final:
        in_specs += [_const_spec((1, D_MODEL))]
        args += [final_g]
    return pl.pallas_call(
        functools.partial(_mix_ffn_kernel, ssm=ssm, final=final),
        grid=(B, L // tm),
        in_specs=in_specs,
        out_specs=row(D_MODEL),
        out_shape=jax.ShapeDtypeStruct((B, L, D_MODEL), F32),
        compiler_params=pltpu.CompilerParams(
            dimension_semantics=("arbitrary", "arbitrary"), vmem_limit_bytes=VMEM_LIMIT),
        name="ssm_out_ffn" if ssm else "attn_out_ffn",
    )(*args)


def _inproj_kernel(x_ref, mod_ref, g_ref, wz_ref, wxbc_ref, wdt_ref, dtb_ref, z_ref, xbc_ref, dt_ref):
    m = mod_ref[0]
    h = _rms_mod(x_ref[0], g_ref[...], _mod_chunk(m, 1), _mod_chunk(m, 0)).astype(BF16)
    z_ref[0] = jnp.dot(h, wz_ref[...], preferred_element_type=F32).astype(BF16)
    xbc_ref[0] = jnp.dot(h, wxbc_ref[...], preferred_element_type=F32).astype(BF16)
    raw = jnp.dot(h, wdt_ref[...], preferred_element_type=F32) + dtb_ref[...]
    dt_ref[0] = jnp.maximum(raw, 0.0) + jnp.log1p(jnp.exp(-jnp.abs(raw)))


def _inproj(x, mod, mod_row, g, w_z, w_xbc, w_dt, dt_bias):
    B, L, _ = x.shape
    tm = min(L, 256)
    row = lambda width: pl.BlockSpec((1, tm, width), lambda b, i: (b, i, 0))
    return pl.pallas_call(
        _inproj_kernel,
        grid=(B, L // tm),
        in_specs=[row(D_MODEL),
                  pl.BlockSpec((1, 1, 6 * D_MODEL), lambda b, i: (mod_row(b), 0, 0)),
                  _const_spec((1, D_MODEL)), _const_spec(w_z.shape), _const_spec(w_xbc.shape),
                  _const_spec(w_dt.shape), _const_spec((1, 2 * LANES))],
        out_specs=[row(D_INNER), row(CONV_DIM), row(2 * LANES)],
        out_shape=[jax.ShapeDtypeStruct((B, L, D_INNER), BF16),
                   jax.ShapeDtypeStruct((B, L, CONV_DIM), BF16),
                   jax.ShapeDtypeStruct((B, L, 2 * LANES), F32)],
        compiler_params=pltpu.CompilerParams(
            dimension_semantics=("arbitrary", "arbitrary"), vmem_limit_bytes=VMEM_LIMIT),
        name="ssm_inproj",
    )(x, mod, g, w_z, w_xbc, w_dt, dt_bias)


def _split3(v):
    hi = v.astype(BF16)
    r1 = v - hi.astype(F32)
    mid = r1.astype(BF16)
    lo = (r1 - mid.astype(F32)).astype(BF16)
    return hi, mid, lo


def _ssd_kernel(*refs, reverse, nc, has_init, add_skip, conv):
    if conv:
        (xm_ref, xp_ref, xn_ref, cw_ref, cb_ref), rest = refs[:5], refs[5:]
    else:
        xa_ref, rest = refs[0], refs[1:]
    (dt_ref, alog_ref, dskip_ref, e128_ref, e64_ref), rest = rest[:5], rest[5:]
    if has_init:
        h0_ref, rest = rest[0], rest[1:]
    if conv:
        y_ref, hfin_ref, xa_out_ref, state, cbuf, xact, cbb, ce, tbuf = rest
    else:
        y_ref, hfin_ref, state, xact, cbb, ce, tbuf = rest
    c = pl.program_id(1)
    cc = (nc - 1 - c) if reverse else c

    @pl.when(c == 0)
    def _():
        state[...] = h0_ref[0] if has_init else jnp.zeros(state.shape, F32)

    if conv:
        cbuf[0:HALO, :] = xp_ref[0].astype(F32) * jnp.where(cc > 0, 1.0, 0.0)
        cbuf[HALO:HALO + CHUNK, :] = xm_ref[0].astype(F32)
        cbuf[HALO + CHUNK:2 * HALO + CHUNK, :] = (xn_ref[0].astype(F32)
                                                  * jnp.where(cc < nc - 1, 1.0, 0.0))
        cstep = 512
        for j in range(CONV_DIM // cstep):
            sl = slice(j * cstep, (j + 1) * cstep)
            u = (cbuf[HALO - 1:HALO - 1 + CHUNK, sl] * cw_ref[0:1, sl]
                 + cbuf[HALO:HALO + CHUNK, sl] * cw_ref[1:2, sl]
                 + cbuf[HALO + 1:HALO + 1 + CHUNK, sl] * cw_ref[2:3, sl] + cb_ref[:, sl])
            act = _silu(u)
            xact[:, sl] = act
            xa_out_ref[0, :, sl] = act.astype(BF16)
    else:
        xact[...] = xa_ref[0].astype(F32)

    lane = lax.broadcasted_iota(jnp.int32, (CHUNK, LANES), 1)
    row = lax.broadcasted_iota(jnp.int32, (CHUNK, LANES), 0)
    dt = jnp.where(lane < SSM_HEADS, dt_ref[0], 0.0)
    a = dt * (-jnp.exp(alog_ref[0]))
    valid = (lane >= row) if reverse else (lane <= row)
    tri = jnp.where(valid, 1.0, 0.0).astype(BF16)
    a_hi, a_mid, a_lo = _split3(a)
    cum = (jnp.dot(tri, a_hi, preferred_element_type=F32)
           + jnp.dot(tri, a_mid, preferred_element_type=F32)
           + jnp.dot(tri, a_lo, preferred_element_type=F32))
    c_hi, c_mid, c_lo = _split3(cum)
    pieces = (c_hi.astype(F32) + pltpu.roll(c_mid.astype(F32), SSM_HEADS, 1)
              + pltpu.roll(c_lo.astype(F32), 2 * SSM_HEADS, 1)).astype(BF16)
    cbb[...] = jnp.dot(pieces, e128_ref[...], preferred_element_type=F32)
    ce[...] = jnp.dot(pieces, e64_ref[...], preferred_element_type=F32)
    tb = (cum + pltpu.roll(dt, SSM_HEADS, 1)).T
    tbuf[0:2 * SSM_HEADS, :] = tb[0:2 * SSM_HEADS, :]
    last = 0 if reverse else CHUNK - 1
    cum_t = tb[0:SSM_HEADS, :]
    total_t = cum_t[:, last:last + 1]
    tbuf[2 * SSM_HEADS:3 * SSM_HEADS, :] = jnp.exp(total_t - cum_t) * tb[SSM_HEADS:2 * SSM_HEADS, :]

    lane_g = lax.broadcasted_iota(jnp.int32, (CHUNK, GROUP_DIM), 1)
    for g in range(SSM_GROUPS):
        gs = slice(g * GROUP_DIM, (g + 1) * GROUP_DIM)
        b_g = xact[:, D_INNER + g * D_STATE:D_INNER + (g + 1) * D_STATE]
        c_g = xact[:, D_INNER + BC_DIM + g * D_STATE:D_INNER + BC_DIM + (g + 1) * D_STATE]
        c_bf = c_g.astype(BF16)
        cb_mat = lax.dot_general(c_bf, b_g.astype(BF16), (((1,), (1,)), ((), ())),
                                 preferred_element_type=F32)
        b_t = b_g.T
        x_g = xact[:, gs]
        y_g = (jnp.dot(c_bf, state[g].astype(BF16), preferred_element_type=F32)
               * jnp.exp(ce[:, gs]))
        if add_skip:
            y_g = y_g + dskip_ref[:, gs] * x_g
        st = jnp.zeros((D_STATE, GROUP_DIM), F32)
        for hp in range(HEADS_PER_GROUP // 2):
            lhs_y, lhs_s, rhs = [], [], []
            for hh in range(2):
                hl = 2 * hp + hh
                h = g * HEADS_PER_GROUP + hl
                diff = cbb[:, h * LANES:(h + 1) * LANES] - tbuf[h:h + 1, :]
                dec = jnp.exp(jnp.where(valid, diff, NEG_INF))
                lhs_y.append((dec * cb_mat * tbuf[SSM_HEADS + h:SSM_HEADS + h + 1, :]).astype(BF16))
                lhs_s.append((b_t * tbuf[2 * SSM_HEADS + h:2 * SSM_HEADS + h + 1, :]).astype(BF16))
                in_head = (lane_g >= hl * SSM_HEAD_DIM) & (lane_g < (hl + 1) * SSM_HEAD_DIM)
                rhs.append(jnp.where(in_head, x_g, 0.0).astype(BF16))
            rhs = jnp.concatenate(rhs, axis=0)
            y_g = y_g + jnp.dot(jnp.concatenate(lhs_y, axis=1), rhs, preferred_element_type=F32)
            st = st + jnp.dot(jnp.concatenate(lhs_s, axis=1), rhs, preferred_element_type=F32)
        y_ref[0, :, gs] = y_g.astype(BF16)
        state[g] = state[g] * jnp.exp(ce[last:last + 1, gs]) + st

    @pl.when(c == nc - 1)
    def _():
        hfin_ref[0] = state[...]


def _ssd(xin, dt, conv, a_log, d_skip, e128, e64, h0, direction):
    B, L, _ = xin.shape
    nc = L // CHUNK
    reverse = direction == 1
    has_init = h0 is not None
    per = CHUNK // HALO
    chunk = (lambda c: nc - 1 - c) if reverse else (lambda c: c)
    st_shape = (SSM_GROUPS, D_STATE, GROUP_DIM)
    st_spec = pl.BlockSpec((1,) + st_shape, lambda b, c: (b, 0, 0, 0))
    main_spec = pl.BlockSpec((1, CHUNK, CONV_DIM), lambda b, c: (b, chunk(c), 0))
    if conv is not None:
        in_specs = [
            main_spec,
            pl.BlockSpec((1, HALO, CONV_DIM), lambda b, c: (b, jnp.maximum(chunk(c) * per - 1, 0), 0)),
            pl.BlockSpec((1, HALO, CONV_DIM),
                         lambda b, c: (b, jnp.minimum((chunk(c) + 1) * per, L // HALO - 1), 0)),
            _const_spec((8, CONV_DIM)), _const_spec((1, CONV_DIM)),
        ]
        args = [xin, xin, xin, conv[0], conv[1]]
    else:
        in_specs = [main_spec]
        args = [xin]
    in_specs += [
        pl.BlockSpec((1, CHUNK, LANES), lambda b, c: (b, chunk(c), direction)),
        pl.BlockSpec((1, 1, LANES), lambda b, c: (direction, 0, 0)),
        _const_spec((1, D_INNER)), _const_spec(e128.shape), _const_spec(e64.shape),
    ]
    args += [dt, a_log, d_skip, e128, e64]
    if has_init:
        in_specs.append(st_spec)
        args.append(h0)
    out_specs = [pl.BlockSpec((1, CHUNK, D_INNER), lambda b, c: (b, chunk(c), 0)), st_spec]
    out_shape = [jax.ShapeDtypeStruct((B, L, D_INNER), BF16),
                 jax.ShapeDtypeStruct((B,) + st_shape, F32)]
    scratch = [pltpu.VMEM(st_shape, F32)]
    if conv is not None:
        out_specs.append(main_spec)
        out_shape.append(jax.ShapeDtypeStruct((B, L, CONV_DIM), BF16))
        scratch.append(pltpu.VMEM((CHUNK + 2 * HALO, CONV_DIM), F32))
    scratch += [
        pltpu.VMEM((CHUNK, CONV_DIM), F32),
        pltpu.VMEM((CHUNK, SSM_HEADS * LANES), F32),
        pltpu.VMEM((CHUNK, D_INNER), F32),
        pltpu.VMEM((LANES, CHUNK), F32),
    ]
    return pl.pallas_call(
        functools.partial(_ssd_kernel, reverse=reverse, nc=nc, has_init=has_init,
                          add_skip=not reverse, conv=conv is not None),
        grid=(B, nc),
        in_specs=in_specs,
        out_specs=out_specs,
        out_shape=out_shape,
        scratch_shapes=scratch,
        compiler_params=pltpu.CompilerParams(
            dimension_semantics=("arbitrary", "arbitrary"), vmem_limit_bytes=VMEM_LIMIT),
        name="ssd_bwd" if reverse else "ssd_fwd",
    )(*args)


def _expand_matrix(width):
    k = jnp.arange(LANES)[:, None]
    col_head = (jnp.arange(SSM_HEADS * width) // width)[None, :]
    return ((k % SSM_HEADS == col_head) & (k < 3 * SSM_HEADS)).astype(BF16)


def kernel(x, c, ctx, c_ctx, ada_w, ada_b, norm_mix_g, norm_ffn_g, attn_w_qkv, attn_w_o, attn_sinks,
           ssm_w_in, ssm_conv_w, ssm_conv_b, ssm_dt_bias, ssm_A_log, ssm_D, ssm_norm_g, ssm_w_out,
           ffn_w_gate, ffn_w_up, ffn_w_down, final_norm_g):
    B, L, _ = x.shape
    assert x.shape == (B, L, D_MODEL) and B <= CTX_ROW and L % 512 == 0
    bf = lambda w: w.astype(BF16)
    row2 = lambda v: v.reshape(1, -1)

    cc = jnp.zeros((MOD_ROWS, D_MODEL), F32).at[:B].set(c).at[CTX_ROW].set(c_ctx)
    mod = _adaln(cc, ada_w, ada_b)
    mods = [mod[i].reshape(MOD_ROWS, 1, 6 * D_MODEL) for i in range(DEPTH)]
    x_row = lambda b: b
    ctx_row = lambda b: CTX_ROW

    w_qkv = attn_w_qkv[0]
    w_q_t = bf(w_qkv[:, :Q_DIM].T)
    w_k = bf(w_qkv[:, Q_DIM:Q_DIM + KV_DIM])
    w_v_t = bf(w_qkv[:, Q_DIM + KV_DIM:].T)
    g_mix = row2(norm_mix_g[0])
    q, k, v = _qkv(x, mods[0], g_mix, w_q_t, w_k, w_v_t, _rope_tables(L), x_row)
    qc, kc, vc = _qkv(ctx, mods[0], g_mix, w_q_t, w_k, w_v_t, None, ctx_row)
    sinks = attn_sinks[0].astype(F32)
    o = _attn(q, k, v, kc, vc, sinks)
    oc = _attn(qc, None, None, kc, vc, sinks)
    ffn0 = (bf(attn_w_o[0]), row2(norm_ffn_g[0]), bf(ffn_w_gate[0]), bf(ffn_w_up[0]), bf(ffn_w_down[0]))
    x = _mix_ffn(x, mods[0], x_row, o, *ffn0)
    ctx = _mix_ffn(ctx, mods[0], ctx_row, oc, *ffn0)

    w_in = ssm_w_in[0]
    w_z = bf(w_in[:, :D_INNER])
    w_xbc = bf(w_in[:, D_INNER:D_INNER + CONV_DIM])
    w_dt_raw = w_in[:, D_INNER + CONV_DIM:]
    w_dt = jnp.zeros((D_MODEL, 2 * LANES), F32)
    w_dt = w_dt.at[:, :SSM_HEADS].set(w_dt_raw[:, :SSM_HEADS])
    w_dt = bf(w_dt.at[:, LANES:LANES + SSM_HEADS].set(w_dt_raw[:, SSM_HEADS:]))
    dt_bias = jnp.zeros((2, LANES), F32).at[:, :SSM_HEADS].set(ssm_dt_bias[0]).reshape(1, 2 * LANES)
    a_log = jnp.zeros((2, 1, LANES), F32).at[:, 0, :SSM_HEADS].set(ssm_A_log[0])
    conv_w = jnp.zeros((8, CONV_DIM), F32).at[:3].set(ssm_conv_w[0])
    conv_b = row2(ssm_conv_b[0])
    d_skip = row2(jnp.repeat(ssm_D[0], SSM_HEAD_DIM))
    e128 = _expand_matrix(LANES)
    e64 = _expand_matrix(SSM_HEAD_DIM)
    g_mix1 = row2(norm_mix_g[1])

    z, xbc, dt = _inproj(x, mods[1], x_row, g_mix1, w_z, w_xbc, w_dt, dt_bias)
    _, xbc_c, dt_c = _inproj(ctx, mods[1], ctx_row, g_mix1, w_z, w_xbc, w_dt, dt_bias)
    ssd = functools.partial(_ssd, a_log=a_log, d_skip=d_skip, e128=e128, e64=e64)
    conv = (conv_w, conv_b)
    _, h_f, xact_c = ssd(xbc_c, dt_c, conv, h0=None, direction=0)
    _, h_b = ssd(xact_c, dt_c, None, h0=None, direction=1)
    y_f, _, xact = ssd(xbc, dt, conv, h0=h_f, direction=0)
    y_b, _ = ssd(xact, dt, None, h0=h_b, direction=1)
    return _mix_ffn(x, mods[1], x_row, (y_f, y_b, z), bf(ssm_w_out[0]), row2(norm_ffn_g[1]),
                    bf(ffn_w_gate[1]), bf(ffn_w_up[1]), bf(ffn_w_down[1]),
                    ssm_norm_g=row2(ssm_norm_g[0]), final_g=row2(final_norm_g))
```

```python
import functools
import math

import numpy as np

import jax
import jax.numpy as jnp
from jax import lax
from jax.experimental import pallas as pl
from jax.experimental.pallas import tpu as pltpu

F32 = jnp.float32
BF16 = jnp.bfloat16

D_MODEL = 1024
DEPTH = 2
GRID_W = 64
EPS = 1e-6

HEAD_DIM = 64
N_HEADS = D_MODEL // HEAD_DIM
N_KV_HEADS = N_HEADS // 4
Q_PER_KV = N_HEADS // N_KV_HEADS
Q_DIM = N_HEADS * HEAD_DIM
KV_DIM = N_KV_HEADS * HEAD_DIM
QKV_DIM = Q_DIM + 2 * KV_DIM
WINDOW = 128
BLOCK = 128
ROPE_FREQS = HEAD_DIM // 4
ROPE_BASE = 10000.0

D_INNER = 2 * D_MODEL
SSM_HEAD_DIM = 64
SSM_HEADS = D_INNER // SSM_HEAD_DIM
SSM_GROUPS = 8
HEADS_PER_GROUP = SSM_HEADS // SSM_GROUPS
GROUP_DIM = D_INNER // SSM_GROUPS
D_STATE = 128
CHUNK = 128
BC_DIM = SSM_GROUPS * D_STATE
CONV_DIM = D_INNER + 2 * BC_DIM
D_FF = ((8 * D_MODEL // 3 + 255) // 256) * 256

LANES = 128
HALO = 16
MOD_ROWS = 8
CTX_ROW = 4
VMEM_LIMIT = 48 * 1024 * 1024
NEG_INF = float("-inf")
LOG2E = math.log2(math.e)
FF_CHUNK = D_FF
HEADS_PER_DOT = N_HEADS


def _const_spec(shape):
    nd = len(shape)
    return pl.BlockSpec(shape, lambda *_: (0,) * nd, pipeline_mode=pl.Buffered(1))


def _silu(v):
    return v * jax.nn.sigmoid(v)


def _rms_mod(x, g, scale, shift):
    ms = jnp.mean(x * x, axis=-1, keepdims=True)
    return (x * lax.rsqrt(ms + EPS) * g) * (1.0 + scale) + shift


def _mod_chunk(m, i):
    return m[:, i * D_MODEL:(i + 1) * D_MODEL]


def _adaln_kernel(c_ref, w_ref, b_ref, o_ref):
    s = _silu(c_ref[...])
    o_ref[0] = jnp.dot(s, w_ref[0], preferred_element_type=F32) + b_ref[0]


def _adaln(cc, ada_w, ada_b):
    tn = 1536
    n = 6 * D_MODEL
    return pl.pallas_call(
        _adaln_kernel,
        grid=(DEPTH, n // tn),
        in_specs=[
            pl.BlockSpec((MOD_ROWS, D_MODEL), lambda i, j: (0, 0)),
            pl.BlockSpec((1, D_MODEL, tn), lambda i, j: (i, 0, j)),
            pl.BlockSpec((1, 1, tn), lambda i, j: (i, 0, j)),
        ],
        out_specs=pl.BlockSpec((1, MOD_ROWS, tn), lambda i, j: (i, 0, j)),
        out_shape=jax.ShapeDtypeStruct((DEPTH, MOD_ROWS, n), F32),
        compiler_params=pltpu.CompilerParams(
            dimension_semantics=("arbitrary", "arbitrary"), vmem_limit_bytes=VMEM_LIMIT),
        name="adaln",
    )(cc, ada_w, ada_b.reshape(DEPTH, 1, n))


_NT = (((1,), (1,)), ((), ()))
_TN = (((0,), (0,)), ((), ()))


def _qkv_kernel(*refs, rope):
    if rope:
        (x_ref, mod_ref, g_ref, wq_ref, wk_ref, wv_ref, cos_t_ref, sin_t_ref,
         cos_ref, sa_ref, sb_ref, qt_ref, k_ref, vt_ref) = refs
    else:
        x_ref, mod_ref, g_ref, wq_ref, wk_ref, wv_ref, qt_ref, k_ref, vt_ref = refs
    m = mod_ref[0]
    h = _rms_mod(x_ref[0], g_ref[...], _mod_chunk(m, 1), _mod_chunk(m, 0)).astype(BF16)
    q_t = lax.dot_general(wq_ref[...], h, _NT, preferred_element_type=F32)
    k = jnp.dot(h, wk_ref[...], preferred_element_type=F32)
    vt_ref[0] = lax.dot_general(wv_ref[...], h, _NT, preferred_element_type=F32).astype(BF16)
    scale = HEAD_DIM ** -0.5 * LOG2E
    if rope:
        f = ROPE_FREQS
        for head in range(N_HEADS):
            for axis in range(2):
                r0 = head * HEAD_DIM + axis * 2 * f
                cs = cos_t_ref[axis * f:(axis + 1) * f, :]
                sn = sin_t_ref[axis * f:(axis + 1) * f, :]
                x1 = q_t[r0:r0 + f, :]
                x2 = q_t[r0 + f:r0 + 2 * f, :]
                qt_ref[0, r0:r0 + f, :] = ((x1 * cs - x2 * sn) * scale).astype(BF16)
                qt_ref[0, r0 + f:r0 + 2 * f, :] = ((x2 * cs + x1 * sn) * scale).astype(BF16)
        for blk in range(KV_DIM // LANES):
            t = k[:, blk * LANES:(blk + 1) * LANES]
            t = (t * cos_ref[...] + pltpu.roll(t, LANES - f, 1) * sa_ref[...]
                 + pltpu.roll(t, f, 1) * sb_ref[...])
            k_ref[0, :, blk * LANES:(blk + 1) * LANES] = t.astype(BF16)
    else:
        qt_ref[0] = (q_t * scale).astype(BF16)
        k_ref[0] = k.astype(BF16)


def _rope_tables(L):
    f32 = np.float32
    rows = L // GRID_W
    row = np.repeat(np.arange(rows, dtype=f32), GRID_W)
    col = np.tile(np.arange(GRID_W, dtype=f32), rows)
    inv = (f32(ROPE_BASE) ** (-np.arange(ROPE_FREQS, dtype=f32) / f32(ROPE_FREQS))).astype(f32)
    ang_r = (row[:, None] * inv).astype(f32)
    ang_c = (col[:, None] * inv).astype(f32)
    zero = np.zeros_like(ang_r)

    def lanes(r_first, r_second, c_first, c_second):
        head = np.concatenate([r_first, r_second, c_first, c_second], axis=-1)
        return np.tile(head, (1, LANES // HEAD_DIM)).astype(f32)

    cr, sr, cc, sc = np.cos(ang_r), np.sin(ang_r), np.cos(ang_c), np.sin(ang_c)
    cos_t = np.ascontiguousarray(np.concatenate([cr, cc], axis=1).T).astype(f32)
    sin_t = np.ascontiguousarray(np.concatenate([sr, sc], axis=1).T).astype(f32)
    return (cos_t, sin_t, lanes(cr, cr, cc, cc), lanes(-sr, zero, -sc, zero),
            lanes(zero, sr, zero, sc))


def _qkv(x, mod, g, w_q_t, w_k, w_v_t, tables, mod_row):
    B, L, _ = x.shape
    tm = min(L, 512)
    rope = tables is not None
    in_specs = [
        pl.BlockSpec((1, tm, D_MODEL), lambda b, i: (b, i, 0)),
        pl.BlockSpec((1, 1, 6 * D_MODEL), lambda b, i: (mod_row(b), 0, 0)),
        _const_spec((1, D_MODEL)),
        _const_spec(w_q_t.shape), _const_spec(w_k.shape), _const_spec(w_v_t.shape),
    ]
    args = [x, mod, g, w_q_t, w_k, w_v_t]
    if rope:
        in_specs += [pl.BlockSpec((2 * ROPE_FREQS, tm), lambda b, i: (0, i))] * 2
        in_specs += [pl.BlockSpec((tm, LANES), lambda b, i: (i, 0))] * 3
        args += list(tables)
    return pl.pallas_call(
        functools.partial(_qkv_kernel, rope=rope),
        grid=(B, L // tm),
        in_specs=in_specs,
        out_specs=[
            pl.BlockSpec((1, Q_DIM, tm), lambda b, i: (b, 0, i)),
            pl.BlockSpec((1, tm, KV_DIM), lambda b, i: (b, i, 0)),
            pl.BlockSpec((1, KV_DIM, tm), lambda b, i: (b, 0, i)),
        ],
        out_shape=[
            jax.ShapeDtypeStruct((B, Q_DIM, L), BF16),
            jax.ShapeDtypeStruct((B, L, KV_DIM), BF16),
            jax.ShapeDtypeStruct((B, KV_DIM, L), BF16),
        ],
        compiler_params=pltpu.CompilerParams(
            dimension_semantics=("arbitrary", "arbitrary"), vmem_limit_bytes=VMEM_LIMIT),
        name="qkv_rope" if rope else "qkv_ctx",
    )(*args)


def _attn_kernel(*refs, window, nq, n_ctx):
    if window:
        (sink_ref, qt_ref, kp_ref, kc_ref, kn_ref, vp_ref, vc_ref, vn_ref,
         kx_ref, vx_ref, ot_ref, kbuf, vtbuf) = refs
    else:
        sink_ref, qt_ref, kx_ref, vx_ref, ot_ref, kbuf, vtbuf = refs
    width = HEADS_PER_DOT * BLOCK
    kbuf[0:n_ctx, :] = kx_ref[0]
    vtbuf[:, 0:n_ctx] = vx_ref[0]
    if window:
        for j, (kr, vr) in enumerate(((kp_ref, vp_ref), (kc_ref, vc_ref), (kn_ref, vn_ref))):
            kbuf[n_ctx + j * BLOCK:n_ctx + (j + 1) * BLOCK, :] = kr[0]
            vtbuf[:, n_ctx + j * BLOCK:n_ctx + (j + 1) * BLOCK] = vr[0]
        n = pl.program_id(1)
        kj = lax.broadcasted_iota(jnp.int32, (BLOCK, width), 0)
        qi = lax.broadcasted_iota(jnp.int32, (BLOCK, width), 1) % BLOCK
        bias_prev = jnp.where((kj >= qi) & (n > 0), 0.0, NEG_INF).astype(F32)
        bias_next = jnp.where((kj <= qi) & (n < nq - 1), 0.0, NEG_INF).astype(F32)
    lane = lax.broadcasted_iota(jnp.int32, (1, width), 1)
    ones_rows = jnp.ones((16, kbuf.shape[0]), BF16)
    for h0 in range(0, N_HEADS, HEADS_PER_DOT):
        cols = []
        for h in range(h0, h0 + HEADS_PER_DOT):
            g = h // Q_PER_KV
            parts = [qt_ref[0, h * HEAD_DIM:(h + 1) * HEAD_DIM, :]]
            if g > 0:
                parts.insert(0, jnp.zeros((g * HEAD_DIM, BLOCK), BF16))
            if g < N_KV_HEADS - 1:
                parts.append(jnp.zeros(((N_KV_HEADS - 1 - g) * HEAD_DIM, BLOCK), BF16))
            cols.append(jnp.concatenate(parts, axis=0) if len(parts) > 1 else parts[0])
        q_exp = jnp.concatenate(cols, axis=1) if len(cols) > 1 else cols[0]
        sink = jnp.full((1, width), sink_ref[h0], F32)
        for r in range(1, HEADS_PER_DOT):
            sink = jnp.where(lane >= r * BLOCK, sink_ref[h0 + r], sink)
        sink = sink * LOG2E
        s = jnp.dot(kbuf[...], q_exp, preferred_element_type=F32)
        if window:
            s = jnp.concatenate([
                s[:n_ctx], s[n_ctx:n_ctx + BLOCK] + bias_prev,
                s[n_ctx + BLOCK:n_ctx + 2 * BLOCK], s[n_ctx + 2 * BLOCK:] + bias_next], axis=0)
        mx = jnp.maximum(jnp.max(s, axis=0, keepdims=True), sink)
        p = jnp.exp2(s - mx).astype(BF16)
        sink_p = jnp.exp2(sink - mx)
        gw = Q_PER_KV * BLOCK
        for g0 in range(0, HEADS_PER_DOT // Q_PER_KV):
            g = h0 // Q_PER_KV + g0
            gl = slice(g0 * gw, (g0 + 1) * gw)
            v_ext = jnp.concatenate([vtbuf[g * HEAD_DIM:(g + 1) * HEAD_DIM, :], ones_rows], axis=0)
            pv = jnp.dot(v_ext, p[:, gl], preferred_element_type=F32)
            o_t = pv[:HEAD_DIM, :] / (pv[HEAD_DIM:HEAD_DIM + 1, :] + sink_p[:, gl])
            for r in range(Q_PER_KV):
                h = g * Q_PER_KV + r
                ot_ref[0, h * HEAD_DIM:(h + 1) * HEAD_DIM, :] = (
                    o_t[:, r * BLOCK:(r + 1) * BLOCK].astype(BF16))


def _attn(q_t, k, v_t, kx, vx_t, sinks):
    B, _, L = q_t.shape
    C = kx.shape[1]
    nq = L // BLOCK
    window = k is not None
    smem = pl.BlockSpec(memory_space=pltpu.SMEM)
    q_spec = pl.BlockSpec((1, Q_DIM, BLOCK), lambda b, n: (b, 0, n))
    kx_spec = pl.BlockSpec((1, C, KV_DIM), lambda b, n: (b, 0, 0))
    vx_spec = pl.BlockSpec((1, KV_DIM, C), lambda b, n: (b, 0, 0))
    if window:
        prev = lambda n: jnp.maximum(n - 1, 0)
        nxt = lambda n: jnp.minimum(n + 1, nq - 1)
        k_spec = lambda f: pl.BlockSpec((1, BLOCK, KV_DIM), lambda b, n: (b, f(n), 0))
        v_spec = lambda f: pl.BlockSpec((1, KV_DIM, BLOCK), lambda b, n: (b, 0, f(n)))
        same = lambda n: n
        in_specs = [smem, q_spec, k_spec(prev), k_spec(same), k_spec(nxt),
                    v_spec(prev), v_spec(same), v_spec(nxt), kx_spec, vx_spec]
        args = (sinks, q_t, k, k, k, v_t, v_t, v_t, kx, vx_t)
        n_keys = C + 3 * BLOCK
    else:
        in_specs = [smem, q_spec, kx_spec, vx_spec]
        args = (sinks, q_t, kx, vx_t)
        n_keys = C
    return pl.pallas_call(
        functools.partial(_attn_kernel, window=window, nq=nq, n_ctx=C),
        grid=(B, nq),
        in_specs=in_specs,
        out_specs=pl.BlockSpec((1, Q_DIM, BLOCK), lambda b, n: (b, 0, n)),
        out_shape=jax.ShapeDtypeStruct((B, Q_DIM, L), BF16),
        scratch_shapes=[pltpu.VMEM((n_keys, KV_DIM), BF16), pltpu.VMEM((KV_DIM, n_keys), BF16)],
        compiler_params=pltpu.CompilerParams(
            dimension_semantics=("arbitrary", "arbitrary"), vmem_limit_bytes=VMEM_LIMIT),
        name="attn_window" if window else "attn_ctx",
    )(*args)


def _mix_ffn_kernel(*refs, ssm, final, n_sub):
    if ssm:
        (x_ref, mod_ref, yf_ref, yb_ref, z_ref, xs_ref, dsk_ref, ng_ref,
         wo_ref, gf_ref, wg_ref, wu_ref, wd_ref) = refs[:13]
        rest = refs[13:]
    else:
        (x_ref, mod_ref, o_ref_in, wo_ref, gf_ref, wg_ref, wu_ref, wd_ref) = refs[:8]
        rest = refs[8:]
    if final:
        fg_ref, out_ref = rest
    else:
        (out_ref,) = rest
    m = mod_ref[0]
    ts = x_ref.shape[1] // n_sub
    for sub in range(n_sub):
        rs = slice(sub * ts, (sub + 1) * ts)
        if ssm:
            y = (yf_ref[0, rs, :].astype(F32) + yb_ref[0, rs, :].astype(F32)
                 + dsk_ref[...] * xs_ref[0, rs, :].astype(F32)) * _silu(z_ref[0, rs, :].astype(F32))
            parts = []
            for g in range(SSM_GROUPS):
                yg = y[:, g * GROUP_DIM:(g + 1) * GROUP_DIM]
                ms = jnp.mean(yg * yg, axis=-1, keepdims=True)
                parts.append((yg * lax.rsqrt(ms + EPS)
                              * ng_ref[:, g * GROUP_DIM:(g + 1) * GROUP_DIM]).astype(BF16))
            mix = jnp.dot(jnp.concatenate(parts, axis=1), wo_ref[...], preferred_element_type=F32)
        else:
            mix = lax.dot_general(o_ref_in[0, :, rs], wo_ref[...], _TN, preferred_element_type=F32)
        x1 = x_ref[0, rs, :] + _mod_chunk(m, 2) * mix
        h2 = _rms_mod(x1, gf_ref[...], _mod_chunk(m, 4), _mod_chunk(m, 3)).astype(BF16)
        ffn = None
        for f0 in range(0, D_FF, FF_CHUNK):
            fs = slice(f0, min(f0 + FF_CHUNK, D_FF))
            gate = jnp.dot(h2, wg_ref[:, fs], preferred_element_type=F32)
            up = jnp.dot(h2, wu_ref[:, fs], preferred_element_type=F32)
            act = (_silu(gate) * up).astype(BF16)
            part = jnp.dot(act, wd_ref[fs, :], preferred_element_type=F32)
            ffn = part if ffn is None else ffn + part
        x2 = x1 + _mod_chunk(m, 5) * ffn
        if final:
            ms = jnp.mean(x2 * x2, axis=-1, keepdims=True)
            x2 = x2 * lax.rsqrt(ms + EPS) * fg_ref[...]
        out_ref[0, rs, :] = x2


def _mix_ffn(x, mod, mod_row, mixer_in, w_mix, g_ffn, w_gate, w_up, w_down, ssm_norm_g=None,
             d_skip=None, final_g=None, tm=256, n_sub=1):
    B, L, _ = x.shape
    tm = min(L, tm)
    ssm = ssm_norm_g is not None
    final = final_g is not None
    row = lambda width: pl.BlockSpec((1, tm, width), lambda b, i: (b, i, 0))
    in_specs = [row(D_MODEL), pl.BlockSpec((1, 1, 6 * D_MODEL), lambda b, i: (mod_row(b), 0, 0))]
    args = [x, mod]
    if ssm:
        in_specs += [row(D_INNER)] * 4 + [_const_spec((1, D_INNER))] * 2
        args += list(mixer_in) + [d_skip, ssm_norm_g]
    else:
        in_specs += [pl.BlockSpec((1, Q_DIM, tm), lambda b, i: (b, 0, i))]
        args += [mixer_in]
    in_specs += [_const_spec(w_mix.shape), _const_spec((1, D_MODEL)), _const_spec(w_gate.shape),
                 _const_spec(w_up.shape), _const_spec(w_down.shape)]
    args += [w_mix, g_ffn, w_gate, w_up, w_down]
    if final:
        in_specs += [_const_spec((1, D_MODEL))]
        args += [final_g]
    return pl.pallas_call(
        functools.partial(_mix_ffn_kernel, ssm=ssm, final=final, n_sub=n_sub),
        grid=(B, L // tm),
        in_specs=in_specs,
        out_specs=row(D_MODEL),
        out_shape=jax.ShapeDtypeStruct((B, L, D_MODEL), F32),
        compiler_params=pltpu.CompilerParams(
            dimension_semantics=("arbitrary", "arbitrary"), vmem_limit_bytes=VMEM_LIMIT),
        name="ssm_out_ffn" if ssm else "attn_out_ffn",
    )(*args)


def _inproj_kernel(x_ref, mod_ref, g_ref, wz_ref, wxbc_ref, wdt_ref, dtb_ref, z_ref, xbc_ref, dt_ref):
    m = mod_ref[0]
    h = _rms_mod(x_ref[0], g_ref[...], _mod_chunk(m, 1), _mod_chunk(m, 0)).astype(BF16)
    z_ref[0] = jnp.dot(h, wz_ref[...], preferred_element_type=F32).astype(BF16)
    xbc_ref[0] = jnp.dot(h, wxbc_ref[...], preferred_element_type=F32).astype(BF16)
    raw = jnp.dot(h, wdt_ref[...], preferred_element_type=F32) + dtb_ref[...]
    dt_ref[0] = jnp.maximum(raw, 0.0) + jnp.log1p(jnp.exp(-jnp.abs(raw)))


def _inproj(x, mod, mod_row, g, w_z, w_xbc, w_dt, dt_bias):
    B, L, _ = x.shape
    tm = min(L, 256)
    row = lambda width: pl.BlockSpec((1, tm, width), lambda b, i: (b, i, 0))
    return pl.pallas_call(
        _inproj_kernel,
        grid=(B, L // tm),
        in_specs=[row(D_MODEL),
                  pl.BlockSpec((1, 1, 6 * D_MODEL), lambda b, i: (mod_row(b), 0, 0)),
                  _const_spec((1, D_MODEL)), _const_spec(w_z.shape), _const_spec(w_xbc.shape),
                  _const_spec(w_dt.shape), _const_spec((1, 2 * LANES))],
        out_specs=[row(D_INNER), row(CONV_DIM), row(2 * LANES)],
        out_shape=[jax.ShapeDtypeStruct((B, L, D_INNER), BF16),
                   jax.ShapeDtypeStruct((B, L, CONV_DIM), BF16),
                   jax.ShapeDtypeStruct((B, L, 2 * LANES), F32)],
        compiler_params=pltpu.CompilerParams(
            dimension_semantics=("arbitrary", "arbitrary"), vmem_limit_bytes=VMEM_LIMIT),
        name="ssm_inproj",
    )(x, mod, g, w_z, w_xbc, w_dt, dt_bias)


def _split3(v):
    hi = v.astype(BF16)
    r1 = v - hi.astype(F32)
    mid = r1.astype(BF16)
    lo = (r1 - mid.astype(F32)).astype(BF16)
    return hi, mid, lo


def _ssd_kernel(*refs, reverse, nc, has_init, want_y, conv):
    if conv:
        (xm_ref, xp_ref, xn_ref, cw_ref, cb_ref), rest = refs[:5], refs[5:]
    else:
        xa_ref, rest = refs[0], refs[1:]
    (dt_ref, alog_ref, e64_ref), rest = rest[:3], rest[3:]
    if has_init:
        h0_ref, rest = rest[0], rest[1:]
    if want_y:
        y_ref, rest = rest[0], rest[1:]
    hfin_ref, rest = rest[0], rest[1:]
    if conv:
        xa_ref, state, ce, tbuf = rest
    else:
        state, ce, tbuf = rest
    c = pl.program_id(1)
    cc = (nc - 1 - c) if reverse else c

    @pl.when(c == 0)
    def _():
        state[...] = h0_ref[0] if has_init else jnp.zeros(state.shape, F32)

    if conv:
        k_ext = 2 * CHUNK
        tt = lax.broadcasted_iota(jnp.int32, (CHUNK, k_ext), 0)
        kk = lax.broadcasted_iota(jnp.int32, (CHUNK, k_ext), 1)
        take_prev = ((kk == tt - 1)
                     | ((tt == 0) & (kk == CHUNK + HALO - 1) & (cc > 0)))
        take_next = (((kk == tt + 1) & (tt < CHUNK - 1))
                     | ((tt == CHUNK - 1) & (kk == CHUNK + HALO) & (cc < nc - 1)))
        sel_prev = jnp.where(take_prev, 1.0, 0.0).astype(BF16)
        sel_next = jnp.where(take_next, 1.0, 0.0).astype(BF16)
        cstep = 512
        pad = jnp.zeros((k_ext - CHUNK - 2 * HALO, cstep), BF16)
        for j in range(CONV_DIM // cstep):
            sl = slice(j * cstep, (j + 1) * cstep)
            mid = xm_ref[0, :, sl]
            ext = jnp.concatenate([mid, xp_ref[0, :, sl], xn_ref[0, :, sl], pad], axis=0)
            u = (jnp.dot(sel_prev, ext, preferred_element_type=F32) * cw_ref[0:1, sl]
                 + mid.astype(F32) * cw_ref[1:2, sl]
                 + jnp.dot(sel_next, ext, preferred_element_type=F32) * cw_ref[2:3, sl]
                 + cb_ref[:, sl])
            xa_ref[0, :, sl] = _silu(u).astype(BF16)

    lane = lax.broadcasted_iota(jnp.int32, (CHUNK, LANES), 1)
    row = lax.broadcasted_iota(jnp.int32, (CHUNK, LANES), 0)
    dt = jnp.where(lane < SSM_HEADS, dt_ref[0], 0.0)
    a = dt * (-jnp.exp(alog_ref[0]))
    valid = (lane >= row) if reverse else (lane <= row)
    tri = jnp.where(valid, 1.0, 0.0).astype(BF16)
    a_hi, a_mid, a_lo = _split3(a)
    cum = (jnp.dot(tri, a_hi, preferred_element_type=F32)
           + jnp.dot(tri, a_mid, preferred_element_type=F32)
           + jnp.dot(tri, a_lo, preferred_element_type=F32))
    c_hi, c_mid, c_lo = _split3(cum)
    pieces = (c_hi.astype(F32) + pltpu.roll(c_mid.astype(F32), SSM_HEADS, 1)
              + pltpu.roll(c_lo.astype(F32), 2 * SSM_HEADS, 1)).astype(BF16)
    ce[...] = jnp.dot(pieces, e64_ref[...], preferred_element_type=F32)
    tb = (cum + pltpu.roll(dt, SSM_HEADS, 1)).T
    last = 0 if reverse else CHUNK - 1
    cum_t = tb[0:SSM_HEADS, :]
    dt_t = tb[SSM_HEADS:2 * SSM_HEADS, :]
    tbuf[0:SSM_HEADS, :] = cum_t - jnp.log(dt_t)
    tbuf[SSM_HEADS:2 * SSM_HEADS, :] = jnp.exp(cum_t[:, last:last + 1] - cum_t) * dt_t

    lane_g = lax.broadcasted_iota(jnp.int32, (CHUNK, GROUP_DIM), 1)
    head_mask = [jnp.where((lane_g >= hl * SSM_HEAD_DIM) & (lane_g < (hl + 1) * SSM_HEAD_DIM),
                           1.0, 0.0).astype(BF16) for hl in range(HEADS_PER_GROUP)]
    low_half = lane < SSM_HEAD_DIM
    for g in range(SSM_GROUPS):
        gs = slice(g * GROUP_DIM, (g + 1) * GROUP_DIM)
        b_g = xa_ref[0, :, D_INNER + g * D_STATE:D_INNER + (g + 1) * D_STATE]
        c_g = xa_ref[0, :, D_INNER + BC_DIM + g * D_STATE:D_INNER + BC_DIM + (g + 1) * D_STATE]
        b_t = b_g.astype(F32).T
        x_g = xa_ref[0, :, gs]
        if want_y:
            cb_mat = lax.dot_general(c_g, b_g, _NT, preferred_element_type=F32)
            y_g = (jnp.dot(c_g, state[g].astype(BF16), preferred_element_type=F32)
                   * jnp.exp(ce[:, gs]))
        st = jnp.zeros((D_STATE, GROUP_DIM), F32)
        for hp in range(HEADS_PER_GROUP // 2):
            pair = g * (HEADS_PER_GROUP // 2) + hp
            lhs_y, lhs_s, rhs = [], [], []
            if want_y:
                ce_pair = ce[:, pair * LANES:(pair + 1) * LANES]
                swapped = pltpu.roll(ce_pair, SSM_HEAD_DIM, 1)
            for hh in range(2):
                hl = 2 * hp + hh
                h = g * HEADS_PER_GROUP + hl
                if want_y:
                    own = low_half if hh == 0 else jnp.logical_not(low_half)
                    cum_i = jnp.where(own, ce_pair, swapped)
                    dec = jnp.exp(jnp.where(valid, cum_i - tbuf[h:h + 1, :], NEG_INF))
                    lhs_y.append((dec * cb_mat).astype(BF16))
                lhs_s.append((b_t * tbuf[SSM_HEADS + h:SSM_HEADS + h + 1, :]).astype(BF16))
                rhs.append(x_g * head_mask[hl])
            rhs = jnp.concatenate(rhs, axis=0)
            if want_y:
                y_g = y_g + jnp.dot(jnp.concatenate(lhs_y, axis=1), rhs, preferred_element_type=F32)
            st = st + jnp.dot(jnp.concatenate(lhs_s, axis=1), rhs, preferred_element_type=F32)
        if want_y:
            y_ref[0, :, gs] = y_g.astype(BF16)
        state[g] = state[g] * jnp.exp(ce[last:last + 1, gs]) + st

    @pl.when(c == nc - 1)
    def _():
        hfin_ref[0] = state[...]


def _ssd(xin, dt, conv, a_log, e64, h0, direction, want_y):
    B, L, _ = xin.shape
    nc = L // CHUNK
    reverse = direction == 1
    has_init = h0 is not None
    per = CHUNK // HALO
    chunk = (lambda c: nc - 1 - c) if reverse else (lambda c: c)
    st_shape = (SSM_GROUPS, D_STATE, GROUP_DIM)
    st_spec = pl.BlockSpec((1,) + st_shape, lambda b, c: (b, 0, 0, 0))
    main_spec = pl.BlockSpec((1, CHUNK, CONV_DIM), lambda b, c: (b, chunk(c), 0))
    if conv is not None:
        in_specs = [
            main_spec,
            pl.BlockSpec((1, HALO, CONV_DIM), lambda b, c: (b, jnp.maximum(chunk(c) * per - 1, 0), 0)),
            pl.BlockSpec((1, HALO, CONV_DIM),
                         lambda b, c: (b, jnp.minimum((chunk(c) + 1) * per, L // HALO - 1), 0)),
            _const_spec((8, CONV_DIM)), _const_spec((1, CONV_DIM)),
        ]
        args = [xin, xin, xin, conv[0], conv[1]]
    else:
        in_specs = [main_spec]
        args = [xin]
    in_specs += [
        pl.BlockSpec((1, CHUNK, LANES), lambda b, c: (b, chunk(c), direction)),
        pl.BlockSpec((1, 1, LANES), lambda b, c: (direction, 0, 0)),
        _const_spec(e64.shape),
    ]
    args += [dt, a_log, e64]
    if has_init:
        in_specs.append(st_spec)
        args.append(h0)
    out_specs, out_shape = [], []
    if want_y:
        out_specs.append(pl.BlockSpec((1, CHUNK, D_INNER), lambda b, c: (b, chunk(c), 0)))
        out_shape.append(jax.ShapeDtypeStruct((B, L, D_INNER), BF16))
    out_specs.append(st_spec)
    out_shape.append(jax.ShapeDtypeStruct((B,) + st_shape, F32))
    scratch = [pltpu.VMEM(st_shape, F32)]
    if conv is not None:
        out_specs.append(main_spec)
        out_shape.append(jax.ShapeDtypeStruct((B, L, CONV_DIM), BF16))
    scratch += [pltpu.VMEM((CHUNK, D_INNER), F32), pltpu.VMEM((2 * SSM_HEADS, CHUNK), F32)]
    return pl.pallas_call(
        functools.partial(_ssd_kernel, reverse=reverse, nc=nc, has_init=has_init,
                          want_y=want_y, conv=conv is not None),
        grid=(B, nc),
        in_specs=in_specs,
        out_specs=out_specs,
        out_shape=out_shape,
        scratch_shapes=scratch,
        compiler_params=pltpu.CompilerParams(
            dimension_semantics=("arbitrary", "arbitrary"), vmem_limit_bytes=VMEM_LIMIT),
        name="ssd_bwd" if reverse else "ssd_fwd",
    )(*args)


def _expand_matrix(width):
    k = np.arange(LANES)[:, None]
    col_head = (np.arange(SSM_HEADS * width) // width)[None, :]
    return jnp.asarray((k % SSM_HEADS == col_head) & (k < 3 * SSM_HEADS), dtype=BF16)


def kernel(x, c, ctx, c_ctx, ada_w, ada_b, norm_mix_g, norm_ffn_g, attn_w_qkv, attn_w_o, attn_sinks,
           ssm_w_in, ssm_conv_w, ssm_conv_b, ssm_dt_bias, ssm_A_log, ssm_D, ssm_norm_g, ssm_w_out,
           ffn_w_gate, ffn_w_up, ffn_w_down, final_norm_g):
    B, L, _ = x.shape
    assert x.shape == (B, L, D_MODEL) and B <= CTX_ROW and L % 512 == 0
    bf = lambda w: w.astype(BF16)
    row2 = lambda v: v.reshape(1, -1)

    cc = jnp.zeros((MOD_ROWS, D_MODEL), F32).at[:B].set(c).at[CTX_ROW].set(c_ctx)
    mod = _adaln(cc, ada_w, ada_b)
    mods = [mod[i].reshape(MOD_ROWS, 1, 6 * D_MODEL) for i in range(DEPTH)]
    x_row = lambda b: b
    ctx_row = lambda b: CTX_ROW

    w_qkv = attn_w_qkv[0]
    w_q_t = bf(w_qkv[:, :Q_DIM].T)
    w_k = bf(w_qkv[:, Q_DIM:Q_DIM + KV_DIM])
    w_v_t = bf(w_qkv[:, Q_DIM + KV_DIM:].T)
    g_mix = row2(norm_mix_g[0])
    q, k, v = _qkv(x, mods[0], g_mix, w_q_t, w_k, w_v_t, _rope_tables(L), x_row)
    qc, kc, vc = _qkv(ctx, mods[0], g_mix, w_q_t, w_k, w_v_t, None, ctx_row)
    sinks = attn_sinks[0].astype(F32)
    o = _attn(q, k, v, kc, vc, sinks)
    oc = _attn(qc, None, None, kc, vc, sinks)
    ffn0 = (bf(attn_w_o[0]), row2(norm_ffn_g[0]), bf(ffn_w_gate[0]), bf(ffn_w_up[0]), bf(ffn_w_down[0]))
    x = _mix_ffn(x, mods[0], x_row, o, *ffn0)
    ctx = _mix_ffn(ctx, mods[0], ctx_row, oc, *ffn0)

    w_in = ssm_w_in[0]
    w_z = bf(w_in[:, :D_INNER])
    w_xbc = bf(w_in[:, D_INNER:D_INNER + CONV_DIM])
    w_dt_raw = w_in[:, D_INNER + CONV_DIM:]
    w_dt = jnp.zeros((D_MODEL, 2 * LANES), F32)
    w_dt = w_dt.at[:, :SSM_HEADS].set(w_dt_raw[:, :SSM_HEADS])
    w_dt = bf(w_dt.at[:, LANES:LANES + SSM_HEADS].set(w_dt_raw[:, SSM_HEADS:]))
    dt_bias = jnp.zeros((2, LANES), F32).at[:, :SSM_HEADS].set(ssm_dt_bias[0]).reshape(1, 2 * LANES)
    a_log = jnp.zeros((2, 1, LANES), F32).at[:, 0, :SSM_HEADS].set(ssm_A_log[0])
    conv_w = jnp.zeros((8, CONV_DIM), F32).at[:3].set(ssm_conv_w[0])
    conv_b = row2(ssm_conv_b[0])
    d_skip = row2(jnp.repeat(ssm_D[0], SSM_HEAD_DIM))
    e64 = _expand_matrix(SSM_HEAD_DIM)
    g_mix1 = row2(norm_mix_g[1])

    z, xbc, dt = _inproj(x, mods[1], x_row, g_mix1, w_z, w_xbc, w_dt, dt_bias)
    _, xbc_c, dt_c = _inproj(ctx, mods[1], ctx_row, g_mix1, w_z, w_xbc, w_dt, dt_bias)
    ssd = functools.partial(_ssd, a_log=a_log, e64=e64)
    conv = (conv_w, conv_b)
    h_f, xact_c = ssd(xbc_c, dt_c, conv, h0=None, direction=0, want_y=False)
    (h_b,) = ssd(xact_c, dt_c, None, h0=None, direction=1, want_y=False)
    y_f, _, xact = ssd(xbc, dt, conv, h0=h_f, direction=0, want_y=True)
    y_b, _ = ssd(xact, dt, None, h0=h_b, direction=1, want_y=True)
    return _mix_ffn(x, mods[1], x_row, (y_f, y_b, z, xact), bf(ssm_w_out[0]), row2(norm_ffn_g[1]),
                    bf(ffn_w_gate[1]), bf(ffn_w_up[1]), bf(ffn_w_down[1]),
                    ssm_norm_g=row2(ssm_norm_g[0]), d_skip=d_skip, final_g=row2(final_norm_g))
```

```python
import functools
import math

import numpy as np

import jax
import jax.numpy as jnp
from jax import lax
from jax.experimental import pallas as pl
from jax.experimental.pallas import tpu as pltpu

F32 = jnp.float32
BF16 = jnp.bfloat16

D_MODEL = 1024
DEPTH = 2
GRID_W = 64
EPS = 1e-6

HEAD_DIM = 64
N_HEADS = D_MODEL // HEAD_DIM
N_KV_HEADS = N_HEADS // 4
Q_PER_KV = N_HEADS // N_KV_HEADS
Q_DIM = N_HEADS * HEAD_DIM
KV_DIM = N_KV_HEADS * HEAD_DIM
QKV_DIM = Q_DIM + 2 * KV_DIM
WINDOW = 128
BLOCK = 128
ROPE_FREQS = HEAD_DIM // 4
ROPE_BASE = 10000.0

D_INNER = 2 * D_MODEL
SSM_HEAD_DIM = 64
SSM_HEADS = D_INNER // SSM_HEAD_DIM
SSM_GROUPS = 8
HEADS_PER_GROUP = SSM_HEADS // SSM_GROUPS
GROUP_DIM = D_INNER // SSM_GROUPS
D_STATE = 128
CHUNK = 128
BC_DIM = SSM_GROUPS * D_STATE
CONV_DIM = D_INNER + 2 * BC_DIM
D_FF = ((8 * D_MODEL // 3 + 255) // 256) * 256

LANES = 128
SUBLANES = 8
MOD_ROWS = 8
CTX_ROW = 4
VMEM_LIMIT = 48 * 1024 * 1024
NEG_INF = float("-inf")
LOG2E = math.log2(math.e)
FF_CHUNK = D_FF
HEADS_PER_DOT = N_HEADS


def _const_spec(shape):
    nd = len(shape)
    return pl.BlockSpec(shape, lambda *_: (0,) * nd, pipeline_mode=pl.Buffered(1))


def _layer_spec(stacked, layer):
    return pl.BlockSpec((None,) + stacked.shape[1:], lambda *_: (layer, 0, 0),
                        pipeline_mode=pl.Buffered(1))


def _silu(v):
    return v * jax.nn.sigmoid(v)


def _rms_mod(x, g, scale, shift):
    ms = jnp.mean(x * x, axis=-1, keepdims=True)
    return (x * lax.rsqrt(ms + EPS) * g) * (1.0 + scale) + shift


def _mod_chunk(m, i):
    return m[:, i * D_MODEL:(i + 1) * D_MODEL]


def _adaln_kernel(c_ref, w_ref, b_ref, o_ref):
    s = _silu(c_ref[...])
    o_ref[0] = jnp.dot(s, w_ref[0], preferred_element_type=F32) + b_ref[0]


def _adaln(cc, ada_w, ada_b):
    tn = 1536
    n = 6 * D_MODEL
    return pl.pallas_call(
        _adaln_kernel,
        grid=(DEPTH, n // tn),
        in_specs=[
            pl.BlockSpec((MOD_ROWS, D_MODEL), lambda i, j: (0, 0)),
            pl.BlockSpec((1, D_MODEL, tn), lambda i, j: (i, 0, j)),
            pl.BlockSpec((1, 1, tn), lambda i, j: (i, 0, j)),
        ],
        out_specs=pl.BlockSpec((1, MOD_ROWS, tn), lambda i, j: (i, 0, j)),
        out_shape=jax.ShapeDtypeStruct((DEPTH, MOD_ROWS, n), F32),
        compiler_params=pltpu.CompilerParams(
            dimension_semantics=("arbitrary", "arbitrary"), vmem_limit_bytes=VMEM_LIMIT),
        name="adaln",
    )(cc, ada_w, ada_b.reshape(DEPTH, 1, n))


_NT = (((1,), (1,)), ((), ()))
_TN = (((0,), (0,)), ((), ()))


def _qkv_kernel(*refs, rope):
    if rope:
        (x_ref, mod_ref, g_ref, wq_ref, wk_ref, wv_ref, cos_t_ref, sin_t_ref,
         cos_ref, sa_ref, sb_ref, qt_ref, k_ref, vt_ref) = refs
    else:
        x_ref, mod_ref, g_ref, wq_ref, wk_ref, wv_ref, qt_ref, k_ref, vt_ref = refs
    m = mod_ref[0]
    h = _rms_mod(x_ref[0], g_ref[...], _mod_chunk(m, 1), _mod_chunk(m, 0)).astype(BF16)
    q_t = lax.dot_general(wq_ref[...], h, _NT, preferred_element_type=F32)
    k = jnp.dot(h, wk_ref[...], preferred_element_type=F32)
    vt_ref[0] = lax.dot_general(wv_ref[...], h, _NT, preferred_element_type=F32).astype(BF16)
    scale = HEAD_DIM ** -0.5 * LOG2E
    if rope:
        f = ROPE_FREQS
        for head in range(N_HEADS):
            for axis in range(2):
                r0 = head * HEAD_DIM + axis * 2 * f
                cs = cos_t_ref[axis * f:(axis + 1) * f, :]
                sn = sin_t_ref[axis * f:(axis + 1) * f, :]
                x1 = q_t[r0:r0 + f, :]
                x2 = q_t[r0 + f:r0 + 2 * f, :]
                qt_ref[0, r0:r0 + f, :] = ((x1 * cs - x2 * sn) * scale).astype(BF16)
                qt_ref[0, r0 + f:r0 + 2 * f, :] = ((x2 * cs + x1 * sn) * scale).astype(BF16)
        for blk in range(KV_DIM // LANES):
            t = k[:, blk * LANES:(blk + 1) * LANES]
            t = (t * cos_ref[...] + pltpu.roll(t, LANES - f, 1) * sa_ref[...]
                 + pltpu.roll(t, f, 1) * sb_ref[...])
            k_ref[0, :, blk * LANES:(blk + 1) * LANES] = t.astype(BF16)
    else:
        qt_ref[0] = (q_t * scale).astype(BF16)
        k_ref[0] = k.astype(BF16)


def _rope_tables(L):
    f32 = np.float32
    rows = L // GRID_W
    row = np.repeat(np.arange(rows, dtype=f32), GRID_W)
    col = np.tile(np.arange(GRID_W, dtype=f32), rows)
    inv = (f32(ROPE_BASE) ** (-np.arange(ROPE_FREQS, dtype=f32) / f32(ROPE_FREQS))).astype(f32)
    ang_r = (row[:, None] * inv).astype(f32)
    ang_c = (col[:, None] * inv).astype(f32)
    zero = np.zeros_like(ang_r)

    def lanes(r_first, r_second, c_first, c_second):
        head = np.concatenate([r_first, r_second, c_first, c_second], axis=-1)
        return np.tile(head, (1, LANES // HEAD_DIM)).astype(f32)

    cr, sr, cc, sc = np.cos(ang_r), np.sin(ang_r), np.cos(ang_c), np.sin(ang_c)
    cos_t = np.ascontiguousarray(np.concatenate([cr, cc], axis=1).T).astype(f32)
    sin_t = np.ascontiguousarray(np.concatenate([sr, sc], axis=1).T).astype(f32)
    return (cos_t, sin_t, lanes(cr, cr, cc, cc), lanes(-sr, zero, -sc, zero),
            lanes(zero, sr, zero, sc))


def _qkv(x, mod, g, w_q_t, w_k, w_v_t, tables, mod_row):
    B, L, _ = x.shape
    tm = min(L, 512)
    rope = tables is not None
    in_specs = [
        pl.BlockSpec((1, tm, D_MODEL), lambda b, i: (b, i, 0)),
        pl.BlockSpec((1, 1, 6 * D_MODEL), lambda b, i: (mod_row(b), 0, 0)),
        _const_spec((1, D_MODEL)),
        _const_spec(w_q_t.shape), _const_spec(w_k.shape), _const_spec(w_v_t.shape),
    ]
    args = [x, mod, g, w_q_t, w_k, w_v_t]
    if rope:
        in_specs += [pl.BlockSpec((2 * ROPE_FREQS, tm), lambda b, i: (0, i))] * 2
        in_specs += [pl.BlockSpec((tm, LANES), lambda b, i: (i, 0))] * 3
        args += list(tables)
    return pl.pallas_call(
        functools.partial(_qkv_kernel, rope=rope),
        grid=(B, L // tm),
        in_specs=in_specs,
        out_specs=[
            pl.BlockSpec((1, Q_DIM, tm), lambda b, i: (b, 0, i)),
            pl.BlockSpec((1, tm, KV_DIM), lambda b, i: (b, i, 0)),
            pl.BlockSpec((1, KV_DIM, tm), lambda b, i: (b, 0, i)),
        ],
        out_shape=[
            jax.ShapeDtypeStruct((B, Q_DIM, L), BF16),
            jax.ShapeDtypeStruct((B, L, KV_DIM), BF16),
            jax.ShapeDtypeStruct((B, KV_DIM, L), BF16),
        ],
        compiler_params=pltpu.CompilerParams(
            dimension_semantics=("arbitrary", "arbitrary"), vmem_limit_bytes=VMEM_LIMIT),
        name="qkv_rope" if rope else "qkv_ctx",
    )(*args)


def _attn_kernel(*refs, window, nq, n_ctx):
    if window:
        (sink_ref, qt_ref, kp_ref, kc_ref, kn_ref, vp_ref, vc_ref, vn_ref,
         kx_ref, vx_ref, ot_ref, kbuf, vtbuf) = refs
    else:
        sink_ref, qt_ref, kx_ref, vx_ref, ot_ref, kbuf, vtbuf = refs
    width = HEADS_PER_DOT * BLOCK
    kbuf[0:n_ctx, :] = kx_ref[0]
    vtbuf[:, 0:n_ctx] = vx_ref[0]
    if window:
        for j, (kr, vr) in enumerate(((kp_ref, vp_ref), (kc_ref, vc_ref), (kn_ref, vn_ref))):
            kbuf[n_ctx + j * BLOCK:n_ctx + (j + 1) * BLOCK, :] = kr[0]
            vtbuf[:, n_ctx + j * BLOCK:n_ctx + (j + 1) * BLOCK] = vr[0]
        n = pl.program_id(1)
        kj = lax.broadcasted_iota(jnp.int32, (BLOCK, width), 0)
        qi = lax.broadcasted_iota(jnp.int32, (BLOCK, width), 1) % BLOCK
        bias_prev = jnp.where((kj >= qi) & (n > 0), 0.0, NEG_INF).astype(F32)
        bias_next = jnp.where((kj <= qi) & (n < nq - 1), 0.0, NEG_INF).astype(F32)
    lane = lax.broadcasted_iota(jnp.int32, (1, width), 1)
    ones_rows = jnp.ones((16, kbuf.shape[0]), BF16)
    for h0 in range(0, N_HEADS, HEADS_PER_DOT):
        cols = []
        for h in range(h0, h0 + HEADS_PER_DOT):
            g = h // Q_PER_KV
            parts = [qt_ref[0, h * HEAD_DIM:(h + 1) * HEAD_DIM, :]]
            if g > 0:
                parts.insert(0, jnp.zeros((g * HEAD_DIM, BLOCK), BF16))
            if g < N_KV_HEADS - 1:
                parts.append(jnp.zeros(((N_KV_HEADS - 1 - g) * HEAD_DIM, BLOCK), BF16))
            cols.append(jnp.concatenate(parts, axis=0) if len(parts) > 1 else parts[0])
        q_exp = jnp.concatenate(cols, axis=1) if len(cols) > 1 else cols[0]
        sink = jnp.full((1, width), sink_ref[h0], F32)
        for r in range(1, HEADS_PER_DOT):
            sink = jnp.where(lane >= r * BLOCK, sink_ref[h0 + r], sink)
        sink = sink * LOG2E
        s = jnp.dot(kbuf[...], q_exp, preferred_element_type=F32)
        if window:
            s = jnp.concatenate([
                s[:n_ctx], s[n_ctx:n_ctx + BLOCK] + bias_prev,
                s[n_ctx + BLOCK:n_ctx + 2 * BLOCK], s[n_ctx + 2 * BLOCK:] + bias_next], axis=0)
        mx = jnp.maximum(jnp.max(s, axis=0, keepdims=True), sink)
        p = jnp.exp2(s - mx).astype(BF16)
        sink_p = jnp.exp2(sink - mx)
        gw = Q_PER_KV * BLOCK
        for g0 in range(0, HEADS_PER_DOT // Q_PER_KV):
            g = h0 // Q_PER_KV + g0
            gl = slice(g0 * gw, (g0 + 1) * gw)
            v_ext = jnp.concatenate([vtbuf[g * HEAD_DIM:(g + 1) * HEAD_DIM, :], ones_rows], axis=0)
            pv = jnp.dot(v_ext, p[:, gl], preferred_element_type=F32)
            o_t = pv[:HEAD_DIM, :] / (pv[HEAD_DIM:HEAD_DIM + 1, :] + sink_p[:, gl])
            for r in range(Q_PER_KV):
                h = g * Q_PER_KV + r
                ot_ref[0, h * HEAD_DIM:(h + 1) * HEAD_DIM, :] = (
                    o_t[:, r * BLOCK:(r + 1) * BLOCK].astype(BF16))


def _attn(q_t, k, v_t, kx, vx_t, sinks):
    B, _, L = q_t.shape
    C = kx.shape[1]
    nq = L // BLOCK
    window = k is not None
    smem = pl.BlockSpec(memory_space=pltpu.SMEM)
    q_spec = pl.BlockSpec((1, Q_DIM, BLOCK), lambda b, n: (b, 0, n))
    kx_spec = pl.BlockSpec((1, C, KV_DIM), lambda b, n: (b, 0, 0))
    vx_spec = pl.BlockSpec((1, KV_DIM, C), lambda b, n: (b, 0, 0))
    if window:
        prev = lambda n: jnp.maximum(n - 1, 0)
        nxt = lambda n: jnp.minimum(n + 1, nq - 1)
        k_spec = lambda f: pl.BlockSpec((1, BLOCK, KV_DIM), lambda b, n: (b, f(n), 0))
        v_spec = lambda f: pl.BlockSpec((1, KV_DIM, BLOCK), lambda b, n: (b, 0, f(n)))
        same = lambda n: n
        in_specs = [smem, q_spec, k_spec(prev), k_spec(same), k_spec(nxt),
                    v_spec(prev), v_spec(same), v_spec(nxt), kx_spec, vx_spec]
        args = (sinks, q_t, k, k, k, v_t, v_t, v_t, kx, vx_t)
        n_keys = C + 3 * BLOCK
    else:
        in_specs = [smem, q_spec, kx_spec, vx_spec]
        args = (sinks, q_t, kx, vx_t)
        n_keys = C
    return pl.pallas_call(
        functools.partial(_attn_kernel, window=window, nq=nq, n_ctx=C),
        grid=(B, nq),
        in_specs=in_specs,
        out_specs=pl.BlockSpec((1, Q_DIM, BLOCK), lambda b, n: (b, 0, n)),
        out_shape=jax.ShapeDtypeStruct((B, Q_DIM, L), BF16),
        scratch_shapes=[pltpu.VMEM((n_keys, KV_DIM), BF16), pltpu.VMEM((KV_DIM, n_keys), BF16)],
        compiler_params=pltpu.CompilerParams(
            dimension_semantics=("arbitrary", "arbitrary"), vmem_limit_bytes=VMEM_LIMIT),
        name="attn_window" if window else "attn_ctx",
    )(*args)


def _mix_ffn_kernel(*refs, ssm, final, n_sub):
    if ssm:
        (x_ref, mod_ref, yf_ref, yb_ref, z_ref, xs_ref, dsk_ref, ng_ref,
         wo_ref, gf_ref, wg_ref, wu_ref, wd_ref) = refs[:13]
        rest = refs[13:]
    else:
        (x_ref, mod_ref, o_ref_in, wo_ref, gf_ref, wg_ref, wu_ref, wd_ref) = refs[:8]
        rest = refs[8:]
    if final:
        fg_ref, out_ref = rest
    else:
        (out_ref,) = rest
    m = mod_ref[0]
    ts = x_ref.shape[1] // n_sub
    for sub in range(n_sub):
        rs = slice(sub * ts, (sub + 1) * ts)
        if ssm:
            y = (yf_ref[0, rs, :].astype(F32) + yb_ref[0, rs, :].astype(F32)
                 + dsk_ref[...] * xs_ref[0, rs, :].astype(F32)) * _silu(z_ref[0, rs, :].astype(F32))
            parts = []
            for g in range(SSM_GROUPS):
                yg = y[:, g * GROUP_DIM:(g + 1) * GROUP_DIM]
                ms = jnp.mean(yg * yg, axis=-1, keepdims=True)
                parts.append((yg * lax.rsqrt(ms + EPS)
                              * ng_ref[:, g * GROUP_DIM:(g + 1) * GROUP_DIM]).astype(BF16))
            mix = jnp.dot(jnp.concatenate(parts, axis=1), wo_ref[...], preferred_element_type=F32)
        else:
            mix = lax.dot_general(o_ref_in[0, :, rs], wo_ref[...], _TN, preferred_element_type=F32)
        x1 = x_ref[0, rs, :] + _mod_chunk(m, 2) * mix
        h2 = _rms_mod(x1, gf_ref[...], _mod_chunk(m, 4), _mod_chunk(m, 3)).astype(BF16)
        ffn = None
        for f0 in range(0, D_FF, FF_CHUNK):
            fs = slice(f0, min(f0 + FF_CHUNK, D_FF))
            gate = jnp.dot(h2, wg_ref[:, fs], preferred_element_type=F32)
            up = jnp.dot(h2, wu_ref[:, fs], preferred_element_type=F32)
            act = (_silu(gate) * up).astype(BF16)
            part = jnp.dot(act, wd_ref[fs, :], preferred_element_type=F32)
            ffn = part if ffn is None else ffn + part
        x2 = x1 + _mod_chunk(m, 5) * ffn
        if final:
            ms = jnp.mean(x2 * x2, axis=-1, keepdims=True)
            x2 = x2 * lax.rsqrt(ms + EPS) * fg_ref[...]
        out_ref[0, rs, :] = x2


def _mix_ffn(x, mod, mod_row, mixer_in, w_mix, g_ffn, w_gate, w_up, w_down, layer,
             ssm_norm_g=None, d_skip=None, final_g=None, tm=256, n_sub=1):
    B, L, _ = x.shape
    tm = min(L, tm)
    ssm = ssm_norm_g is not None
    final = final_g is not None
    row = lambda width: pl.BlockSpec((1, tm, width), lambda b, i: (b, i, 0))
    in_specs = [row(D_MODEL), pl.BlockSpec((1, 1, 6 * D_MODEL), lambda b, i: (mod_row(b), 0, 0))]
    args = [x, mod]
    if ssm:
        in_specs += [row(D_INNER)] * 4 + [_const_spec((1, D_INNER))] * 2
        args += list(mixer_in) + [d_skip, ssm_norm_g]
    else:
        in_specs += [pl.BlockSpec((1, Q_DIM, tm), lambda b, i: (b, 0, i))]
        args += [mixer_in]
    in_specs += [_layer_spec(w_mix, 0), _const_spec((1, D_MODEL)), _layer_spec(w_gate, layer),
                 _layer_spec(w_up, layer), _layer_spec(w_down, layer)]
    args += [w_mix, g_ffn, w_gate, w_up, w_down]
    if final:
        in_specs += [_const_spec((1, D_MODEL))]
        args += [final_g]
    return pl.pallas_call(
        functools.partial(_mix_ffn_kernel, ssm=ssm, final=final, n_sub=n_sub),
        grid=(B, L // tm),
        in_specs=in_specs,
        out_specs=row(D_MODEL),
        out_shape=jax.ShapeDtypeStruct((B, L, D_MODEL), F32),
        compiler_params=pltpu.CompilerParams(
            dimension_semantics=("arbitrary", "arbitrary"), vmem_limit_bytes=VMEM_LIMIT),
        name="ssm_out_ffn" if ssm else "attn_out_ffn",
    )(*args)


def _inproj_kernel(x_ref, xp_ref, xn_ref, mod_ref, g_ref, w_ref, wdt_ref, dtb_ref, cw_ref, cb_ref,
                   z_ref, xa_ref, dt_ref, cbuf, *, n_tiles):
    i = pl.program_id(1)
    m = mod_ref[0]
    g, scale, shift = g_ref[...], _mod_chunk(m, 1), _mod_chunk(m, 0)
    tm = x_ref.shape[1]
    h = _rms_mod(x_ref[0], g, scale, shift).astype(BF16)
    x_ext = jnp.concatenate([xp_ref[0], x_ref[0], xn_ref[0]], axis=0)
    h_ext = _rms_mod(x_ext, g, scale, shift).astype(BF16)
    z_ref[0] = jnp.dot(h, w_ref[:, :D_INNER], preferred_element_type=F32).astype(BF16)
    raw = jnp.dot(h, wdt_ref[...], preferred_element_type=F32) + dtb_ref[...]
    dt_ref[0] = jnp.maximum(raw, 0.0) + jnp.log1p(jnp.exp(-jnp.abs(raw)))
    keep_prev = jnp.where(i > 0, 1.0, 0.0)
    keep_next = jnp.where(i < n_tiles - 1, 1.0, 0.0)
    cstep = 512
    for j in range(CONV_DIM // cstep):
        sl = slice(j * cstep, (j + 1) * cstep)
        u = jnp.dot(h_ext, w_ref[:, D_INNER + j * cstep:D_INNER + (j + 1) * cstep],
                    preferred_element_type=F32)
        cbuf[0:SUBLANES, sl] = u[0:SUBLANES] * keep_prev
        cbuf[SUBLANES:SUBLANES + tm, sl] = u[SUBLANES:SUBLANES + tm]
        cbuf[SUBLANES + tm:, sl] = u[SUBLANES + tm:] * keep_next
        conv = (cbuf[SUBLANES - 1:SUBLANES - 1 + tm, sl] * cw_ref[0:1, sl]
                + cbuf[SUBLANES:SUBLANES + tm, sl] * cw_ref[1:2, sl]
                + cbuf[SUBLANES + 1:SUBLANES + 1 + tm, sl] * cw_ref[2:3, sl] + cb_ref[:, sl])
        xa_ref[0, :, sl] = _silu(conv).astype(BF16)


def _inproj(x, mod, mod_row, *, g, w_in, w_dt, dt_bias, conv_w, conv_b):
    B, L, _ = x.shape
    tm = min(L, 256)
    n_tiles = L // tm
    per = tm // SUBLANES
    row = lambda width: pl.BlockSpec((1, tm, width), lambda b, i: (b, i, 0))
    return pl.pallas_call(
        functools.partial(_inproj_kernel, n_tiles=n_tiles),
        grid=(B, n_tiles),
        in_specs=[row(D_MODEL),
                  pl.BlockSpec((1, SUBLANES, D_MODEL), lambda b, i: (b, jnp.maximum(i * per - 1, 0), 0)),
                  pl.BlockSpec((1, SUBLANES, D_MODEL),
                               lambda b, i: (b, jnp.minimum((i + 1) * per, L // SUBLANES - 1), 0)),
                  pl.BlockSpec((1, 1, 6 * D_MODEL), lambda b, i: (mod_row(b), 0, 0)),
                  _const_spec((1, D_MODEL)), _const_spec(w_in.shape), _const_spec(w_dt.shape),
                  _const_spec((1, 2 * LANES)), _const_spec((8, CONV_DIM)), _const_spec((1, CONV_DIM))],
        out_specs=[row(D_INNER), row(CONV_DIM), row(2 * LANES)],
        out_shape=[jax.ShapeDtypeStruct((B, L, D_INNER), BF16),
                   jax.ShapeDtypeStruct((B, L, CONV_DIM), BF16),
                   jax.ShapeDtypeStruct((B, L, 2 * LANES), F32)],
        scratch_shapes=[pltpu.VMEM((tm + 2 * SUBLANES, CONV_DIM), F32)],
        compiler_params=pltpu.CompilerParams(
            dimension_semantics=("arbitrary", "arbitrary"), vmem_limit_bytes=VMEM_LIMIT),
        name="ssm_inproj",
    )(x, x, x, mod, g, w_in, w_dt, dt_bias, conv_w, conv_b)


def _split3(v):
    hi = v.astype(BF16)
    r1 = v - hi.astype(F32)
    mid = r1.astype(BF16)
    lo = (r1 - mid.astype(F32)).astype(BF16)
    return hi, mid, lo


def _ssd_kernel(*refs, reverse, nc, has_init, want_y):
    (xa_ref, dt_ref, alog_ref, e64_ref), rest = refs[:4], refs[4:]
    if has_init:
        h0_ref, rest = rest[0], rest[1:]
    if want_y:
        y_ref, rest = rest[0], rest[1:]
    hfin_ref, state, ce, tbuf = rest
    c = pl.program_id(1)

    @pl.when(c == 0)
    def _():
        state[...] = h0_ref[0] if has_init else jnp.zeros(state.shape, F32)

    lane = lax.broadcasted_iota(jnp.int32, (CHUNK, LANES), 1)
    row = lax.broadcasted_iota(jnp.int32, (CHUNK, LANES), 0)
    dt = jnp.where(lane < SSM_HEADS, dt_ref[0], 0.0)
    a = dt * (-LOG2E * jnp.exp(alog_ref[0]))
    valid = (lane >= row) if reverse else (lane <= row)
    tri = jnp.where(valid, 1.0, 0.0).astype(BF16)
    a_hi, a_mid, a_lo = _split3(a)
    cum = (jnp.dot(tri, a_hi, preferred_element_type=F32)
           + jnp.dot(tri, a_mid, preferred_element_type=F32)
           + jnp.dot(tri, a_lo, preferred_element_type=F32))
    c_hi, c_mid, c_lo = _split3(cum)
    pieces = (c_hi.astype(F32) + pltpu.roll(c_mid.astype(F32), SSM_HEADS, 1)
              + pltpu.roll(c_lo.astype(F32), 2 * SSM_HEADS, 1)).astype(BF16)
    ce[...] = jnp.dot(pieces, e64_ref[...], preferred_element_type=F32)
    tb = (cum + pltpu.roll(dt, SSM_HEADS, 1)).T
    last = 0 if reverse else CHUNK - 1
    cum_t = tb[0:SSM_HEADS, :]
    dt_t = tb[SSM_HEADS:2 * SSM_HEADS, :]
    tbuf[0:SSM_HEADS, :] = cum_t - jnp.log2(dt_t)
    tbuf[SSM_HEADS:2 * SSM_HEADS, :] = jnp.exp2(cum_t[:, last:last + 1] - cum_t) * dt_t

    lane_g = lax.broadcasted_iota(jnp.int32, (CHUNK, GROUP_DIM), 1)
    head_mask = [jnp.where((lane_g >= hl * SSM_HEAD_DIM) & (lane_g < (hl + 1) * SSM_HEAD_DIM),
                           1.0, 0.0).astype(BF16) for hl in range(HEADS_PER_GROUP)]
    low_half = lane < SSM_HEAD_DIM
    for g in range(SSM_GROUPS):
        gs = slice(g * GROUP_DIM, (g + 1) * GROUP_DIM)
        b_g = xa_ref[0, :, D_INNER + g * D_STATE:D_INNER + (g + 1) * D_STATE]
        c_g = xa_ref[0, :, D_INNER + BC_DIM + g * D_STATE:D_INNER + BC_DIM + (g + 1) * D_STATE]
        b_t = b_g.astype(F32).T
        x_g = xa_ref[0, :, gs]
        if want_y:
            cb_mat = lax.dot_general(c_g, b_g, _NT, preferred_element_type=F32)
            y_g = (jnp.dot(c_g, state[g].astype(BF16), preferred_element_type=F32)
                   * jnp.exp2(ce[:, gs]))
        st = jnp.zeros((D_STATE, GROUP_DIM), F32)
        for hp in range(HEADS_PER_GROUP // 2):
            pair = g * (HEADS_PER_GROUP // 2) + hp
            lhs_y, lhs_s, rhs = [], [], []
            if want_y:
                ce_pair = ce[:, pair * LANES:(pair + 1) * LANES]
                swapped = pltpu.roll(ce_pair, SSM_HEAD_DIM, 1)
            for hh in range(2):
                hl = 2 * hp + hh
                h = g * HEADS_PER_GROUP + hl
                if want_y:
                    own = low_half if hh == 0 else jnp.logical_not(low_half)
                    cum_i = jnp.where(own, ce_pair, swapped)
                    dec = jnp.exp2(jnp.where(valid, cum_i - tbuf[h:h + 1, :], NEG_INF))
                    lhs_y.append((dec * cb_mat).astype(BF16))
                lhs_s.append((b_t * tbuf[SSM_HEADS + h:SSM_HEADS + h + 1, :]).astype(BF16))
                rhs.append(x_g * head_mask[hl])
            rhs = jnp.concatenate(rhs, axis=0)
            if want_y:
                y_g = y_g + jnp.dot(jnp.concatenate(lhs_y, axis=1), rhs, preferred_element_type=F32)
            st = st + jnp.dot(jnp.concatenate(lhs_s, axis=1), rhs, preferred_element_type=F32)
        if want_y:
            y_ref[0, :, gs] = y_g.astype(BF16)
        state[g] = state[g] * jnp.exp2(ce[last:last + 1, gs]) + st

    @pl.when(c == nc - 1)
    def _():
        hfin_ref[0] = state[...]


def _ssd(xact, dt, a_log, e64, h0, direction, want_y):
    B, L, _ = xact.shape
    nc = L // CHUNK
    reverse = direction == 1
    has_init = h0 is not None
    chunk = (lambda c: nc - 1 - c) if reverse else (lambda c: c)
    st_shape = (SSM_GROUPS, D_STATE, GROUP_DIM)
    st_spec = pl.BlockSpec((1,) + st_shape, lambda b, c: (b, 0, 0, 0))
    in_specs = [
        pl.BlockSpec((1, CHUNK, CONV_DIM), lambda b, c: (b, chunk(c), 0)),
        pl.BlockSpec((1, CHUNK, LANES), lambda b, c: (b, chunk(c), direction)),
        pl.BlockSpec((1, 1, LANES), lambda b, c: (direction, 0, 0)),
        _const_spec(e64.shape),
    ]
    args = [xact, dt, a_log, e64]
    if has_init:
        in_specs.append(st_spec)
        args.append(h0)
    out_specs, out_shape = [], []
    if want_y:
        out_specs.append(pl.BlockSpec((1, CHUNK, D_INNER), lambda b, c: (b, chunk(c), 0)))
        out_shape.append(jax.ShapeDtypeStruct((B, L, D_INNER), BF16))
    out_specs.append(st_spec)
    out_shape.append(jax.ShapeDtypeStruct((B,) + st_shape, F32))
    scratch = [pltpu.VMEM(st_shape, F32), pltpu.VMEM((CHUNK, D_INNER), F32),
               pltpu.VMEM((2 * SSM_HEADS, CHUNK), F32)]
    return pl.pallas_call(
        functools.partial(_ssd_kernel, reverse=reverse, nc=nc, has_init=has_init, want_y=want_y),
        grid=(B, nc),
        in_specs=in_specs,
        out_specs=out_specs,
        out_shape=out_shape,
        scratch_shapes=scratch,
        compiler_params=pltpu.CompilerParams(
            dimension_semantics=("arbitrary", "arbitrary"), vmem_limit_bytes=VMEM_LIMIT),
        name="ssd_bwd" if reverse else "ssd_fwd",
    )(*args)


def _expand_matrix(width):
    k = np.arange(LANES)[:, None]
    col_head = (np.arange(SSM_HEADS * width) // width)[None, :]
    return jnp.asarray((k % SSM_HEADS == col_head) & (k < 3 * SSM_HEADS), dtype=BF16)


def kernel(x, c, ctx, c_ctx, ada_w, ada_b, norm_mix_g, norm_ffn_g, attn_w_qkv, attn_w_o, attn_sinks,
           ssm_w_in, ssm_conv_w, ssm_conv_b, ssm_dt_bias, ssm_A_log, ssm_D, ssm_norm_g, ssm_w_out,
           ffn_w_gate, ffn_w_up, ffn_w_down, final_norm_g):
    B, L, _ = x.shape
    assert x.shape == (B, L, D_MODEL) and B <= CTX_ROW and L % 512 == 0
    bf = lambda w: w.astype(BF16)
    row2 = lambda v: v.reshape(1, -1)

    cc = jnp.zeros((MOD_ROWS, D_MODEL), F32).at[:B].set(c).at[CTX_ROW].set(c_ctx)
    mod = _adaln(cc, ada_w, ada_b)
    mods = [mod[i].reshape(MOD_ROWS, 1, 6 * D_MODEL) for i in range(DEPTH)]
    x_row = lambda b: b
    ctx_row = lambda b: CTX_ROW

    w_qkv = attn_w_qkv[0]
    w_q_t = bf(w_qkv[:, :Q_DIM].T)
    w_k = bf(w_qkv[:, Q_DIM:Q_DIM + KV_DIM])
    w_v_t = bf(w_qkv[:, Q_DIM + KV_DIM:].T)
    g_mix = row2(norm_mix_g[0])
    q, k, v = _qkv(x, mods[0], g_mix, w_q_t, w_k, w_v_t, _rope_tables(L), x_row)
    qc, kc, vc = _qkv(ctx, mods[0], g_mix, w_q_t, w_k, w_v_t, None, ctx_row)
    sinks = attn_sinks[0].astype(F32)
    o = _attn(q, k, v, kc, vc, sinks)
    oc = _attn(qc, None, None, kc, vc, sinks)
    w_gate, w_up, w_down = bf(ffn_w_gate), bf(ffn_w_up), bf(ffn_w_down)
    ffn0 = (bf(attn_w_o), row2(norm_ffn_g[0]), w_gate, w_up, w_down, 0)
    x = _mix_ffn(x, mods[0], x_row, o, *ffn0)
    ctx = _mix_ffn(ctx, mods[0], ctx_row, oc, *ffn0)

    w_in = ssm_w_in[0]
    w_dt_raw = w_in[:, D_INNER + CONV_DIM:]
    w_dt = jnp.zeros((D_MODEL, 2 * LANES), F32)
    w_dt = w_dt.at[:, :SSM_HEADS].set(w_dt_raw[:, :SSM_HEADS])
    w_dt = bf(w_dt.at[:, LANES:LANES + SSM_HEADS].set(w_dt_raw[:, SSM_HEADS:]))
    dt_bias = jnp.zeros((2, LANES), F32).at[:, :SSM_HEADS].set(ssm_dt_bias[0]).reshape(1, 2 * LANES)
    a_log = jnp.zeros((2, 1, LANES), F32).at[:, 0, :SSM_HEADS].set(ssm_A_log[0])
    conv_w = jnp.zeros((8, CONV_DIM), F32).at[:3].set(ssm_conv_w[0])
    conv_b = row2(ssm_conv_b[0])
    d_skip = row2(jnp.repeat(ssm_D[0], SSM_HEAD_DIM))
    e64 = _expand_matrix(SSM_HEAD_DIM)
    g_mix1 = row2(norm_mix_g[1])

    inproj = functools.partial(_inproj, g=g_mix1, w_in=bf(w_in), w_dt=w_dt, dt_bias=dt_bias,
                               conv_w=conv_w, conv_b=conv_b)
    z, xact, dt = inproj(x, mods[1], x_row)
    _, xact_c, dt_c = inproj(ctx, mods[1], ctx_row)
    ssd = functools.partial(_ssd, a_log=a_log, e64=e64)
    (h_f,) = ssd(xact_c, dt_c, h0=None, direction=0, want_y=False)
    (h_b,) = ssd(xact_c, dt_c, h0=None, direction=1, want_y=False)
    y_f, _ = ssd(xact, dt, h0=h_f, direction=0, want_y=True)
    y_b, _ = ssd(xact, dt, h0=h_b, direction=1, want_y=True)
    return _mix_ffn(x, mods[1], x_row, (y_f, y_b, z, xact), bf(ssm_w_out), row2(norm_ffn_g[1]),
                    w_gate, w_up, w_down, 1,
                    ssm_norm_g=row2(ssm_norm_g[0]), d_skip=d_skip, final_g=row2(final_norm_g))
```

```python
import functools
import math

import numpy as np

import jax
import jax.numpy as jnp
from jax import lax
from jax.experimental import pallas as pl
from jax.experimental.pallas import tpu as pltpu

F32 = jnp.float32
BF16 = jnp.bfloat16

D_MODEL = 1024
DEPTH = 2
GRID_W = 64
EPS = 1e-6

HEAD_DIM = 64
N_HEADS = D_MODEL // HEAD_DIM
N_KV_HEADS = N_HEADS // 4
Q_PER_KV = N_HEADS // N_KV_HEADS
Q_DIM = N_HEADS * HEAD_DIM
KV_DIM = N_KV_HEADS * HEAD_DIM
QKV_DIM = Q_DIM + 2 * KV_DIM
WINDOW = 128
BLOCK = 128
ROPE_FREQS = HEAD_DIM // 4
ROPE_BASE = 10000.0

D_INNER = 2 * D_MODEL
SSM_HEAD_DIM = 64
SSM_HEADS = D_INNER // SSM_HEAD_DIM
SSM_GROUPS = 8
HEADS_PER_GROUP = SSM_HEADS // SSM_GROUPS
GROUP_DIM = D_INNER // SSM_GROUPS
D_STATE = 128
CHUNK = 128
BC_DIM = SSM_GROUPS * D_STATE
CONV_DIM = D_INNER + 2 * BC_DIM
D_FF = ((8 * D_MODEL // 3 + 255) // 256) * 256

LANES = 128
SUBLANES = 8
MOD_ROWS = 8
CTX_ROW = 4
VMEM_LIMIT = 48 * 1024 * 1024
NEG_INF = float("-inf")
LOG2E = math.log2(math.e)
FF_CHUNK = D_FF
HEADS_PER_DOT = N_HEADS


def _const_spec(shape):
    nd = len(shape)
    return pl.BlockSpec(shape, lambda *_: (0,) * nd, pipeline_mode=pl.Buffered(1))


def _layer_spec(stacked, layer):
    return pl.BlockSpec((None,) + stacked.shape[1:], lambda *_: (layer, 0, 0),
                        pipeline_mode=pl.Buffered(1))


def _silu(v):
    return v * jax.nn.sigmoid(v)


def _rms_mod(x, g, scale, shift):
    ms = jnp.mean(x * x, axis=-1, keepdims=True)
    return (x * lax.rsqrt(ms + EPS) * g) * (1.0 + scale) + shift


def _mod_chunk(m, i):
    return m[:, i * D_MODEL:(i + 1) * D_MODEL]


def _adaln_kernel(c_ref, w_ref, b_ref, o_ref):
    s = _silu(c_ref[...])
    o_ref[0] = jnp.dot(s, w_ref[0], preferred_element_type=F32) + b_ref[0]


def _adaln(cc, ada_w, ada_b):
    tn = 1536
    n = 6 * D_MODEL
    return pl.pallas_call(
        _adaln_kernel,
        grid=(DEPTH, n // tn),
        in_specs=[
            pl.BlockSpec((MOD_ROWS, D_MODEL), lambda i, j: (0, 0)),
            pl.BlockSpec((1, D_MODEL, tn), lambda i, j: (i, 0, j)),
            pl.BlockSpec((1, 1, tn), lambda i, j: (i, 0, j)),
        ],
        out_specs=pl.BlockSpec((1, MOD_ROWS, tn), lambda i, j: (i, 0, j)),
        out_shape=jax.ShapeDtypeStruct((DEPTH, MOD_ROWS, n), F32),
        compiler_params=pltpu.CompilerParams(
            dimension_semantics=("arbitrary", "arbitrary"), vmem_limit_bytes=VMEM_LIMIT),
        name="adaln",
    )(cc, ada_w, ada_b.reshape(DEPTH, 1, n))


_NT = (((1,), (1,)), ((), ()))
_TN = (((0,), (0,)), ((), ()))


def _qkv_kernel(*refs, rope):
    if rope:
        (x_ref, mod_ref, g_ref, wq_ref, wk_ref, wv_ref, cos_t_ref, sin_t_ref,
         cos_ref, sa_ref, sb_ref, qt_ref, k_ref, vt_ref) = refs
    else:
        x_ref, mod_ref, g_ref, wq_ref, wk_ref, wv_ref, qt_ref, k_ref, vt_ref = refs
    m = mod_ref[0]
    h = _rms_mod(x_ref[0], g_ref[...], _mod_chunk(m, 1), _mod_chunk(m, 0)).astype(BF16)
    q_t = lax.dot_general(wq_ref[...], h, _NT, preferred_element_type=F32)
    k = jnp.dot(h, wk_ref[...], preferred_element_type=F32)
    v_t = lax.dot_general(wv_ref[...], h, _NT, preferred_element_type=F32).astype(BF16)
    n_blk = x_ref.shape[1] // BLOCK

    def put(ref, rows, val):
        for j in range(n_blk):
            ref[0, j, rows, :] = val[:, j * BLOCK:(j + 1) * BLOCK]

    put(vt_ref, slice(0, KV_DIM), v_t)
    scale = HEAD_DIM ** -0.5 * LOG2E
    if rope:
        f = ROPE_FREQS
        for head in range(N_HEADS):
            for axis in range(2):
                r0 = head * HEAD_DIM + axis * 2 * f
                cs = cos_t_ref[axis * f:(axis + 1) * f, :]
                sn = sin_t_ref[axis * f:(axis + 1) * f, :]
                x1 = q_t[r0:r0 + f, :]
                x2 = q_t[r0 + f:r0 + 2 * f, :]
                put(qt_ref, slice(r0, r0 + f), ((x1 * cs - x2 * sn) * scale).astype(BF16))
                put(qt_ref, slice(r0 + f, r0 + 2 * f), ((x2 * cs + x1 * sn) * scale).astype(BF16))
        for blk in range(KV_DIM // LANES):
            t = k[:, blk * LANES:(blk + 1) * LANES]
            t = (t * cos_ref[...] + pltpu.roll(t, LANES - f, 1) * sa_ref[...]
                 + pltpu.roll(t, f, 1) * sb_ref[...])
            k_ref[0, :, blk * LANES:(blk + 1) * LANES] = t.astype(BF16)
    else:
        put(qt_ref, slice(0, Q_DIM), (q_t * scale).astype(BF16))
        k_ref[0] = k.astype(BF16)


def _rope_tables(L):
    f32 = np.float32
    rows = L // GRID_W
    row = np.repeat(np.arange(rows, dtype=f32), GRID_W)
    col = np.tile(np.arange(GRID_W, dtype=f32), rows)
    inv = (f32(ROPE_BASE) ** (-np.arange(ROPE_FREQS, dtype=f32) / f32(ROPE_FREQS))).astype(f32)
    ang_r = (row[:, None] * inv).astype(f32)
    ang_c = (col[:, None] * inv).astype(f32)
    zero = np.zeros_like(ang_r)

    def lanes(r_first, r_second, c_first, c_second):
        head = np.concatenate([r_first, r_second, c_first, c_second], axis=-1)
        return np.tile(head, (1, LANES // HEAD_DIM)).astype(f32)

    cr, sr, cc, sc = np.cos(ang_r), np.sin(ang_r), np.cos(ang_c), np.sin(ang_c)
    cos_t = np.ascontiguousarray(np.concatenate([cr, cc], axis=1).T).astype(f32)
    sin_t = np.ascontiguousarray(np.concatenate([sr, sc], axis=1).T).astype(f32)
    return (cos_t, sin_t, lanes(cr, cr, cc, cc), lanes(-sr, zero, -sc, zero),
            lanes(zero, sr, zero, sc))


def _qkv(x, mod, g, w_q_t, w_k, w_v_t, tables, mod_row):
    B, L, _ = x.shape
    tm = min(L, 512)
    rope = tables is not None
    in_specs = [
        pl.BlockSpec((1, tm, D_MODEL), lambda b, i: (b, i, 0)),
        pl.BlockSpec((1, 1, 6 * D_MODEL), lambda b, i: (mod_row(b), 0, 0)),
        _const_spec((1, D_MODEL)),
        _const_spec(w_q_t.shape), _const_spec(w_k.shape), _const_spec(w_v_t.shape),
    ]
    args = [x, mod, g, w_q_t, w_k, w_v_t]
    if rope:
        in_specs += [pl.BlockSpec((2 * ROPE_FREQS, tm), lambda b, i: (0, i))] * 2
        in_specs += [pl.BlockSpec((tm, LANES), lambda b, i: (i, 0))] * 3
        args += list(tables)
    return pl.pallas_call(
        functools.partial(_qkv_kernel, rope=rope),
        grid=(B, L // tm),
        in_specs=in_specs,
        out_specs=[
            pl.BlockSpec((1, tm // BLOCK, Q_DIM, BLOCK), lambda b, i: (b, i, 0, 0)),
            pl.BlockSpec((1, tm, KV_DIM), lambda b, i: (b, i, 0)),
            pl.BlockSpec((1, tm // BLOCK, KV_DIM, BLOCK), lambda b, i: (b, i, 0, 0)),
        ],
        out_shape=[
            jax.ShapeDtypeStruct((B, L // BLOCK, Q_DIM, BLOCK), BF16),
            jax.ShapeDtypeStruct((B, L, KV_DIM), BF16),
            jax.ShapeDtypeStruct((B, L // BLOCK, KV_DIM, BLOCK), BF16),
        ],
        compiler_params=pltpu.CompilerParams(
            dimension_semantics=("arbitrary", "arbitrary"), vmem_limit_bytes=VMEM_LIMIT),
        name="qkv_rope" if rope else "qkv_ctx",
    )(*args)


def _attn_kernel(*refs, window, nq, n_ctx):
    if window:
        (sink_ref, qt_ref, kp_ref, kc_ref, kn_ref, vp_ref, vc_ref, vn_ref,
         kx_ref, vx_ref, ot_ref, kbuf, vtbuf) = refs
    else:
        sink_ref, qt_ref, kx_ref, vx_ref, ot_ref, kbuf, vtbuf = refs
    width = HEADS_PER_DOT * BLOCK
    kbuf[0:n_ctx, :] = kx_ref[0]
    for j in range(n_ctx // BLOCK):
        vtbuf[:, j * BLOCK:(j + 1) * BLOCK] = vx_ref[0, j]
    if window:
        for j, (kr, vr) in enumerate(((kp_ref, vp_ref), (kc_ref, vc_ref), (kn_ref, vn_ref))):
            kbuf[n_ctx + j * BLOCK:n_ctx + (j + 1) * BLOCK, :] = kr[0]
            vtbuf[:, n_ctx + j * BLOCK:n_ctx + (j + 1) * BLOCK] = vr[0, 0]
        n = pl.program_id(1)
        kj = lax.broadcasted_iota(jnp.int32, (BLOCK, width), 0)
        qi = lax.broadcasted_iota(jnp.int32, (BLOCK, width), 1) % BLOCK
        bias_prev = jnp.where((kj >= qi) & (n > 0), 0.0, NEG_INF).astype(F32)
        bias_next = jnp.where((kj <= qi) & (n < nq - 1), 0.0, NEG_INF).astype(F32)
    lane = lax.broadcasted_iota(jnp.int32, (1, width), 1)
    ones_rows = jnp.ones((16, kbuf.shape[0]), BF16)
    for h0 in range(0, N_HEADS, HEADS_PER_DOT):
        cols = []
        for h in range(h0, h0 + HEADS_PER_DOT):
            g = h // Q_PER_KV
            parts = [qt_ref[0, 0, h * HEAD_DIM:(h + 1) * HEAD_DIM, :]]
            if g > 0:
                parts.insert(0, jnp.zeros((g * HEAD_DIM, BLOCK), BF16))
            if g < N_KV_HEADS - 1:
                parts.append(jnp.zeros(((N_KV_HEADS - 1 - g) * HEAD_DIM, BLOCK), BF16))
            cols.append(jnp.concatenate(parts, axis=0) if len(parts) > 1 else parts[0])
        q_exp = jnp.concatenate(cols, axis=1) if len(cols) > 1 else cols[0]
        sink = jnp.full((1, width), sink_ref[h0], F32)
        for r in range(1, HEADS_PER_DOT):
            sink = jnp.where(lane >= r * BLOCK, sink_ref[h0 + r], sink)
        sink = sink * LOG2E
        s = jnp.dot(kbuf[...], q_exp, preferred_element_type=F32)
        if window:
            s = jnp.concatenate([
                s[:n_ctx], s[n_ctx:n_ctx + BLOCK] + bias_prev,
                s[n_ctx + BLOCK:n_ctx + 2 * BLOCK], s[n_ctx + 2 * BLOCK:] + bias_next], axis=0)
        mx = jnp.maximum(jnp.max(s, axis=0, keepdims=True), sink)
        p = jnp.exp2(s - mx).astype(BF16)
        sink_p = jnp.exp2(sink - mx)
        gw = Q_PER_KV * BLOCK
        for g0 in range(0, HEADS_PER_DOT // Q_PER_KV):
            g = h0 // Q_PER_KV + g0
            gl = slice(g0 * gw, (g0 + 1) * gw)
            v_ext = jnp.concatenate([vtbuf[g * HEAD_DIM:(g + 1) * HEAD_DIM, :], ones_rows], axis=0)
            pv = jnp.dot(v_ext, p[:, gl], preferred_element_type=F32)
            o_t = pv[:HEAD_DIM, :] / (pv[HEAD_DIM:HEAD_DIM + 1, :] + sink_p[:, gl])
            for r in range(Q_PER_KV):
                h = g * Q_PER_KV + r
                ot_ref[0, 0, h * HEAD_DIM:(h + 1) * HEAD_DIM, :] = (
                    o_t[:, r * BLOCK:(r + 1) * BLOCK].astype(BF16))


def _attn(q_t, k, v_t, kx, vx_t, sinks):
    B, nq = q_t.shape[:2]
    C = kx.shape[1]
    window = k is not None
    smem = pl.BlockSpec(memory_space=pltpu.SMEM)
    q_spec = pl.BlockSpec((1, 1, Q_DIM, BLOCK), lambda b, n: (b, n, 0, 0))
    kx_spec = pl.BlockSpec((1, C, KV_DIM), lambda b, n: (b, 0, 0))
    vx_spec = pl.BlockSpec((1, C // BLOCK, KV_DIM, BLOCK), lambda b, n: (b, 0, 0, 0))
    if window:
        prev = lambda n: jnp.maximum(n - 1, 0)
        nxt = lambda n: jnp.minimum(n + 1, nq - 1)
        k_spec = lambda f: pl.BlockSpec((1, BLOCK, KV_DIM), lambda b, n: (b, f(n), 0))
        v_spec = lambda f: pl.BlockSpec((1, 1, KV_DIM, BLOCK), lambda b, n: (b, f(n), 0, 0))
        same = lambda n: n
        in_specs = [smem, q_spec, k_spec(prev), k_spec(same), k_spec(nxt),
                    v_spec(prev), v_spec(same), v_spec(nxt), kx_spec, vx_spec]
        args = (sinks, q_t, k, k, k, v_t, v_t, v_t, kx, vx_t)
        n_keys = C + 3 * BLOCK
    else:
        in_specs = [smem, q_spec, kx_spec, vx_spec]
        args = (sinks, q_t, kx, vx_t)
        n_keys = C
    return pl.pallas_call(
        functools.partial(_attn_kernel, window=window, nq=nq, n_ctx=C),
        grid=(B, nq),
        in_specs=in_specs,
        out_specs=q_spec,
        out_shape=jax.ShapeDtypeStruct((B, nq, Q_DIM, BLOCK), BF16),
        scratch_shapes=[pltpu.VMEM((n_keys, KV_DIM), BF16), pltpu.VMEM((KV_DIM, n_keys), BF16)],
        compiler_params=pltpu.CompilerParams(
            dimension_semantics=("arbitrary", "arbitrary"), vmem_limit_bytes=VMEM_LIMIT),
        name="attn_window" if window else "attn_ctx",
    )(*args)


def _mix_ffn_kernel(*refs, ssm, final, n_sub):
    if ssm:
        (x_ref, mod_ref, yf_ref, yb_ref, z_ref, xs_ref, dsk_ref, ng_ref,
         wo_ref, gf_ref, wg_ref, wu_ref, wd_ref) = refs[:13]
        rest = refs[13:]
    else:
        (x_ref, mod_ref, o_ref_in, wo_ref, gf_ref, wg_ref, wu_ref, wd_ref) = refs[:8]
        rest = refs[8:]
    if final:
        fg_ref, out_ref = rest
    else:
        (out_ref,) = rest
    m = mod_ref[0]
    ts = x_ref.shape[1] // n_sub
    for sub in range(n_sub):
        rs = slice(sub * ts, (sub + 1) * ts)
        if ssm:
            mix = None
            for g in range(SSM_GROUPS):
                gs = slice(g * GROUP_DIM, (g + 1) * GROUP_DIM)
                yg = ((yf_ref[0, rs, gs].astype(F32) + yb_ref[0, rs, gs].astype(F32)
                       + dsk_ref[:, gs] * xs_ref[0, rs, gs].astype(F32))
                      * _silu(z_ref[0, rs, gs].astype(F32)))
                ms = jnp.mean(yg * yg, axis=-1, keepdims=True)
                part = (yg * lax.rsqrt(ms + EPS) * ng_ref[:, gs]).astype(BF16)
                term = jnp.dot(part, wo_ref[gs, :], preferred_element_type=F32)
                mix = term if mix is None else mix + term
        else:
            blocks = range(sub * ts // BLOCK, (sub + 1) * ts // BLOCK)
            o_t = jnp.concatenate([o_ref_in[0, j] for j in blocks], axis=1)
            mix = lax.dot_general(o_t, wo_ref[...], _TN, preferred_element_type=F32)
        x1 = x_ref[0, rs, :] + _mod_chunk(m, 2) * mix
        h2 = _rms_mod(x1, gf_ref[...], _mod_chunk(m, 4), _mod_chunk(m, 3)).astype(BF16)
        ffn = None
        for f0 in range(0, D_FF, FF_CHUNK):
            fs = slice(f0, min(f0 + FF_CHUNK, D_FF))
            gate = jnp.dot(h2, wg_ref[:, fs], preferred_element_type=F32)
            up = jnp.dot(h2, wu_ref[:, fs], preferred_element_type=F32)
            act = (_silu(gate) * up).astype(BF16)
            part = jnp.dot(act, wd_ref[fs, :], preferred_element_type=F32)
            ffn = part if ffn is None else ffn + part
        x2 = x1 + _mod_chunk(m, 5) * ffn
        if final:
            ms = jnp.mean(x2 * x2, axis=-1, keepdims=True)
            x2 = x2 * lax.rsqrt(ms + EPS) * fg_ref[...]
        out_ref[0, rs, :] = x2


def _mix_ffn(x, mod, mod_row, mixer_in, w_mix, g_ffn, w_gate, w_up, w_down, layer,
             ssm_norm_g=None, d_skip=None, final_g=None, tm=256, n_sub=1):
    B, L, _ = x.shape
    tm = min(L, tm)
    ssm = ssm_norm_g is not None
    final = final_g is not None
    row = lambda width: pl.BlockSpec((1, tm, width), lambda b, i: (b, i, 0))
    in_specs = [row(D_MODEL), pl.BlockSpec((1, 1, 6 * D_MODEL), lambda b, i: (mod_row(b), 0, 0))]
    args = [x, mod]
    if ssm:
        in_specs += [row(D_INNER)] * 4 + [_const_spec((1, D_INNER))] * 2
        args += list(mixer_in) + [d_skip, ssm_norm_g]
    else:
        in_specs += [pl.BlockSpec((1, tm // BLOCK, Q_DIM, BLOCK), lambda b, i: (b, i, 0, 0))]
        args += [mixer_in]
    in_specs += [_layer_spec(w_mix, 0), _const_spec((1, D_MODEL)), _layer_spec(w_gate, layer),
                 _layer_spec(w_up, layer), _layer_spec(w_down, layer)]
    args += [w_mix, g_ffn, w_gate, w_up, w_down]
    if final:
        in_specs += [_const_spec((1, D_MODEL))]
        args += [final_g]
    return pl.pallas_call(
        functools.partial(_mix_ffn_kernel, ssm=ssm, final=final, n_sub=n_sub),
        grid=(B, L // tm),
        in_specs=in_specs,
        out_specs=row(D_MODEL),
        out_shape=jax.ShapeDtypeStruct((B, L, D_MODEL), F32),
        compiler_params=pltpu.CompilerParams(
            dimension_semantics=("arbitrary", "arbitrary"), vmem_limit_bytes=VMEM_LIMIT),
        name="ssm_out_ffn" if ssm else "attn_out_ffn",
    )(*args)


def _inproj_kernel(x_ref, xp_ref, xn_ref, mod_ref, g_ref, w_ref, wdt_ref, dtb_ref, cw_ref, cb_ref,
                   z_ref, xa_ref, dt_ref, *, n_tiles):
    i = pl.program_id(1)
    m = mod_ref[0]
    g, scale, shift = g_ref[...], _mod_chunk(m, 1), _mod_chunk(m, 0)
    tm = x_ref.shape[1]
    h = _rms_mod(x_ref[0], g, scale, shift).astype(BF16)
    x_ext = jnp.concatenate([xp_ref[0], x_ref[0], xn_ref[0]], axis=0)
    h_ext = _rms_mod(x_ext, g, scale, shift).astype(BF16)
    z_ref[0] = jnp.dot(h, w_ref[:, :D_INNER], preferred_element_type=F32).astype(BF16)
    raw = jnp.dot(h, wdt_ref[...], preferred_element_type=F32) + dtb_ref[...]
    dt_ref[0] = jnp.maximum(raw, 0.0) + jnp.log1p(jnp.exp(-jnp.abs(raw)))
    keep_prev = jnp.where(i > 0, 1.0, 0.0)
    keep_next = jnp.where(i < n_tiles - 1, 1.0, 0.0)
    cstep = 512
    for j in range(CONV_DIM // cstep):
        sl = slice(j * cstep, (j + 1) * cstep)
        u = jnp.dot(h_ext, w_ref[:, D_INNER + j * cstep:D_INNER + (j + 1) * cstep],
                    preferred_element_type=F32)
        u = jnp.concatenate([u[0:SUBLANES] * keep_prev, u[SUBLANES:SUBLANES + tm],
                             u[SUBLANES + tm:] * keep_next], axis=0)
        rows = tm + 2 * SUBLANES
        below = pltpu.roll(u, 1, 0)[SUBLANES:SUBLANES + tm]
        above = pltpu.roll(u, rows - 1, 0)[SUBLANES:SUBLANES + tm]
        conv = (below * cw_ref[0:1, sl] + u[SUBLANES:SUBLANES + tm] * cw_ref[1:2, sl]
                + above * cw_ref[2:3, sl] + cb_ref[:, sl])
        xa_ref[0, :, sl] = _silu(conv).astype(BF16)


def _inproj(x, mod, mod_row, *, g, w_in, w_dt, dt_bias, conv_w, conv_b):
    B, L, _ = x.shape
    tm = min(L, 256)
    n_tiles = L // tm
    per = tm // SUBLANES
    row = lambda width: pl.BlockSpec((1, tm, width), lambda b, i: (b, i, 0))
    return pl.pallas_call(
        functools.partial(_inproj_kernel, n_tiles=n_tiles),
        grid=(B, n_tiles),
        in_specs=[row(D_MODEL),
                  pl.BlockSpec((1, SUBLANES, D_MODEL), lambda b, i: (b, jnp.maximum(i * per - 1, 0), 0)),
                  pl.BlockSpec((1, SUBLANES, D_MODEL),
                               lambda b, i: (b, jnp.minimum((i + 1) * per, L // SUBLANES - 1), 0)),
                  pl.BlockSpec((1, 1, 6 * D_MODEL), lambda b, i: (mod_row(b), 0, 0)),
                  _const_spec((1, D_MODEL)), _const_spec(w_in.shape), _const_spec(w_dt.shape),
                  _const_spec((1, 2 * LANES)), _const_spec((8, CONV_DIM)), _const_spec((1, CONV_DIM))],
        out_specs=[row(D_INNER), row(CONV_DIM), row(2 * LANES)],
        out_shape=[jax.ShapeDtypeStruct((B, L, D_INNER), BF16),
                   jax.ShapeDtypeStruct((B, L, CONV_DIM), BF16),
                   jax.ShapeDtypeStruct((B, L, 2 * LANES), F32)],
        compiler_params=pltpu.CompilerParams(
            dimension_semantics=("arbitrary", "arbitrary"), vmem_limit_bytes=VMEM_LIMIT),
        name="ssm_inproj",
    )(x, x, x, mod, g, w_in, w_dt, dt_bias, conv_w, conv_b)


def _split3(v):
    hi = v.astype(BF16)
    r1 = v - hi.astype(F32)
    mid = r1.astype(BF16)
    lo = (r1 - mid.astype(F32)).astype(BF16)
    return hi, mid, lo


def _ssd_kernel(*refs, reverse, nc, has_init, want_y):
    (xa_ref, dt_ref, alog_ref, e64_ref), rest = refs[:4], refs[4:]
    if has_init:
        h0_ref, rest = rest[0], rest[1:]
    if want_y:
        y_ref, rest = rest[0], rest[1:]
    hfin_ref, state, ce, tbuf = rest
    c = pl.program_id(1)
    last = 0 if reverse else CHUNK - 1

    @pl.when(c == 0)
    def _():
        state[...] = h0_ref[0] if has_init else jnp.zeros(state.shape, F32)

    lane = lax.broadcasted_iota(jnp.int32, (CHUNK, LANES), 1)
    row = lax.broadcasted_iota(jnp.int32, (CHUNK, LANES), 0)
    lane_valid = (lane, (lane >= row) if reverse else (lane <= row))
    _prepare_decay(dt_ref, alog_ref, e64_ref, ce, tbuf, lane_valid, last)
    _scan_chunk(xa_ref, y_ref if want_y else None, state, ce, tbuf, lane_valid, last)

    @pl.when(c == nc - 1)
    def _():
        hfin_ref[0] = state[...]


def _prepare_decay(dt_ref, alog_ref, e64_ref, ce, tbuf, lane_valid, last):
    lane, valid = lane_valid
    dt = jnp.where(lane < SSM_HEADS, dt_ref[0], 0.0)
    a = dt * (-LOG2E * jnp.exp(alog_ref[0]))
    tri = jnp.where(valid, 1.0, 0.0).astype(BF16)
    a_hi, a_mid, a_lo = _split3(a)
    cum = (jnp.dot(tri, a_hi, preferred_element_type=F32)
           + jnp.dot(tri, a_mid, preferred_element_type=F32)
           + jnp.dot(tri, a_lo, preferred_element_type=F32))
    c_hi, c_mid, c_lo = _split3(cum)
    pieces = (c_hi.astype(F32) + pltpu.roll(c_mid.astype(F32), SSM_HEADS, 1)
              + pltpu.roll(c_lo.astype(F32), 2 * SSM_HEADS, 1)).astype(BF16)
    ce[...] = jnp.dot(pieces, e64_ref[...], preferred_element_type=F32)
    tb = (cum + pltpu.roll(dt, SSM_HEADS, 1)).T
    cum_t = tb[0:SSM_HEADS, :]
    dt_t = tb[SSM_HEADS:2 * SSM_HEADS, :]
    tbuf[0:SSM_HEADS, :] = cum_t - jnp.log2(dt_t)
    tbuf[SSM_HEADS:2 * SSM_HEADS, :] = jnp.exp2(cum_t[:, last:last + 1] - cum_t) * dt_t


def _scan_chunk(xa_ref, y_ref, state, ce, tbuf, lane_valid, last):
    lane, valid = lane_valid
    want_y = y_ref is not None
    lane_g = lax.broadcasted_iota(jnp.int32, (CHUNK, GROUP_DIM), 1)
    head_mask = [jnp.where((lane_g >= hl * SSM_HEAD_DIM) & (lane_g < (hl + 1) * SSM_HEAD_DIM),
                           1.0, 0.0).astype(BF16) for hl in range(HEADS_PER_GROUP)]
    low_half = lane < SSM_HEAD_DIM
    for g in range(SSM_GROUPS):
        gs = slice(g * GROUP_DIM, (g + 1) * GROUP_DIM)
        b_g = xa_ref[0, :, D_INNER + g * D_STATE:D_INNER + (g + 1) * D_STATE]
        c_g = xa_ref[0, :, D_INNER + BC_DIM + g * D_STATE:D_INNER + BC_DIM + (g + 1) * D_STATE]
        b_t = b_g.astype(F32).T
        x_g = xa_ref[0, :, gs]
        if want_y:
            cb_mat = lax.dot_general(c_g, b_g, _NT, preferred_element_type=F32)
            y_g = (jnp.dot(c_g, state[g].astype(BF16), preferred_element_type=F32)
                   * jnp.exp2(ce[:, gs]))
        st = jnp.zeros((D_STATE, GROUP_DIM), F32)
        for hp in range(HEADS_PER_GROUP // 2):
            pair = g * (HEADS_PER_GROUP // 2) + hp
            lhs_y, lhs_s, rhs = [], [], []
            if want_y:
                ce_pair = ce[:, pair * LANES:(pair + 1) * LANES]
                swapped = pltpu.roll(ce_pair, SSM_HEAD_DIM, 1)
            for hh in range(2):
                hl = 2 * hp + hh
                h = g * HEADS_PER_GROUP + hl
                if want_y:
                    own = low_half if hh == 0 else jnp.logical_not(low_half)
                    cum_i = jnp.where(own, ce_pair, swapped)
                    dec = jnp.exp2(jnp.where(valid, cum_i - tbuf[h:h + 1, :], NEG_INF))
                    lhs_y.append((dec * cb_mat).astype(BF16))
                lhs_s.append((b_t * tbuf[SSM_HEADS + h:SSM_HEADS + h + 1, :]).astype(BF16))
                rhs.append(x_g * head_mask[hl])
            rhs = jnp.concatenate(rhs, axis=0)
            if want_y:
                y_g = y_g + jnp.dot(jnp.concatenate(lhs_y, axis=1), rhs, preferred_element_type=F32)
            st = st + jnp.dot(jnp.concatenate(lhs_s, axis=1), rhs, preferred_element_type=F32)
        if want_y:
            y_ref[0, :, gs] = y_g.astype(BF16)
        state[g] = state[g] * jnp.exp2(ce[last:last + 1, gs]) + st


def _ssd(xact, dt, a_log, e64, h0, direction, want_y):
    B, L, _ = xact.shape
    nc = L // CHUNK
    reverse = direction == 1
    has_init = h0 is not None
    chunk = (lambda c: nc - 1 - c) if reverse else (lambda c: c)
    st_shape = (SSM_GROUPS, D_STATE, GROUP_DIM)
    st_spec = pl.BlockSpec((1,) + st_shape, lambda b, c: (b, 0, 0, 0))
    in_specs = [
        pl.BlockSpec((1, CHUNK, CONV_DIM), lambda b, c: (b, chunk(c), 0)),
        pl.BlockSpec((1, CHUNK, LANES), lambda b, c: (b, chunk(c), direction)),
        pl.BlockSpec((1, 1, LANES), lambda b, c: (direction, 0, 0)),
        _const_spec(e64.shape),
    ]
    args = [xact, dt, a_log, e64]
    if has_init:
        in_specs.append(st_spec)
        args.append(h0)
    out_specs, out_shape = [], []
    if want_y:
        out_specs.append(pl.BlockSpec((1, CHUNK, D_INNER), lambda b, c: (b, chunk(c), 0)))
        out_shape.append(jax.ShapeDtypeStruct((B, L, D_INNER), BF16))
    out_specs.append(st_spec)
    out_shape.append(jax.ShapeDtypeStruct((B,) + st_shape, F32))
    scratch = [pltpu.VMEM(st_shape, F32), pltpu.VMEM((CHUNK, D_INNER), F32),
               pltpu.VMEM((2 * SSM_HEADS, CHUNK), F32)]
    return pl.pallas_call(
        functools.partial(_ssd_kernel, reverse=reverse, nc=nc, has_init=has_init, want_y=want_y),
        grid=(B, nc),
        in_specs=in_specs,
        out_specs=out_specs,
        out_shape=out_shape,
        scratch_shapes=scratch,
        compiler_params=pltpu.CompilerParams(
            dimension_semantics=("arbitrary", "arbitrary"), vmem_limit_bytes=VMEM_LIMIT),
        name="ssd_bwd" if reverse else "ssd_fwd",
    )(*args)


def _expand_matrix(width):
    k = np.arange(LANES)[:, None]
    col_head = (np.arange(SSM_HEADS * width) // width)[None, :]
    return jnp.asarray((k % SSM_HEADS == col_head) & (k < 3 * SSM_HEADS), dtype=BF16)


def kernel(x, c, ctx, c_ctx, ada_w, ada_b, norm_mix_g, norm_ffn_g, attn_w_qkv, attn_w_o, attn_sinks,
           ssm_w_in, ssm_conv_w, ssm_conv_b, ssm_dt_bias, ssm_A_log, ssm_D, ssm_norm_g, ssm_w_out,
           ffn_w_gate, ffn_w_up, ffn_w_down, final_norm_g):
    B, L, _ = x.shape
    assert x.shape == (B, L, D_MODEL) and B <= CTX_ROW and L % 512 == 0
    bf = lambda w: w.astype(BF16)
    row2 = lambda v: v.reshape(1, -1)

    cc = jnp.zeros((MOD_ROWS, D_MODEL), F32).at[:B].set(c).at[CTX_ROW].set(c_ctx)
    mod = _adaln(cc, ada_w, ada_b)
    mods = [mod[i].reshape(MOD_ROWS, 1, 6 * D_MODEL) for i in range(DEPTH)]
    x_row = lambda b: b
    ctx_row = lambda b: CTX_ROW

    w_qkv = attn_w_qkv[0]
    w_q_t = bf(w_qkv[:, :Q_DIM].T)
    w_k = bf(w_qkv[:, Q_DIM:Q_DIM + KV_DIM])
    w_v_t = bf(w_qkv[:, Q_DIM + KV_DIM:].T)
    g_mix = row2(norm_mix_g[0])
    q, k, v = _qkv(x, mods[0], g_mix, w_q_t, w_k, w_v_t, _rope_tables(L), x_row)
    qc, kc, vc = _qkv(ctx, mods[0], g_mix, w_q_t, w_k, w_v_t, None, ctx_row)
    sinks = attn_sinks[0].astype(F32)
    o = _attn(q, k, v, kc, vc, sinks)
    oc = _attn(qc, None, None, kc, vc, sinks)
    w_gate, w_up, w_down = bf(ffn_w_gate), bf(ffn_w_up), bf(ffn_w_down)
    ffn0 = (bf(attn_w_o), row2(norm_ffn_g[0]), w_gate, w_up, w_down, 0)
    x = _mix_ffn(x, mods[0], x_row, o, *ffn0)
    ctx = _mix_ffn(ctx, mods[0], ctx_row, oc, *ffn0)

    w_in = ssm_w_in[0]
    w_dt_raw = w_in[:, D_INNER + CONV_DIM:]
    w_dt = jnp.zeros((D_MODEL, 2 * LANES), F32)
    w_dt = w_dt.at[:, :SSM_HEADS].set(w_dt_raw[:, :SSM_HEADS])
    w_dt = bf(w_dt.at[:, LANES:LANES + SSM_HEADS].set(w_dt_raw[:, SSM_HEADS:]))
    dt_bias = jnp.zeros((2, LANES), F32).at[:, :SSM_HEADS].set(ssm_dt_bias[0]).reshape(1, 2 * LANES)
    a_log = jnp.zeros((2, 1, LANES), F32).at[:, 0, :SSM_HEADS].set(ssm_A_log[0])
    conv_w = jnp.zeros((8, CONV_DIM), F32).at[:3].set(ssm_conv_w[0])
    conv_b = row2(ssm_conv_b[0])
    d_skip = row2(jnp.repeat(ssm_D[0], SSM_HEAD_DIM))
    e64 = _expand_matrix(SSM_HEAD_DIM)
    g_mix1 = row2(norm_mix_g[1])

    inproj = functools.partial(_inproj, g=g_mix1, w_in=bf(w_in), w_dt=w_dt, dt_bias=dt_bias,
                               conv_w=conv_w, conv_b=conv_b)
    z, xact, dt = inproj(x, mods[1], x_row)
    _, xact_c, dt_c = inproj(ctx, mods[1], ctx_row)
    ssd = functools.partial(_ssd, a_log=a_log, e64=e64)
    (h_f,) = ssd(xact_c, dt_c, h0=None, direction=0, want_y=False)
    (h_b,) = ssd(xact_c, dt_c, h0=None, direction=1, want_y=False)
    y_f, _ = ssd(xact, dt, h0=h_f, direction=0, want_y=True)
    y_b, _ = ssd(xact, dt, h0=h_b, direction=1, want_y=True)
    return _mix_ffn(x, mods[1], x_row, (y_f, y_b, z, xact), bf(ssm_w_out), row2(norm_ffn_g[1]),
                    w_gate, w_up, w_down, 1,
                    ssm_norm_g=row2(ssm_norm_g[0]), d_skip=d_skip, final_g=row2(final_norm_g))
```

```python
import functools
import math

import numpy as np

import jax
import jax.numpy as jnp
from jax import lax
from jax.experimental import pallas as pl
from jax.experimental.pallas import tpu as pltpu

F32 = jnp.float32
BF16 = jnp.bfloat16

D_MODEL = 1024
DEPTH = 2
GRID_W = 64
EPS = 1e-6

HEAD_DIM = 64
N_HEADS = D_MODEL // HEAD_DIM
N_KV_HEADS = N_HEADS // 4
Q_PER_KV = N_HEADS // N_KV_HEADS
Q_DIM = N_HEADS * HEAD_DIM
KV_DIM = N_KV_HEADS * HEAD_DIM
QKV_DIM = Q_DIM + 2 * KV_DIM
WINDOW = 128
BLOCK = 128
ROPE_FREQS = HEAD_DIM // 4
ROPE_BASE = 10000.0

D_INNER = 2 * D_MODEL
SSM_HEAD_DIM = 64
SSM_HEADS = D_INNER // SSM_HEAD_DIM
SSM_GROUPS = 8
HEADS_PER_GROUP = SSM_HEADS // SSM_GROUPS
GROUP_DIM = D_INNER // SSM_GROUPS
D_STATE = 128
CHUNK = 128
BC_DIM = SSM_GROUPS * D_STATE
CONV_DIM = D_INNER + 2 * BC_DIM
D_FF = ((8 * D_MODEL // 3 + 255) // 256) * 256

LANES = 128
SUBLANES = 8
MOD_ROWS = 8
CTX_ROW = 4
VMEM_LIMIT = 48 * 1024 * 1024
NEG_INF = float("-inf")
LOG2E = math.log2(math.e)
HEADS_PER_DOT = N_HEADS


def _const_spec(shape):
    nd = len(shape)
    return pl.BlockSpec(shape, lambda *_: (0,) * nd, pipeline_mode=pl.Buffered(1))


def _layer_spec(stacked, layer):
    return pl.BlockSpec((None,) + stacked.shape[1:], lambda *_: (layer, 0, 0),
                        pipeline_mode=pl.Buffered(1))


def _silu(v):
    return v * jax.nn.sigmoid(v)


def _rms_mod(x, g, scale, shift):
    ms = jnp.mean(x * x, axis=-1, keepdims=True)
    return (x * lax.rsqrt(ms + EPS) * g) * (1.0 + scale) + shift


def _mod_chunk(m, i):
    return m[:, i * D_MODEL:(i + 1) * D_MODEL]


def _adaln_kernel(c_ref, w_ref, b_ref, o_ref):
    s = _silu(c_ref[...])
    o_ref[0] = jnp.dot(s, w_ref[0], preferred_element_type=F32) + b_ref[0]


def _adaln(cc, ada_w, ada_b):
    tn = 1536
    n = 6 * D_MODEL
    return pl.pallas_call(
        _adaln_kernel,
        grid=(DEPTH, n // tn),
        in_specs=[
            pl.BlockSpec((MOD_ROWS, D_MODEL), lambda i, j: (0, 0)),
            pl.BlockSpec((1, D_MODEL, tn), lambda i, j: (i, 0, j)),
            pl.BlockSpec((1, 1, tn), lambda i, j: (i, 0, j)),
        ],
        out_specs=pl.BlockSpec((1, MOD_ROWS, tn), lambda i, j: (i, 0, j)),
        out_shape=jax.ShapeDtypeStruct((DEPTH, MOD_ROWS, n), F32),
        compiler_params=pltpu.CompilerParams(
            dimension_semantics=("arbitrary", "arbitrary"), vmem_limit_bytes=VMEM_LIMIT),
        name="adaln",
    )(cc, ada_w, ada_b.reshape(DEPTH, 1, n))


_NT = (((1,), (1,)), ((), ()))
_TN = (((0,), (0,)), ((), ()))


def _qkv_kernel(*refs, rope):
    if rope:
        (x_ref, mod_ref, g_ref, wq_ref, wk_ref, wv_ref, cos_t_ref, sin_t_ref,
         cos_ref, sa_ref, sb_ref, qt_ref, k_ref, vt_ref) = refs
    else:
        x_ref, mod_ref, g_ref, wq_ref, wk_ref, wv_ref, qt_ref, k_ref, vt_ref = refs
    m = mod_ref[0]
    h = _rms_mod(x_ref[0], g_ref[...], _mod_chunk(m, 1), _mod_chunk(m, 0)).astype(BF16)
    q_t = lax.dot_general(wq_ref[...], h, _NT, preferred_element_type=F32)
    k = jnp.dot(h, wk_ref[...], preferred_element_type=F32)
    v_t = lax.dot_general(wv_ref[...], h, _NT, preferred_element_type=F32).astype(BF16)
    n_blk = x_ref.shape[1] // BLOCK

    def put(ref, rows, val):
        for j in range(n_blk):
            ref[0, j, rows, :] = val[:, j * BLOCK:(j + 1) * BLOCK]

    put(vt_ref, slice(0, KV_DIM), v_t)
    scale = HEAD_DIM ** -0.5 * LOG2E
    if rope:
        f = ROPE_FREQS
        for head in range(N_HEADS):
            for axis in range(2):
                r0 = head * HEAD_DIM + axis * 2 * f
                cs = cos_t_ref[axis * f:(axis + 1) * f, :]
                sn = sin_t_ref[axis * f:(axis + 1) * f, :]
                x1 = q_t[r0:r0 + f, :]
                x2 = q_t[r0 + f:r0 + 2 * f, :]
                put(qt_ref, slice(r0, r0 + f), ((x1 * cs - x2 * sn) * scale).astype(BF16))
                put(qt_ref, slice(r0 + f, r0 + 2 * f), ((x2 * cs + x1 * sn) * scale).astype(BF16))
        for blk in range(KV_DIM // LANES):
            t = k[:, blk * LANES:(blk + 1) * LANES]
            t = (t * cos_ref[...] + pltpu.roll(t, LANES - f, 1) * sa_ref[...]
                 + pltpu.roll(t, f, 1) * sb_ref[...])
            k_ref[0, :, blk * LANES:(blk + 1) * LANES] = t.astype(BF16)
    else:
        put(qt_ref, slice(0, Q_DIM), (q_t * scale).astype(BF16))
        k_ref[0] = k.astype(BF16)


def _rope_tables(L):
    f32 = np.float32
    rows = L // GRID_W
    row = np.repeat(np.arange(rows, dtype=f32), GRID_W)
    col = np.tile(np.arange(GRID_W, dtype=f32), rows)
    inv = (f32(ROPE_BASE) ** (-np.arange(ROPE_FREQS, dtype=f32) / f32(ROPE_FREQS))).astype(f32)
    ang_r = (row[:, None] * inv).astype(f32)
    ang_c = (col[:, None] * inv).astype(f32)
    zero = np.zeros_like(ang_r)

    def lanes(r_first, r_second, c_first, c_second):
        head = np.concatenate([r_first, r_second, c_first, c_second], axis=-1)
        return np.tile(head, (1, LANES // HEAD_DIM)).astype(f32)

    cr, sr, cc, sc = np.cos(ang_r), np.sin(ang_r), np.cos(ang_c), np.sin(ang_c)
    cos_t = np.ascontiguousarray(np.concatenate([cr, cc], axis=1).T).astype(f32)
    sin_t = np.ascontiguousarray(np.concatenate([sr, sc], axis=1).T).astype(f32)
    return (cos_t, sin_t, lanes(cr, cr, cc, cc), lanes(-sr, zero, -sc, zero),
            lanes(zero, sr, zero, sc))


def _qkv(x, mod, g, w_q_t, w_k, w_v_t, tables, mod_row):
    B, L, _ = x.shape
    tm = min(L, 1024)
    rope = tables is not None
    in_specs = [
        pl.BlockSpec((1, tm, D_MODEL), lambda b, i: (b, i, 0)),
        pl.BlockSpec((1, 1, 6 * D_MODEL), lambda b, i: (mod_row(b), 0, 0)),
        _const_spec((1, D_MODEL)),
        _const_spec(w_q_t.shape), _const_spec(w_k.shape), _const_spec(w_v_t.shape),
    ]
    args = [x, mod, g, w_q_t, w_k, w_v_t]
    if rope:
        in_specs += [pl.BlockSpec((2 * ROPE_FREQS, tm), lambda b, i: (0, i))] * 2
        in_specs += [pl.BlockSpec((tm, LANES), lambda b, i: (i, 0))] * 3
        args += list(tables)
    return pl.pallas_call(
        functools.partial(_qkv_kernel, rope=rope),
        grid=(B, L // tm),
        in_specs=in_specs,
        out_specs=[
            pl.BlockSpec((1, tm // BLOCK, Q_DIM, BLOCK), lambda b, i: (b, i, 0, 0)),
            pl.BlockSpec((1, tm, KV_DIM), lambda b, i: (b, i, 0)),
            pl.BlockSpec((1, tm // BLOCK, KV_DIM, BLOCK), lambda b, i: (b, i, 0, 0)),
        ],
        out_shape=[
            jax.ShapeDtypeStruct((B, L // BLOCK, Q_DIM, BLOCK), BF16),
            jax.ShapeDtypeStruct((B, L, KV_DIM), BF16),
            jax.ShapeDtypeStruct((B, L // BLOCK, KV_DIM, BLOCK), BF16),
        ],
        compiler_params=pltpu.CompilerParams(
            dimension_semantics=("arbitrary", "arbitrary"), vmem_limit_bytes=VMEM_LIMIT),
        name="qkv_rope" if rope else "qkv_ctx",
    )(*args)


def _attn_kernel(*refs, window, nq, n_ctx):
    if window:
        (sink_ref, qt_ref, kp_ref, kc_ref, kn_ref, vp_ref, vc_ref, vn_ref,
         kx_ref, vx_ref, ot_ref, kbuf, vtbuf) = refs
    else:
        sink_ref, qt_ref, kx_ref, vx_ref, ot_ref, kbuf, vtbuf = refs
    width = HEADS_PER_DOT * BLOCK
    kbuf[0:n_ctx, :] = kx_ref[0]
    for j in range(n_ctx // BLOCK):
        vtbuf[:, j * BLOCK:(j + 1) * BLOCK] = vx_ref[0, j]
    if window:
        for j, (kr, vr) in enumerate(((kp_ref, vp_ref), (kc_ref, vc_ref), (kn_ref, vn_ref))):
            kbuf[n_ctx + j * BLOCK:n_ctx + (j + 1) * BLOCK, :] = kr[0]
            vtbuf[:, n_ctx + j * BLOCK:n_ctx + (j + 1) * BLOCK] = vr[0, 0]
        n = pl.program_id(1)
        kj = lax.broadcasted_iota(jnp.int32, (BLOCK, width), 0)
        qi = lax.broadcasted_iota(jnp.int32, (BLOCK, width), 1) % BLOCK
        bias_prev = jnp.where((kj >= qi) & (n > 0), 0.0, NEG_INF).astype(F32)
        bias_next = jnp.where((kj <= qi) & (n < nq - 1), 0.0, NEG_INF).astype(F32)
    lane = lax.broadcasted_iota(jnp.int32, (1, width), 1)
    ones_rows = jnp.ones((16, kbuf.shape[0]), BF16)
    for h0 in range(0, N_HEADS, HEADS_PER_DOT):
        cols = []
        for h in range(h0, h0 + HEADS_PER_DOT):
            g = h // Q_PER_KV
            parts = [qt_ref[0, 0, h * HEAD_DIM:(h + 1) * HEAD_DIM, :]]
            if g > 0:
                parts.insert(0, jnp.zeros((g * HEAD_DIM, BLOCK), BF16))
            if g < N_KV_HEADS - 1:
                parts.append(jnp.zeros(((N_KV_HEADS - 1 - g) * HEAD_DIM, BLOCK), BF16))
            cols.append(jnp.concatenate(parts, axis=0) if len(parts) > 1 else parts[0])
        q_exp = jnp.concatenate(cols, axis=1) if len(cols) > 1 else cols[0]
        sink = jnp.full((1, width), sink_ref[h0], F32)
        for r in range(1, HEADS_PER_DOT):
            sink = jnp.where(lane >= r * BLOCK, sink_ref[h0 + r], sink)
        sink = sink * LOG2E
        s = jnp.dot(kbuf[...], q_exp, preferred_element_type=F32)
        if window:
            s = jnp.concatenate([
                s[:n_ctx], s[n_ctx:n_ctx + BLOCK] + bias_prev,
                s[n_ctx + BLOCK:n_ctx + 2 * BLOCK], s[n_ctx + 2 * BLOCK:] + bias_next], axis=0)
        mx = jnp.maximum(jnp.max(s, axis=0, keepdims=True), sink)
        p = jnp.exp2(s - mx).astype(BF16)
        sink_p = jnp.exp2(sink - mx)
        gw = Q_PER_KV * BLOCK
        for g0 in range(0, HEADS_PER_DOT // Q_PER_KV):
            g = h0 // Q_PER_KV + g0
            gl = slice(g0 * gw, (g0 + 1) * gw)
            v_ext = jnp.concatenate([vtbuf[g * HEAD_DIM:(g + 1) * HEAD_DIM, :], ones_rows], axis=0)
            pv = jnp.dot(v_ext, p[:, gl], preferred_element_type=F32)
            o_t = pv[:HEAD_DIM, :] / (pv[HEAD_DIM:HEAD_DIM + 1, :] + sink_p[:, gl])
            for r in range(Q_PER_KV):
                h = g * Q_PER_KV + r
                ot_ref[0, 0, h * HEAD_DIM:(h + 1) * HEAD_DIM, :] = (
                    o_t[:, r * BLOCK:(r + 1) * BLOCK].astype(BF16))


def _attn(q_t, k, v_t, kx, vx_t, sinks):
    B, nq = q_t.shape[:2]
    C = kx.shape[1]
    window = k is not None
    smem = pl.BlockSpec(memory_space=pltpu.SMEM)
    q_spec = pl.BlockSpec((1, 1, Q_DIM, BLOCK), lambda b, n: (b, n, 0, 0))
    kx_spec = pl.BlockSpec((1, C, KV_DIM), lambda b, n: (b, 0, 0))
    vx_spec = pl.BlockSpec((1, C // BLOCK, KV_DIM, BLOCK), lambda b, n: (b, 0, 0, 0))
    if window:
        prev = lambda n: jnp.maximum(n - 1, 0)
        nxt = lambda n: jnp.minimum(n + 1, nq - 1)
        k_spec = lambda f: pl.BlockSpec((1, BLOCK, KV_DIM), lambda b, n: (b, f(n), 0))
        v_spec = lambda f: pl.BlockSpec((1, 1, KV_DIM, BLOCK), lambda b, n: (b, f(n), 0, 0))
        same = lambda n: n
        in_specs = [smem, q_spec, k_spec(prev), k_spec(same), k_spec(nxt),
                    v_spec(prev), v_spec(same), v_spec(nxt), kx_spec, vx_spec]
        args = (sinks, q_t, k, k, k, v_t, v_t, v_t, kx, vx_t)
        n_keys = C + 3 * BLOCK
    else:
        in_specs = [smem, q_spec, kx_spec, vx_spec]
        args = (sinks, q_t, kx, vx_t)
        n_keys = C
    return pl.pallas_call(
        functools.partial(_attn_kernel, window=window, nq=nq, n_ctx=C),
        grid=(B, nq),
        in_specs=in_specs,
        out_specs=q_spec,
        out_shape=jax.ShapeDtypeStruct((B, nq, Q_DIM, BLOCK), BF16),
        scratch_shapes=[pltpu.VMEM((n_keys, KV_DIM), BF16), pltpu.VMEM((KV_DIM, n_keys), BF16)],
        compiler_params=pltpu.CompilerParams(
            dimension_semantics=("arbitrary", "arbitrary"), vmem_limit_bytes=VMEM_LIMIT),
        name="attn_window" if window else "attn_ctx",
    )(*args)


def _mix_ffn_kernel(*refs, ssm, final):
    if ssm:
        (x_ref, mod_ref, yf_ref, yb_ref, z_ref, xs_ref, dsk_ref, ng_ref,
         wo_ref, gf_ref, wg_ref, wu_ref, wd_ref) = refs[:13]
        rest = refs[13:]
    else:
        (x_ref, mod_ref, o_ref_in, wo_ref, gf_ref, wg_ref, wu_ref, wd_ref) = refs[:8]
        rest = refs[8:]
    if final:
        fg_ref, out_ref = rest
    else:
        (out_ref,) = rest
    m = mod_ref[0]
    if ssm:
        mix = None
        for g in range(SSM_GROUPS):
            gs = slice(g * GROUP_DIM, (g + 1) * GROUP_DIM)
            yg = ((yf_ref[0, :, gs].astype(F32) + yb_ref[0, :, gs].astype(F32)
                   + dsk_ref[:, gs] * xs_ref[0, :, gs].astype(F32))
                  * _silu(z_ref[0, :, gs].astype(F32)))
            ms = jnp.mean(yg * yg, axis=-1, keepdims=True)
            part = (yg * lax.rsqrt(ms + EPS) * ng_ref[:, gs]).astype(BF16)
            term = jnp.dot(part, wo_ref[gs, :], preferred_element_type=F32)
            mix = term if mix is None else mix + term
    else:
        o_t = jnp.concatenate([o_ref_in[0, j] for j in range(o_ref_in.shape[1])], axis=1)
        mix = lax.dot_general(o_t, wo_ref[...], _TN, preferred_element_type=F32)
    x1 = x_ref[0] + _mod_chunk(m, 2) * mix
    h2 = _rms_mod(x1, gf_ref[...], _mod_chunk(m, 4), _mod_chunk(m, 3)).astype(BF16)
    gate = jnp.dot(h2, wg_ref[...], preferred_element_type=F32)
    up = jnp.dot(h2, wu_ref[...], preferred_element_type=F32)
    act = (_silu(gate) * up).astype(BF16)
    x2 = x1 + _mod_chunk(m, 5) * jnp.dot(act, wd_ref[...], preferred_element_type=F32)
    if final:
        ms = jnp.mean(x2 * x2, axis=-1, keepdims=True)
        x2 = x2 * lax.rsqrt(ms + EPS) * fg_ref[...]
    out_ref[0] = x2


def _mix_ffn(x, mod, mod_row, mixer_in, w_mix, g_ffn, w_gate, w_up, w_down, layer,
             ssm_norm_g=None, d_skip=None, final_g=None):
    B, L, _ = x.shape
    tm = min(L, 256)
    ssm = ssm_norm_g is not None
    final = final_g is not None
    row = lambda width: pl.BlockSpec((1, tm, width), lambda b, i: (b, i, 0))
    in_specs = [row(D_MODEL), pl.BlockSpec((1, 1, 6 * D_MODEL), lambda b, i: (mod_row(b), 0, 0))]
    args = [x, mod]
    if ssm:
        in_specs += [row(D_INNER)] * 4 + [_const_spec((1, D_INNER))] * 2
        args += list(mixer_in) + [d_skip, ssm_norm_g]
    else:
        in_specs += [pl.BlockSpec((1, tm // BLOCK, Q_DIM, BLOCK), lambda b, i: (b, i, 0, 0))]
        args += [mixer_in]
    in_specs += [_layer_spec(w_mix, 0), _const_spec((1, D_MODEL)), _layer_spec(w_gate, layer),
                 _layer_spec(w_up, layer), _layer_spec(w_down, layer)]
    args += [w_mix, g_ffn, w_gate, w_up, w_down]
    if final:
        in_specs += [_const_spec((1, D_MODEL))]
        args += [final_g]
    return pl.pallas_call(
        functools.partial(_mix_ffn_kernel, ssm=ssm, final=final),
        grid=(B, L // tm),
        in_specs=in_specs,
        out_specs=row(D_MODEL),
        out_shape=jax.ShapeDtypeStruct((B, L, D_MODEL), F32),
        compiler_params=pltpu.CompilerParams(
            dimension_semantics=("arbitrary", "arbitrary"), vmem_limit_bytes=VMEM_LIMIT),
        name="ssm_out_ffn" if ssm else "attn_out_ffn",
    )(*args)


def _inproj_kernel(x_ref, xp_ref, xn_ref, mod_ref, g_ref, w_ref, wdt_ref, dtb_ref, cw_ref, cb_ref,
                   z_ref, xa_ref, dt_ref, *, n_tiles):
    i = pl.program_id(1)
    m = mod_ref[0]
    g, scale, shift = g_ref[...], _mod_chunk(m, 1), _mod_chunk(m, 0)
    tm = x_ref.shape[1]
    h = _rms_mod(x_ref[0], g, scale, shift).astype(BF16)
    x_ext = jnp.concatenate([xp_ref[0], x_ref[0], xn_ref[0]], axis=0)
    h_ext = _rms_mod(x_ext, g, scale, shift).astype(BF16)
    z_ref[0] = jnp.dot(h, w_ref[:, :D_INNER], preferred_element_type=F32).astype(BF16)
    raw = jnp.dot(h, wdt_ref[...], preferred_element_type=F32) + dtb_ref[...]
    dt_ref[0] = jnp.maximum(raw, 0.0) + jnp.log1p(jnp.exp(-jnp.abs(raw)))
    keep_prev = jnp.where(i > 0, 1.0, 0.0)
    keep_next = jnp.where(i < n_tiles - 1, 1.0, 0.0)
    cstep = 512
    for j in range(CONV_DIM // cstep):
        sl = slice(j * cstep, (j + 1) * cstep)
        u = jnp.dot(h_ext, w_ref[:, D_INNER + j * cstep:D_INNER + (j + 1) * cstep],
                    preferred_element_type=F32)
        u = jnp.concatenate([u[0:SUBLANES] * keep_prev, u[SUBLANES:SUBLANES + tm],
                             u[SUBLANES + tm:] * keep_next], axis=0)
        rows = tm + 2 * SUBLANES
        below = pltpu.roll(u, 1, 0)[SUBLANES:SUBLANES + tm]
        above = pltpu.roll(u, rows - 1, 0)[SUBLANES:SUBLANES + tm]
        conv = (below * cw_ref[0:1, sl] + u[SUBLANES:SUBLANES + tm] * cw_ref[1:2, sl]
                + above * cw_ref[2:3, sl] + cb_ref[:, sl])
        xa_ref[0, :, sl] = _silu(conv).astype(BF16)


def _inproj(x, mod, mod_row, *, g, w_in, w_dt, dt_bias, conv_w, conv_b):
    B, L, _ = x.shape
    tm = min(L, 256)
    n_tiles = L // tm
    per = tm // SUBLANES
    row = lambda width: pl.BlockSpec((1, tm, width), lambda b, i: (b, i, 0))
    return pl.pallas_call(
        functools.partial(_inproj_kernel, n_tiles=n_tiles),
        grid=(B, n_tiles),
        in_specs=[row(D_MODEL),
                  pl.BlockSpec((1, SUBLANES, D_MODEL), lambda b, i: (b, jnp.maximum(i * per - 1, 0), 0)),
                  pl.BlockSpec((1, SUBLANES, D_MODEL),
                               lambda b, i: (b, jnp.minimum((i + 1) * per, L // SUBLANES - 1), 0)),
                  pl.BlockSpec((1, 1, 6 * D_MODEL), lambda b, i: (mod_row(b), 0, 0)),
                  _const_spec((1, D_MODEL)), _const_spec(w_in.shape), _const_spec(w_dt.shape),
                  _const_spec((1, 2 * LANES)), _const_spec((8, CONV_DIM)), _const_spec((1, CONV_DIM))],
        out_specs=[row(D_INNER), row(CONV_DIM), row(2 * LANES)],
        out_shape=[jax.ShapeDtypeStruct((B, L, D_INNER), BF16),
                   jax.ShapeDtypeStruct((B, L, CONV_DIM), BF16),
                   jax.ShapeDtypeStruct((B, L, 2 * LANES), F32)],
        compiler_params=pltpu.CompilerParams(
            dimension_semantics=("arbitrary", "arbitrary"), vmem_limit_bytes=VMEM_LIMIT),
        name="ssm_inproj",
    )(x, x, x, mod, g, w_in, w_dt, dt_bias, conv_w, conv_b)


def _split3(v):
    hi = v.astype(BF16)
    r1 = v - hi.astype(F32)
    mid = r1.astype(BF16)
    lo = (r1 - mid.astype(F32)).astype(BF16)
    return hi, mid, lo


def _ssd_kernel(*refs, reverse, nc, has_init, want_y):
    (xa_ref, dt_ref, alog_ref, e64_ref), rest = refs[:4], refs[4:]
    if has_init:
        h0_ref, rest = rest[0], rest[1:]
    if want_y:
        y_ref, rest = rest[0], rest[1:]
    hfin_ref, state, ce, tbuf = rest
    c = pl.program_id(1)
    last = 0 if reverse else CHUNK - 1

    @pl.when(c == 0)
    def _():
        state[...] = h0_ref[0] if has_init else jnp.zeros(state.shape, F32)

    lane = lax.broadcasted_iota(jnp.int32, (CHUNK, LANES), 1)
    row = lax.broadcasted_iota(jnp.int32, (CHUNK, LANES), 0)
    lane_valid = (lane, (lane >= row) if reverse else (lane <= row))
    _prepare_decay(dt_ref, alog_ref, e64_ref, ce, tbuf, lane_valid, last)
    _scan_chunk(xa_ref, y_ref if want_y else None, state, ce, tbuf, lane_valid, last)

    @pl.when(c == nc - 1)
    def _():
        hfin_ref[0] = state[...]


def _prepare_decay(dt_ref, alog_ref, e64_ref, ce, tbuf, lane_valid, last):
    lane, valid = lane_valid
    dt = jnp.where(lane < SSM_HEADS, dt_ref[0], 0.0)
    a = dt * (-LOG2E * jnp.exp(alog_ref[0]))
    tri = jnp.where(valid, 1.0, 0.0).astype(BF16)
    a_hi, a_mid, a_lo = _split3(a)
    cum = (jnp.dot(tri, a_hi, preferred_element_type=F32)
           + jnp.dot(tri, a_mid, preferred_element_type=F32)
           + jnp.dot(tri, a_lo, preferred_element_type=F32))
    c_hi, c_mid, c_lo = _split3(cum)
    pieces = (c_hi.astype(F32) + pltpu.roll(c_mid.astype(F32), SSM_HEADS, 1)
              + pltpu.roll(c_lo.astype(F32), 2 * SSM_HEADS, 1)).astype(BF16)
    ce[...] = jnp.dot(pieces, e64_ref[...], preferred_element_type=F32)
    tb = (cum + pltpu.roll(dt, SSM_HEADS, 1)).T
    cum_t = tb[0:SSM_HEADS, :]
    dt_t = tb[SSM_HEADS:2 * SSM_HEADS, :]
    tbuf[0:SSM_HEADS, :] = cum_t - jnp.log2(dt_t)
    tbuf[SSM_HEADS:2 * SSM_HEADS, :] = jnp.exp2(cum_t[:, last:last + 1] - cum_t) * dt_t


def _scan_chunk(xa_ref, y_ref, state, ce, tbuf, lane_valid, last):
    lane, valid = lane_valid
    want_y = y_ref is not None
    lane_g = lax.broadcasted_iota(jnp.int32, (CHUNK, GROUP_DIM), 1)
    head_mask = [jnp.where((lane_g >= hl * SSM_HEAD_DIM) & (lane_g < (hl + 1) * SSM_HEAD_DIM),
                           1.0, 0.0).astype(BF16) for hl in range(HEADS_PER_GROUP)]
    low_half = lane < SSM_HEAD_DIM
    for g in range(SSM_GROUPS):
        gs = slice(g * GROUP_DIM, (g + 1) * GROUP_DIM)
        b_g = xa_ref[0, :, D_INNER + g * D_STATE:D_INNER + (g + 1) * D_STATE]
        c_g = xa_ref[0, :, D_INNER + BC_DIM + g * D_STATE:D_INNER + BC_DIM + (g + 1) * D_STATE]
        b_t = b_g.astype(F32).T
        x_g = xa_ref[0, :, gs]
        if want_y:
            cb_mat = lax.dot_general(c_g, b_g, _NT, preferred_element_type=F32)
            y_g = (jnp.dot(c_g, state[g].astype(BF16), preferred_element_type=F32)
                   * jnp.exp2(ce[:, gs]))
        st = jnp.zeros((D_STATE, GROUP_DIM), F32)
        for hp in range(HEADS_PER_GROUP // 2):
            pair = g * (HEADS_PER_GROUP // 2) + hp
            lhs_y, lhs_s, rhs = [], [], []
            if want_y:
                ce_pair = ce[:, pair * LANES:(pair + 1) * LANES]
                swapped = pltpu.roll(ce_pair, SSM_HEAD_DIM, 1)
            for hh in range(2):
                hl = 2 * hp + hh
                h = g * HEADS_PER_GROUP + hl
                if want_y:
                    own = low_half if hh == 0 else jnp.logical_not(low_half)
                    cum_i = jnp.where(own, ce_pair, swapped)
                    dec = jnp.exp2(jnp.where(valid, cum_i - tbuf[h:h + 1, :], NEG_INF))
                    lhs_y.append((dec * cb_mat).astype(BF16))
                lhs_s.append((b_t * tbuf[SSM_HEADS + h:SSM_HEADS + h + 1, :]).astype(BF16))
                rhs.append(x_g * head_mask[hl])
            rhs = jnp.concatenate(rhs, axis=0)
            if want_y:
                y_g = y_g + jnp.dot(jnp.concatenate(lhs_y, axis=1), rhs, preferred_element_type=F32)
            st = st + jnp.dot(jnp.concatenate(lhs_s, axis=1), rhs, preferred_element_type=F32)
        if want_y:
            y_ref[0, :, gs] = y_g.astype(BF16)
        state[g] = state[g] * jnp.exp2(ce[last:last + 1, gs]) + st


def _ssd(xact, dt, a_log, e64, h0, direction, want_y):
    B, L, _ = xact.shape
    nc = L // CHUNK
    reverse = direction == 1
    has_init = h0 is not None
    chunk = (lambda c: nc - 1 - c) if reverse else (lambda c: c)
    st_shape = (SSM_GROUPS, D_STATE, GROUP_DIM)
    st_spec = pl.BlockSpec((1,) + st_shape, lambda b, c: (b, 0, 0, 0))
    in_specs = [
        pl.BlockSpec((1, CHUNK, CONV_DIM), lambda b, c: (b, chunk(c), 0)),
        pl.BlockSpec((1, CHUNK, LANES), lambda b, c: (b, chunk(c), direction)),
        pl.BlockSpec((1, 1, LANES), lambda b, c: (direction, 0, 0)),
        _const_spec(e64.shape),
    ]
    args = [xact, dt, a_log, e64]
    if has_init:
        in_specs.append(st_spec)
        args.append(h0)
    out_specs, out_shape = [], []
    if want_y:
        out_specs.append(pl.BlockSpec((1, CHUNK, D_INNER), lambda b, c: (b, chunk(c), 0)))
        out_shape.append(jax.ShapeDtypeStruct((B, L, D_INNER), BF16))
    out_specs.append(st_spec)
    out_shape.append(jax.ShapeDtypeStruct((B,) + st_shape, F32))
    scratch = [pltpu.VMEM(st_shape, F32), pltpu.VMEM((CHUNK, D_INNER), F32),
               pltpu.VMEM((2 * SSM_HEADS, CHUNK), F32)]
    return pl.pallas_call(
        functools.partial(_ssd_kernel, reverse=reverse, nc=nc, has_init=has_init, want_y=want_y),
        grid=(B, nc),
        in_specs=in_specs,
        out_specs=out_specs,
        out_shape=out_shape,
        scratch_shapes=scratch,
        compiler_params=pltpu.CompilerParams(
            dimension_semantics=("arbitrary", "arbitrary"), vmem_limit_bytes=VMEM_LIMIT),
        name="ssd_bwd" if reverse else "ssd_fwd",
    )(*args)


def _expand_matrix(width):
    k = np.arange(LANES)[:, None]
    col_head = (np.arange(SSM_HEADS * width) // width)[None, :]
    return jnp.asarray((k % SSM_HEADS == col_head) & (k < 3 * SSM_HEADS), dtype=BF16)


def kernel(x, c, ctx, c_ctx, ada_w, ada_b, norm_mix_g, norm_ffn_g, attn_w_qkv, attn_w_o, attn_sinks,
           ssm_w_in, ssm_conv_w, ssm_conv_b, ssm_dt_bias, ssm_A_log, ssm_D, ssm_norm_g, ssm_w_out,
           ffn_w_gate, ffn_w_up, ffn_w_down, final_norm_g):
    B, L, _ = x.shape
    assert x.shape == (B, L, D_MODEL) and B <= CTX_ROW and L % 1024 == 0
    bf = lambda w: w.astype(BF16)
    row2 = lambda v: v.reshape(1, -1)

    cc = jnp.zeros((MOD_ROWS, D_MODEL), F32).at[:B].set(c).at[CTX_ROW].set(c_ctx)
    mod = _adaln(cc, ada_w, ada_b)
    mods = [mod[i].reshape(MOD_ROWS, 1, 6 * D_MODEL) for i in range(DEPTH)]
    x_row = lambda b: b
    ctx_row = lambda b: CTX_ROW

    w_qkv = attn_w_qkv[0]
    w_q_t = bf(w_qkv[:, :Q_DIM].T)
    w_k = bf(w_qkv[:, Q_DIM:Q_DIM + KV_DIM])
    w_v_t = bf(w_qkv[:, Q_DIM + KV_DIM:].T)
    g_mix = row2(norm_mix_g[0])
    q, k, v = _qkv(x, mods[0], g_mix, w_q_t, w_k, w_v_t, _rope_tables(L), x_row)
    qc, kc, vc = _qkv(ctx, mods[0], g_mix, w_q_t, w_k, w_v_t, None, ctx_row)
    sinks = attn_sinks[0].astype(F32)
    o = _attn(q, k, v, kc, vc, sinks)
    oc = _attn(qc, None, None, kc, vc, sinks)
    w_gate, w_up, w_down = bf(ffn_w_gate), bf(ffn_w_up), bf(ffn_w_down)
    ffn0 = (bf(attn_w_o), row2(norm_ffn_g[0]), w_gate, w_up, w_down, 0)
    x = _mix_ffn(x, mods[0], x_row, o, *ffn0)
    ctx = _mix_ffn(ctx, mods[0], ctx_row, oc, *ffn0)

    w_in = ssm_w_in[0]
    w_dt_raw = w_in[:, D_INNER + CONV_DIM:]
    w_dt = jnp.zeros((D_MODEL, 2 * LANES), F32)
    w_dt = w_dt.at[:, :SSM_HEADS].set(w_dt_raw[:, :SSM_HEADS])
    w_dt = bf(w_dt.at[:, LANES:LANES + SSM_HEADS].set(w_dt_raw[:, SSM_HEADS:]))
    dt_bias = jnp.zeros((2, LANES), F32).at[:, :SSM_HEADS].set(ssm_dt_bias[0]).reshape(1, 2 * LANES)
    a_log = jnp.zeros((2, 1, LANES), F32).at[:, 0, :SSM_HEADS].set(ssm_A_log[0])
    conv_w = jnp.zeros((8, CONV_DIM), F32).at[:3].set(ssm_conv_w[0])
    conv_b = row2(ssm_conv_b[0])
    d_skip = row2(jnp.repeat(ssm_D[0], SSM_HEAD_DIM))
    e64 = _expand_matrix(SSM_HEAD_DIM)
    g_mix1 = row2(norm_mix_g[1])

    inproj = functools.partial(_inproj, g=g_mix1, w_in=bf(w_in), w_dt=w_dt, dt_bias=dt_bias,
                               conv_w=conv_w, conv_b=conv_b)
    z, xact, dt = inproj(x, mods[1], x_row)
    _, xact_c, dt_c = inproj(ctx, mods[1], ctx_row)
    ssd = functools.partial(_ssd, a_log=a_log, e64=e64)
    (h_f,) = ssd(xact_c, dt_c, h0=None, direction=0, want_y=False)
    (h_b,) = ssd(xact_c, dt_c, h0=None, direction=1, want_y=False)
    y_f, _ = ssd(xact, dt, h0=h_f, direction=0, want_y=True)
    y_b, _ = ssd(xact, dt, h0=h_b, direction=1, want_y=True)
    return _mix_ffn(x, mods[1], x_row, (y_f, y_b, z, xact), bf(ssm_w_out), row2(norm_ffn_g[1]),
                    w_gate, w_up, w_down, 1,
                    ssm_norm_g=row2(ssm_norm_g[0]), d_skip=d_skip, final_g=row2(final_norm_g))
```

```python
import functools
import math

import numpy as np

import jax
import jax.numpy as jnp
from jax import lax
from jax.experimental import pallas as pl
from jax.experimental.pallas import tpu as pltpu

F32 = jnp.float32
BF16 = jnp.bfloat16

D_MODEL = 1024
DEPTH = 2
GRID_W = 64
EPS = 1e-6

HEAD_DIM = 64
N_HEADS = D_MODEL // HEAD_DIM
N_KV_HEADS = N_HEADS // 4
Q_PER_KV = N_HEADS // N_KV_HEADS
Q_DIM = N_HEADS * HEAD_DIM
KV_DIM = N_KV_HEADS * HEAD_DIM
QKV_DIM = Q_DIM + 2 * KV_DIM
WINDOW = 128
BLOCK = 128
ROPE_FREQS = HEAD_DIM // 4
ROPE_BASE = 10000.0

D_INNER = 2 * D_MODEL
SSM_HEAD_DIM = 64
SSM_HEADS = D_INNER // SSM_HEAD_DIM
SSM_GROUPS = 8
HEADS_PER_GROUP = SSM_HEADS // SSM_GROUPS
GROUP_DIM = D_INNER // SSM_GROUPS
D_STATE = 128
CHUNK = 128
BC_DIM = SSM_GROUPS * D_STATE
CONV_DIM = D_INNER + 2 * BC_DIM
D_FF = ((8 * D_MODEL // 3 + 255) // 256) * 256

LANES = 128
SUBLANES = 8
MOD_ROWS = 8
CTX_ROW = 4
VMEM_LIMIT = 48 * 1024 * 1024
VMEM_LIMIT_FFN = 56 * 1024 * 1024
NEG_INF = float("-inf")
LOG2E = math.log2(math.e)
HEADS_PER_DOT = N_HEADS


def _const_spec(shape):
    nd = len(shape)
    return pl.BlockSpec(shape, lambda *_: (0,) * nd, pipeline_mode=pl.Buffered(1))


def _layer_spec(stacked, layer):
    return pl.BlockSpec((None,) + stacked.shape[1:], lambda *_: (layer, 0, 0),
                        pipeline_mode=pl.Buffered(1))


def _silu(v):
    return v * jax.nn.sigmoid(v)


def _rms_mod(x, g, scale, shift):
    ms = jnp.mean(x * x, axis=-1, keepdims=True)
    return (x * lax.rsqrt(ms + EPS) * g) * (1.0 + scale) + shift


def _mod_chunk(m, i):
    return m[:, i * D_MODEL:(i + 1) * D_MODEL]


def _adaln_kernel(c_ref, w_ref, b_ref, o_ref):
    s = _silu(c_ref[...])
    o_ref[0] = jnp.dot(s, w_ref[0], preferred_element_type=F32) + b_ref[0]


def _adaln(cc, ada_w, ada_b):
    tn = 1536
    n = 6 * D_MODEL
    return pl.pallas_call(
        _adaln_kernel,
        grid=(DEPTH, n // tn),
        in_specs=[
            pl.BlockSpec((MOD_ROWS, D_MODEL), lambda i, j: (0, 0)),
            pl.BlockSpec((1, D_MODEL, tn), lambda i, j: (i, 0, j)),
            pl.BlockSpec((1, 1, tn), lambda i, j: (i, 0, j)),
        ],
        out_specs=pl.BlockSpec((1, MOD_ROWS, tn), lambda i, j: (i, 0, j)),
        out_shape=jax.ShapeDtypeStruct((DEPTH, MOD_ROWS, n), F32),
        compiler_params=pltpu.CompilerParams(
            dimension_semantics=("arbitrary", "arbitrary"), vmem_limit_bytes=VMEM_LIMIT),
        name="adaln",
    )(cc, ada_w, ada_b.reshape(DEPTH, 1, n))


_NT = (((1,), (1,)), ((), ()))
_TN = (((0,), (0,)), ((), ()))


def _qkv_kernel(*refs, rope):
    if rope:
        (x_ref, mod_ref, g_ref, wq_ref, wk_ref, wv_ref, cos_t_ref, sin_t_ref,
         cos_ref, sa_ref, sb_ref, qt_ref, k_ref, vt_ref) = refs
    else:
        x_ref, mod_ref, g_ref, wq_ref, wk_ref, wv_ref, qt_ref, k_ref, vt_ref = refs
    m = mod_ref[0]
    h = _rms_mod(x_ref[0], g_ref[...], _mod_chunk(m, 1), _mod_chunk(m, 0)).astype(BF16)
    q_t = lax.dot_general(wq_ref[...], h, _NT, preferred_element_type=F32)
    k = jnp.dot(h, wk_ref[...], preferred_element_type=F32)
    v_t = lax.dot_general(wv_ref[...], h, _NT, preferred_element_type=F32).astype(BF16)
    n_blk = x_ref.shape[1] // BLOCK

    def put(ref, rows, val):
        for j in range(n_blk):
            ref[0, j, rows, :] = val[:, j * BLOCK:(j + 1) * BLOCK]

    put(vt_ref, slice(0, KV_DIM), v_t)
    scale = HEAD_DIM ** -0.5 * LOG2E
    if rope:
        f = ROPE_FREQS
        for head in range(N_HEADS):
            for axis in range(2):
                r0 = head * HEAD_DIM + axis * 2 * f
                cs = cos_t_ref[axis * f:(axis + 1) * f, :]
                sn = sin_t_ref[axis * f:(axis + 1) * f, :]
                x1 = q_t[r0:r0 + f, :]
                x2 = q_t[r0 + f:r0 + 2 * f, :]
                put(qt_ref, slice(r0, r0 + f), ((x1 * cs - x2 * sn) * scale).astype(BF16))
                put(qt_ref, slice(r0 + f, r0 + 2 * f), ((x2 * cs + x1 * sn) * scale).astype(BF16))
        for blk in range(KV_DIM // LANES):
            t = k[:, blk * LANES:(blk + 1) * LANES]
            t = (t * cos_ref[...] + pltpu.roll(t, LANES - f, 1) * sa_ref[...]
                 + pltpu.roll(t, f, 1) * sb_ref[...])
            k_ref[0, :, blk * LANES:(blk + 1) * LANES] = t.astype(BF16)
    else:
        put(qt_ref, slice(0, Q_DIM), (q_t * scale).astype(BF16))
        k_ref[0] = k.astype(BF16)


def _rope_tables(L):
    f32 = np.float32
    rows = L // GRID_W
    row = np.repeat(np.arange(rows, dtype=f32), GRID_W)
    col = np.tile(np.arange(GRID_W, dtype=f32), rows)
    inv = (f32(ROPE_BASE) ** (-np.arange(ROPE_FREQS, dtype=f32) / f32(ROPE_FREQS))).astype(f32)
    ang_r = (row[:, None] * inv).astype(f32)
    ang_c = (col[:, None] * inv).astype(f32)
    zero = np.zeros_like(ang_r)

    def lanes(r_first, r_second, c_first, c_second):
        head = np.concatenate([r_first, r_second, c_first, c_second], axis=-1)
        return np.tile(head, (1, LANES // HEAD_DIM)).astype(f32)

    cr, sr, cc, sc = np.cos(ang_r), np.sin(ang_r), np.cos(ang_c), np.sin(ang_c)
    cos_t = np.ascontiguousarray(np.concatenate([cr, cc], axis=1).T).astype(f32)
    sin_t = np.ascontiguousarray(np.concatenate([sr, sc], axis=1).T).astype(f32)
    return (cos_t, sin_t, lanes(cr, cr, cc, cc), lanes(-sr, zero, -sc, zero),
            lanes(zero, sr, zero, sc))


def _qkv(x, mod, g, w_q_t, w_k, w_v_t, tables, mod_row):
    B, L, _ = x.shape
    tm = min(L, 1024)
    rope = tables is not None
    in_specs = [
        pl.BlockSpec((1, tm, D_MODEL), lambda b, i: (b, i, 0)),
        pl.BlockSpec((1, 1, 6 * D_MODEL), lambda b, i: (mod_row(b), 0, 0)),
        _const_spec((1, D_MODEL)),
        _const_spec(w_q_t.shape), _const_spec(w_k.shape), _const_spec(w_v_t.shape),
    ]
    args = [x, mod, g, w_q_t, w_k, w_v_t]
    if rope:
        in_specs += [pl.BlockSpec((2 * ROPE_FREQS, tm), lambda b, i: (0, i))] * 2
        in_specs += [pl.BlockSpec((tm, LANES), lambda b, i: (i, 0))] * 3
        args += list(tables)
    return pl.pallas_call(
        functools.partial(_qkv_kernel, rope=rope),
        grid=(B, L // tm),
        in_specs=in_specs,
        out_specs=[
            pl.BlockSpec((1, tm // BLOCK, Q_DIM, BLOCK), lambda b, i: (b, i, 0, 0)),
            pl.BlockSpec((1, tm, KV_DIM), lambda b, i: (b, i, 0)),
            pl.BlockSpec((1, tm // BLOCK, KV_DIM, BLOCK), lambda b, i: (b, i, 0, 0)),
        ],
        out_shape=[
            jax.ShapeDtypeStruct((B, L // BLOCK, Q_DIM, BLOCK), BF16),
            jax.ShapeDtypeStruct((B, L, KV_DIM), BF16),
            jax.ShapeDtypeStruct((B, L // BLOCK, KV_DIM, BLOCK), BF16),
        ],
        compiler_params=pltpu.CompilerParams(
            dimension_semantics=("arbitrary", "arbitrary"), vmem_limit_bytes=VMEM_LIMIT),
        name="qkv_rope" if rope else "qkv_ctx",
    )(*args)


def _attn_kernel(*refs, window, nq, n_ctx):
    if window:
        (sink_ref, qt_ref, kp_ref, kc_ref, kn_ref, vp_ref, vc_ref, vn_ref,
         kx_ref, vx_ref, ot_ref, kbuf, vtbuf) = refs
    else:
        sink_ref, qt_ref, kx_ref, vx_ref, ot_ref, kbuf, vtbuf = refs
    width = HEADS_PER_DOT * BLOCK
    kbuf[0:n_ctx, :] = kx_ref[0]
    for j in range(n_ctx // BLOCK):
        vtbuf[:, j * BLOCK:(j + 1) * BLOCK] = vx_ref[0, j]
    if window:
        for j, (kr, vr) in enumerate(((kp_ref, vp_ref), (kc_ref, vc_ref), (kn_ref, vn_ref))):
            kbuf[n_ctx + j * BLOCK:n_ctx + (j + 1) * BLOCK, :] = kr[0]
            vtbuf[:, n_ctx + j * BLOCK:n_ctx + (j + 1) * BLOCK] = vr[0, 0]
        n = pl.program_id(1)
        kj = lax.broadcasted_iota(jnp.int32, (BLOCK, width), 0)
        qi = lax.broadcasted_iota(jnp.int32, (BLOCK, width), 1) % BLOCK
        bias_prev = jnp.where((kj >= qi) & (n > 0), 0.0, NEG_INF).astype(F32)
        bias_next = jnp.where((kj <= qi) & (n < nq - 1), 0.0, NEG_INF).astype(F32)
    lane = lax.broadcasted_iota(jnp.int32, (1, width), 1)
    ones_rows = jnp.ones((16, kbuf.shape[0]), BF16)
    for h0 in range(0, N_HEADS, HEADS_PER_DOT):
        cols = []
        for h in range(h0, h0 + HEADS_PER_DOT):
            g = h // Q_PER_KV
            parts = [qt_ref[0, 0, h * HEAD_DIM:(h + 1) * HEAD_DIM, :]]
            if g > 0:
                parts.insert(0, jnp.zeros((g * HEAD_DIM, BLOCK), BF16))
            if g < N_KV_HEADS - 1:
                parts.append(jnp.zeros(((N_KV_HEADS - 1 - g) * HEAD_DIM, BLOCK), BF16))
            cols.append(jnp.concatenate(parts, axis=0) if len(parts) > 1 else parts[0])
        q_exp = jnp.concatenate(cols, axis=1) if len(cols) > 1 else cols[0]
        sink = jnp.full((1, width), sink_ref[h0], F32)
        for r in range(1, HEADS_PER_DOT):
            sink = jnp.where(lane >= r * BLOCK, sink_ref[h0 + r], sink)
        sink = sink * LOG2E
        s = jnp.dot(kbuf[...], q_exp, preferred_element_type=F32)
        if window:
            s = jnp.concatenate([
                s[:n_ctx], s[n_ctx:n_ctx + BLOCK] + bias_prev,
                s[n_ctx + BLOCK:n_ctx + 2 * BLOCK], s[n_ctx + 2 * BLOCK:] + bias_next], axis=0)
        mx = jnp.maximum(jnp.max(s, axis=0, keepdims=True), sink)
        p = jnp.exp2(s - mx).astype(BF16)
        sink_p = jnp.exp2(sink - mx)
        gw = Q_PER_KV * BLOCK
        for g0 in range(0, HEADS_PER_DOT // Q_PER_KV):
            g = h0 // Q_PER_KV + g0
            gl = slice(g0 * gw, (g0 + 1) * gw)
            v_ext = jnp.concatenate([vtbuf[g * HEAD_DIM:(g + 1) * HEAD_DIM, :], ones_rows], axis=0)
            pv = jnp.dot(v_ext, p[:, gl], preferred_element_type=F32)
            o_t = pv[:HEAD_DIM, :] / (pv[HEAD_DIM:HEAD_DIM + 1, :] + sink_p[:, gl])
            for r in range(Q_PER_KV):
                h = g * Q_PER_KV + r
                ot_ref[0, 0, h * HEAD_DIM:(h + 1) * HEAD_DIM, :] = (
                    o_t[:, r * BLOCK:(r + 1) * BLOCK].astype(BF16))


def _attn(q_t, k, v_t, kx, vx_t, sinks):
    B, nq = q_t.shape[:2]
    C = kx.shape[1]
    window = k is not None
    smem = pl.BlockSpec(memory_space=pltpu.SMEM)
    q_spec = pl.BlockSpec((1, 1, Q_DIM, BLOCK), lambda b, n: (b, n, 0, 0))
    kx_spec = pl.BlockSpec((1, C, KV_DIM), lambda b, n: (b, 0, 0))
    vx_spec = pl.BlockSpec((1, C // BLOCK, KV_DIM, BLOCK), lambda b, n: (b, 0, 0, 0))
    if window:
        prev = lambda n: jnp.maximum(n - 1, 0)
        nxt = lambda n: jnp.minimum(n + 1, nq - 1)
        k_spec = lambda f: pl.BlockSpec((1, BLOCK, KV_DIM), lambda b, n: (b, f(n), 0))
        v_spec = lambda f: pl.BlockSpec((1, 1, KV_DIM, BLOCK), lambda b, n: (b, f(n), 0, 0))
        same = lambda n: n
        in_specs = [smem, q_spec, k_spec(prev), k_spec(same), k_spec(nxt),
                    v_spec(prev), v_spec(same), v_spec(nxt), kx_spec, vx_spec]
        args = (sinks, q_t, k, k, k, v_t, v_t, v_t, kx, vx_t)
        n_keys = C + 3 * BLOCK
    else:
        in_specs = [smem, q_spec, kx_spec, vx_spec]
        args = (sinks, q_t, kx, vx_t)
        n_keys = C
    return pl.pallas_call(
        functools.partial(_attn_kernel, window=window, nq=nq, n_ctx=C),
        grid=(B, nq),
        in_specs=in_specs,
        out_specs=q_spec,
        out_shape=jax.ShapeDtypeStruct((B, nq, Q_DIM, BLOCK), BF16),
        scratch_shapes=[pltpu.VMEM((n_keys, KV_DIM), BF16), pltpu.VMEM((KV_DIM, n_keys), BF16)],
        compiler_params=pltpu.CompilerParams(
            dimension_semantics=("arbitrary", "arbitrary"), vmem_limit_bytes=VMEM_LIMIT),
        name="attn_window" if window else "attn_ctx",
    )(*args)


def _mix_ffn_kernel(*refs, ssm, final):
    if ssm:
        (x_ref, mod_ref, yf_ref, yb_ref, z_ref, xs_ref, dsk_ref, ng_ref,
         wo_ref, gf_ref, wg_ref, wu_ref, wd_ref) = refs[:13]
        rest = refs[13:]
    else:
        (x_ref, mod_ref, o_ref_in, wo_ref, gf_ref, wg_ref, wu_ref, wd_ref) = refs[:8]
        rest = refs[8:]
    if final:
        fg_ref, out_ref = rest
    else:
        (out_ref,) = rest
    m = mod_ref[0]
    if ssm:
        mix = None
        for g in range(SSM_GROUPS):
            gs = slice(g * GROUP_DIM, (g + 1) * GROUP_DIM)
            yg = ((yf_ref[0, :, gs].astype(F32) + yb_ref[0, :, gs].astype(F32)
                   + dsk_ref[:, gs] * xs_ref[0, :, gs].astype(F32))
                  * _silu(z_ref[0, :, gs].astype(F32)))
            ms = jnp.mean(yg * yg, axis=-1, keepdims=True)
            part = (yg * lax.rsqrt(ms + EPS) * ng_ref[:, gs]).astype(BF16)
            term = jnp.dot(part, wo_ref[gs, :], preferred_element_type=F32)
            mix = term if mix is None else mix + term
    else:
        o_t = jnp.concatenate([o_ref_in[0, j] for j in range(o_ref_in.shape[1])], axis=1)
        mix = lax.dot_general(o_t, wo_ref[...], _TN, preferred_element_type=F32)
    x1 = x_ref[0] + _mod_chunk(m, 2) * mix
    h2 = _rms_mod(x1, gf_ref[...], _mod_chunk(m, 4), _mod_chunk(m, 3)).astype(BF16)
    gate = jnp.dot(h2, wg_ref[...], preferred_element_type=F32)
    up = jnp.dot(h2, wu_ref[...], preferred_element_type=F32)
    act = (_silu(gate) * up).astype(BF16)
    x2 = x1 + _mod_chunk(m, 5) * jnp.dot(act, wd_ref[...], preferred_element_type=F32)
    if final:
        ms = jnp.mean(x2 * x2, axis=-1, keepdims=True)
        x2 = x2 * lax.rsqrt(ms + EPS) * fg_ref[...]
    out_ref[0] = x2


def _mix_ffn(x, mod, mod_row, mixer_in, w_mix, g_ffn, w_gate, w_up, w_down, layer,
             ssm_norm_g=None, d_skip=None, final_g=None):
    B, L, _ = x.shape
    tm = min(L, 512)
    ssm = ssm_norm_g is not None
    final = final_g is not None
    row = lambda width: pl.BlockSpec((1, tm, width), lambda b, i: (b, i, 0))
    in_specs = [row(D_MODEL), pl.BlockSpec((1, 1, 6 * D_MODEL), lambda b, i: (mod_row(b), 0, 0))]
    args = [x, mod]
    if ssm:
        in_specs += [row(D_INNER)] * 4 + [_const_spec((1, D_INNER))] * 2
        args += list(mixer_in) + [d_skip, ssm_norm_g]
    else:
        in_specs += [pl.BlockSpec((1, tm // BLOCK, Q_DIM, BLOCK), lambda b, i: (b, i, 0, 0))]
        args += [mixer_in]
    in_specs += [_layer_spec(w_mix, 0), _const_spec((1, D_MODEL)), _layer_spec(w_gate, layer),
                 _layer_spec(w_up, layer), _layer_spec(w_down, layer)]
    args += [w_mix, g_ffn, w_gate, w_up, w_down]
    if final:
        in_specs += [_const_spec((1, D_MODEL))]
        args += [final_g]
    return pl.pallas_call(
        functools.partial(_mix_ffn_kernel, ssm=ssm, final=final),
        grid=(B, L // tm),
        in_specs=in_specs,
        out_specs=row(D_MODEL),
        out_shape=jax.ShapeDtypeStruct((B, L, D_MODEL), F32),
        compiler_params=pltpu.CompilerParams(
            dimension_semantics=("arbitrary", "arbitrary"), vmem_limit_bytes=VMEM_LIMIT_FFN),
        name="ssm_out_ffn" if ssm else "attn_out_ffn",
    )(*args)


def _inproj_kernel(x_ref, xp_ref, xn_ref, mod_ref, g_ref, w_ref, wdt_ref, dtb_ref, cw_ref, cb_ref,
                   z_ref, xa_ref, dt_ref, *, n_tiles):
    i = pl.program_id(1)
    m = mod_ref[0]
    g, scale, shift = g_ref[...], _mod_chunk(m, 1), _mod_chunk(m, 0)
    tm = x_ref.shape[1]
    h = _rms_mod(x_ref[0], g, scale, shift).astype(BF16)
    x_ext = jnp.concatenate([xp_ref[0], x_ref[0], xn_ref[0]], axis=0)
    h_ext = _rms_mod(x_ext, g, scale, shift).astype(BF16)
    z_ref[0] = jnp.dot(h, w_ref[:, :D_INNER], preferred_element_type=F32).astype(BF16)
    raw = jnp.dot(h, wdt_ref[...], preferred_element_type=F32) + dtb_ref[...]
    dt_ref[0] = jnp.maximum(raw, 0.0) + jnp.log1p(jnp.exp(-jnp.abs(raw)))
    keep_prev = jnp.where(i > 0, 1.0, 0.0)
    keep_next = jnp.where(i < n_tiles - 1, 1.0, 0.0)
    cstep = 512
    for j in range(CONV_DIM // cstep):
        sl = slice(j * cstep, (j + 1) * cstep)
        u = jnp.dot(h_ext, w_ref[:, D_INNER + j * cstep:D_INNER + (j + 1) * cstep],
                    preferred_element_type=F32)
        u = jnp.concatenate([u[0:SUBLANES] * keep_prev, u[SUBLANES:SUBLANES + tm],
                             u[SUBLANES + tm:] * keep_next], axis=0)
        rows = tm + 2 * SUBLANES
        below = pltpu.roll(u, 1, 0)[SUBLANES:SUBLANES + tm]
        above = pltpu.roll(u, rows - 1, 0)[SUBLANES:SUBLANES + tm]
        conv = (below * cw_ref[0:1, sl] + u[SUBLANES:SUBLANES + tm] * cw_ref[1:2, sl]
                + above * cw_ref[2:3, sl] + cb_ref[:, sl])
        xa_ref[0, :, sl] = _silu(conv).astype(BF16)


def _inproj(x, mod, mod_row, *, g, w_in, w_dt, dt_bias, conv_w, conv_b):
    B, L, _ = x.shape
    tm = min(L, 256)
    n_tiles = L // tm
    per = tm // SUBLANES
    row = lambda width: pl.BlockSpec((1, tm, width), lambda b, i: (b, i, 0))
    return pl.pallas_call(
        functools.partial(_inproj_kernel, n_tiles=n_tiles),
        grid=(B, n_tiles),
        in_specs=[row(D_MODEL),
                  pl.BlockSpec((1, SUBLANES, D_MODEL), lambda b, i: (b, jnp.maximum(i * per - 1, 0), 0)),
                  pl.BlockSpec((1, SUBLANES, D_MODEL),
                               lambda b, i: (b, jnp.minimum((i + 1) * per, L // SUBLANES - 1), 0)),
                  pl.BlockSpec((1, 1, 6 * D_MODEL), lambda b, i: (mod_row(b), 0, 0)),
                  _const_spec((1, D_MODEL)), _const_spec(w_in.shape), _const_spec(w_dt.shape),
                  _const_spec((1, 2 * LANES)), _const_spec((8, CONV_DIM)), _const_spec((1, CONV_DIM))],
        out_specs=[row(D_INNER), row(CONV_DIM), row(2 * LANES)],
        out_shape=[jax.ShapeDtypeStruct((B, L, D_INNER), BF16),
                   jax.ShapeDtypeStruct((B, L, CONV_DIM), BF16),
                   jax.ShapeDtypeStruct((B, L, 2 * LANES), F32)],
        compiler_params=pltpu.CompilerParams(
            dimension_semantics=("arbitrary", "arbitrary"), vmem_limit_bytes=VMEM_LIMIT),
        name="ssm_inproj",
    )(x, x, x, mod, g, w_in, w_dt, dt_bias, conv_w, conv_b)


def _split3(v):
    hi = v.astype(BF16)
    r1 = v - hi.astype(F32)
    mid = r1.astype(BF16)
    lo = (r1 - mid.astype(F32)).astype(BF16)
    return hi, mid, lo


def _ssd_kernel(*refs, reverse, nc, has_init, want_y):
    (xa_ref, dt_ref, alog_ref, e64_ref), rest = refs[:4], refs[4:]
    if has_init:
        h0_ref, rest = rest[0], rest[1:]
    if want_y:
        y_ref, rest = rest[0], rest[1:]
    hfin_ref, state, ce, tbuf = rest
    c = pl.program_id(1)
    last = 0 if reverse else CHUNK - 1

    @pl.when(c == 0)
    def _():
        state[...] = h0_ref[0] if has_init else jnp.zeros(state.shape, F32)

    lane = lax.broadcasted_iota(jnp.int32, (CHUNK, LANES), 1)
    row = lax.broadcasted_iota(jnp.int32, (CHUNK, LANES), 0)
    lane_valid = (lane, (lane >= row) if reverse else (lane <= row))
    _prepare_decay(dt_ref, alog_ref, e64_ref, ce, tbuf, lane_valid, last)
    _scan_chunk(xa_ref, y_ref if want_y else None, state, ce, tbuf, lane_valid, last)

    @pl.when(c == nc - 1)
    def _():
        hfin_ref[0] = state[...]


def _prepare_decay(dt_ref, alog_ref, e64_ref, ce, tbuf, lane_valid, last):
    lane, valid = lane_valid
    dt = jnp.where(lane < SSM_HEADS, dt_ref[0], 0.0)
    a = dt * (-LOG2E * jnp.exp(alog_ref[0]))
    tri = jnp.where(valid, 1.0, 0.0).astype(BF16)
    a_hi, a_mid, a_lo = _split3(a)
    cum = (jnp.dot(tri, a_hi, preferred_element_type=F32)
           + jnp.dot(tri, a_mid, preferred_element_type=F32)
           + jnp.dot(tri, a_lo, preferred_element_type=F32))
    c_hi, c_mid, c_lo = _split3(cum)
    pieces = (c_hi.astype(F32) + pltpu.roll(c_mid.astype(F32), SSM_HEADS, 1)
              + pltpu.roll(c_lo.astype(F32), 2 * SSM_HEADS, 1)).astype(BF16)
    ce[...] = jnp.dot(pieces, e64_ref[...], preferred_element_type=F32)
    tb = (cum + pltpu.roll(dt, SSM_HEADS, 1)).T
    cum_t = tb[0:SSM_HEADS, :]
    dt_t = tb[SSM_HEADS:2 * SSM_HEADS, :]
    tbuf[0:SSM_HEADS, :] = cum_t - jnp.log2(dt_t)
    tbuf[SSM_HEADS:2 * SSM_HEADS, :] = jnp.exp2(cum_t[:, last:last + 1] - cum_t) * dt_t


def _scan_chunk(xa_ref, y_ref, state, ce, tbuf, lane_valid, last):
    lane, valid = lane_valid
    want_y = y_ref is not None
    lane_g = lax.broadcasted_iota(jnp.int32, (CHUNK, GROUP_DIM), 1)
    head_mask = [jnp.where((lane_g >= hl * SSM_HEAD_DIM) & (lane_g < (hl + 1) * SSM_HEAD_DIM),
                           1.0, 0.0).astype(BF16) for hl in range(HEADS_PER_GROUP)]
    low_half = lane < SSM_HEAD_DIM
    for g in range(SSM_GROUPS):
        gs = slice(g * GROUP_DIM, (g + 1) * GROUP_DIM)
        b_g = xa_ref[0, :, D_INNER + g * D_STATE:D_INNER + (g + 1) * D_STATE]
        c_g = xa_ref[0, :, D_INNER + BC_DIM + g * D_STATE:D_INNER + BC_DIM + (g + 1) * D_STATE]
        b_t = b_g.astype(F32).T
        x_g = xa_ref[0, :, gs]
        if want_y:
            cb_mat = lax.dot_general(c_g, b_g, _NT, preferred_element_type=F32)
            y_g = (jnp.dot(c_g, state[g].astype(BF16), preferred_element_type=F32)
                   * jnp.exp2(ce[:, gs]))
        st = jnp.zeros((D_STATE, GROUP_DIM), F32)
        for hp in range(HEADS_PER_GROUP // 2):
            pair = g * (HEADS_PER_GROUP // 2) + hp
            lhs_y, lhs_s, rhs = [], [], []
            if want_y:
                ce_pair = ce[:, pair * LANES:(pair + 1) * LANES]
                swapped = pltpu.roll(ce_pair, SSM_HEAD_DIM, 1)
            for hh in range(2):
                hl = 2 * hp + hh
                h = g * HEADS_PER_GROUP + hl
                if want_y:
                    own = low_half if hh == 0 else jnp.logical_not(low_half)
                    cum_i = jnp.where(own, ce_pair, swapped)
                    dec = jnp.exp2(jnp.where(valid, cum_i - tbuf[h:h + 1, :], NEG_INF))
                    lhs_y.append((dec * cb_mat).astype(BF16))
                lhs_s.append((b_t * tbuf[SSM_HEADS + h:SSM_HEADS + h + 1, :]).astype(BF16))
                rhs.append(x_g * head_mask[hl])
            rhs = jnp.concatenate(rhs, axis=0)
            if want_y:
                y_g = y_g + jnp.dot(jnp.concatenate(lhs_y, axis=1), rhs, preferred_element_type=F32)
            st = st + jnp.dot(jnp.concatenate(lhs_s, axis=1), rhs, preferred_element_type=F32)
        if want_y:
            y_ref[0, :, gs] = y_g.astype(BF16)
        state[g] = state[g] * jnp.exp2(ce[last:last + 1, gs]) + st


def _ssd(xact, dt, a_log, e64, h0, direction, want_y):
    B, L, _ = xact.shape
    nc = L // CHUNK
    reverse = direction == 1
    has_init = h0 is not None
    chunk = (lambda c: nc - 1 - c) if reverse else (lambda c: c)
    st_shape = (SSM_GROUPS, D_STATE, GROUP_DIM)
    st_spec = pl.BlockSpec((1,) + st_shape, lambda b, c: (b, 0, 0, 0))
    in_specs = [
        pl.BlockSpec((1, CHUNK, CONV_DIM), lambda b, c: (b, chunk(c), 0)),
        pl.BlockSpec((1, CHUNK, LANES), lambda b, c: (b, chunk(c), direction)),
        pl.BlockSpec((1, 1, LANES), lambda b, c: (direction, 0, 0)),
        _const_spec(e64.shape),
    ]
    args = [xact, dt, a_log, e64]
    if has_init:
        in_specs.append(st_spec)
        args.append(h0)
    out_specs, out_shape = [], []
    if want_y:
        out_specs.append(pl.BlockSpec((1, CHUNK, D_INNER), lambda b, c: (b, chunk(c), 0)))
        out_shape.append(jax.ShapeDtypeStruct((B, L, D_INNER), BF16))
    out_specs.append(st_spec)
    out_shape.append(jax.ShapeDtypeStruct((B,) + st_shape, F32))
    scratch = [pltpu.VMEM(st_shape, F32), pltpu.VMEM((CHUNK, D_INNER), F32),
               pltpu.VMEM((2 * SSM_HEADS, CHUNK), F32)]
    return pl.pallas_call(
        functools.partial(_ssd_kernel, reverse=reverse, nc=nc, has_init=has_init, want_y=want_y),
        grid=(B, nc),
        in_specs=in_specs,
        out_specs=out_specs,
        out_shape=out_shape,
        scratch_shapes=scratch,
        compiler_params=pltpu.CompilerParams(
            dimension_semantics=("arbitrary", "arbitrary"), vmem_limit_bytes=VMEM_LIMIT),
        name="ssd_bwd" if reverse else "ssd_fwd",
    )(*args)


def _expand_matrix(width):
    k = np.arange(LANES)[:, None]
    col_head = (np.arange(SSM_HEADS * width) // width)[None, :]
    return jnp.asarray((k % SSM_HEADS == col_head) & (k < 3 * SSM_HEADS), dtype=BF16)


def kernel(x, c, ctx, c_ctx, ada_w, ada_b, norm_mix_g, norm_ffn_g, attn_w_qkv, attn_w_o, attn_sinks,
           ssm_w_in, ssm_conv_w, ssm_conv_b, ssm_dt_bias, ssm_A_log, ssm_D, ssm_norm_g, ssm_w_out,
           ffn_w_gate, ffn_w_up, ffn_w_down, final_norm_g):
    B, L, _ = x.shape
    assert x.shape == (B, L, D_MODEL) and B <= CTX_ROW and L % 1024 == 0
    bf = lambda w: w.astype(BF16)
    row2 = lambda v: v.reshape(1, -1)

    cc = jnp.zeros((MOD_ROWS, D_MODEL), F32).at[:B].set(c).at[CTX_ROW].set(c_ctx)
    mod = _adaln(cc, ada_w, ada_b)
    mods = [mod[i].reshape(MOD_ROWS, 1, 6 * D_MODEL) for i in range(DEPTH)]
    x_row = lambda b: b
    ctx_row = lambda b: CTX_ROW

    w_qkv = attn_w_qkv[0]
    w_q_t = bf(w_qkv[:, :Q_DIM].T)
    w_k = bf(w_qkv[:, Q_DIM:Q_DIM + KV_DIM])
    w_v_t = bf(w_qkv[:, Q_DIM + KV_DIM:].T)
    g_mix = row2(norm_mix_g[0])
    q, k, v = _qkv(x, mods[0], g_mix, w_q_t, w_k, w_v_t, _rope_tables(L), x_row)
    qc, kc, vc = _qkv(ctx, mods[0], g_mix, w_q_t, w_k, w_v_t, None, ctx_row)
    sinks = attn_sinks[0].astype(F32)
    o = _attn(q, k, v, kc, vc, sinks)
    oc = _attn(qc, None, None, kc, vc, sinks)
    w_gate, w_up, w_down = bf(ffn_w_gate), bf(ffn_w_up), bf(ffn_w_down)
    ffn0 = (bf(attn_w_o), row2(norm_ffn_g[0]), w_gate, w_up, w_down, 0)
    x = _mix_ffn(x, mods[0], x_row, o, *ffn0)
    ctx = _mix_ffn(ctx, mods[0], ctx_row, oc, *ffn0)

    w_in = ssm_w_in[0]
    w_dt_raw = w_in[:, D_INNER + CONV_DIM:]
    w_dt = jnp.zeros((D_MODEL, 2 * LANES), F32)
    w_dt = w_dt.at[:, :SSM_HEADS].set(w_dt_raw[:, :SSM_HEADS])
    w_dt = bf(w_dt.at[:, LANES:LANES + SSM_HEADS].set(w_dt_raw[:, SSM_HEADS:]))
    dt_bias = jnp.zeros((2, LANES), F32).at[:, :SSM_HEADS].set(ssm_dt_bias[0]).reshape(1, 2 * LANES)
    a_log = jnp.zeros((2, 1, LANES), F32).at[:, 0, :SSM_HEADS].set(ssm_A_log[0])
    conv_w = jnp.zeros((8, CONV_DIM), F32).at[:3].set(ssm_conv_w[0])
    conv_b = row2(ssm_conv_b[0])
    d_skip = row2(jnp.repeat(ssm_D[0], SSM_HEAD_DIM))
    e64 = _expand_matrix(SSM_HEAD_DIM)
    g_mix1 = row2(norm_mix_g[1])

    inproj = functools.partial(_inproj, g=g_mix1, w_in=bf(w_in), w_dt=w_dt, dt_bias=dt_bias,
                               conv_w=conv_w, conv_b=conv_b)
    z, xact, dt = inproj(x, mods[1], x_row)
    _, xact_c, dt_c = inproj(ctx, mods[1], ctx_row)
    ssd = functools.partial(_ssd, a_log=a_log, e64=e64)
    (h_f,) = ssd(xact_c, dt_c, h0=None, direction=0, want_y=False)
    (h_b,) = ssd(xact_c, dt_c, h0=None, direction=1, want_y=False)
    y_f, _ = ssd(xact, dt, h0=h_f, direction=0, want_y=True)
    y_b, _ = ssd(xact, dt, h0=h_b, direction=1, want_y=True)
    return _mix_ffn(x, mods[1], x_row, (y_f, y_b, z, xact), bf(ssm_w_out), row2(norm_ffn_g[1]),
                    w_gate, w_up, w_down, 1,
                    ssm_norm_g=row2(ssm_norm_g[0]), d_skip=d_skip, final_g=row2(final_norm_g))
```

```python
import functools
import math

import numpy as np

import jax
import jax.numpy as jnp
from jax import lax
from jax.experimental import pallas as pl
from jax.experimental.pallas import tpu as pltpu

F32 = jnp.float32
BF16 = jnp.bfloat16

D_MODEL = 1024
DEPTH = 2
GRID_W = 64
EPS = 1e-6

HEAD_DIM = 64
N_HEADS = D_MODEL // HEAD_DIM
N_KV_HEADS = N_HEADS // 4
Q_PER_KV = N_HEADS // N_KV_HEADS
Q_DIM = N_HEADS * HEAD_DIM
KV_DIM = N_KV_HEADS * HEAD_DIM
QKV_DIM = Q_DIM + 2 * KV_DIM
WINDOW = 128
BLOCK = 128
ROPE_FREQS = HEAD_DIM // 4
ROPE_BASE = 10000.0

D_INNER = 2 * D_MODEL
SSM_HEAD_DIM = 64
SSM_HEADS = D_INNER // SSM_HEAD_DIM
SSM_GROUPS = 8
HEADS_PER_GROUP = SSM_HEADS // SSM_GROUPS
GROUP_DIM = D_INNER // SSM_GROUPS
D_STATE = 128
CHUNK = 128
BC_DIM = SSM_GROUPS * D_STATE
CONV_DIM = D_INNER + 2 * BC_DIM
D_FF = ((8 * D_MODEL // 3 + 255) // 256) * 256

LANES = 128
SUBLANES = 8
MOD_ROWS = 8
CTX_ROW = 4
VMEM_LIMIT = 48 * 1024 * 1024
VMEM_LIMIT_FFN = 56 * 1024 * 1024
NEG_INF = float("-inf")
LOG2E = math.log2(math.e)
HEADS_PER_DOT = N_HEADS // 2


def _const_spec(shape):
    nd = len(shape)
    return pl.BlockSpec(shape, lambda *_: (0,) * nd, pipeline_mode=pl.Buffered(1))


def _layer_spec(stacked, layer):
    return pl.BlockSpec((None,) + stacked.shape[1:], lambda *_: (layer, 0, 0),
                        pipeline_mode=pl.Buffered(1))


def _silu(v):
    return v * jax.nn.sigmoid(v)


def _rms_mod(x, g, scale, shift):
    ms = jnp.mean(x * x, axis=-1, keepdims=True)
    return (x * lax.rsqrt(ms + EPS) * g) * (1.0 + scale) + shift


def _mod_chunk(m, i):
    return m[:, i * D_MODEL:(i + 1) * D_MODEL]


def _adaln_kernel(c_ref, w_ref, b_ref, o_ref):
    s = _silu(c_ref[...])
    o_ref[0] = jnp.dot(s, w_ref[0], preferred_element_type=F32) + b_ref[0]


def _adaln(cc, ada_w, ada_b):
    tn = 1536
    n = 6 * D_MODEL
    return pl.pallas_call(
        _adaln_kernel,
        grid=(DEPTH, n // tn),
        in_specs=[
            pl.BlockSpec((MOD_ROWS, D_MODEL), lambda i, j: (0, 0)),
            pl.BlockSpec((1, D_MODEL, tn), lambda i, j: (i, 0, j)),
            pl.BlockSpec((1, 1, tn), lambda i, j: (i, 0, j)),
        ],
        out_specs=pl.BlockSpec((1, MOD_ROWS, tn), lambda i, j: (i, 0, j)),
        out_shape=jax.ShapeDtypeStruct((DEPTH, MOD_ROWS, n), F32),
        compiler_params=pltpu.CompilerParams(
            dimension_semantics=("arbitrary", "arbitrary"), vmem_limit_bytes=VMEM_LIMIT),
        name="adaln",
    )(cc, ada_w, ada_b.reshape(DEPTH, 1, n))


_NT = (((1,), (1,)), ((), ()))
_TN = (((0,), (0,)), ((), ()))


def _qkv_kernel(*refs, rope):
    if rope:
        (x_ref, mod_ref, g_ref, wq_ref, wk_ref, wv_ref, cos_t_ref, sin_t_ref,
         cos_ref, sa_ref, sb_ref, qt_ref, k_ref, vt_ref) = refs
    else:
        x_ref, mod_ref, g_ref, wq_ref, wk_ref, wv_ref, qt_ref, k_ref, vt_ref = refs
    m = mod_ref[0]
    h = _rms_mod(x_ref[0], g_ref[...], _mod_chunk(m, 1), _mod_chunk(m, 0)).astype(BF16)
    q_t = lax.dot_general(wq_ref[...], h, _NT, preferred_element_type=F32)
    k = jnp.dot(h, wk_ref[...], preferred_element_type=F32)
    v_t = lax.dot_general(wv_ref[...], h, _NT, preferred_element_type=F32).astype(BF16)
    n_blk = x_ref.shape[1] // BLOCK

    def put(ref, rows, val):
        for j in range(n_blk):
            ref[0, j, rows, :] = val[:, j * BLOCK:(j + 1) * BLOCK]

    put(vt_ref, slice(0, KV_DIM), v_t)
    scale = HEAD_DIM ** -0.5 * LOG2E
    if rope:
        f = ROPE_FREQS
        for head in range(N_HEADS):
            for axis in range(2):
                r0 = head * HEAD_DIM + axis * 2 * f
                cs = cos_t_ref[axis * f:(axis + 1) * f, :]
                sn = sin_t_ref[axis * f:(axis + 1) * f, :]
                x1 = q_t[r0:r0 + f, :]
                x2 = q_t[r0 + f:r0 + 2 * f, :]
                put(qt_ref, slice(r0, r0 + f), ((x1 * cs - x2 * sn) * scale).astype(BF16))
                put(qt_ref, slice(r0 + f, r0 + 2 * f), ((x2 * cs + x1 * sn) * scale).astype(BF16))
        for blk in range(KV_DIM // LANES):
            t = k[:, blk * LANES:(blk + 1) * LANES]
            t = (t * cos_ref[...] + pltpu.roll(t, LANES - f, 1) * sa_ref[...]
                 + pltpu.roll(t, f, 1) * sb_ref[...])
            k_ref[0, :, blk * LANES:(blk + 1) * LANES] = t.astype(BF16)
    else:
        put(qt_ref, slice(0, Q_DIM), (q_t * scale).astype(BF16))
        k_ref[0] = k.astype(BF16)


def _rope_tables(L):
    f32 = np.float32
    rows = L // GRID_W
    row = np.repeat(np.arange(rows, dtype=f32), GRID_W)
    col = np.tile(np.arange(GRID_W, dtype=f32), rows)
    inv = (f32(ROPE_BASE) ** (-np.arange(ROPE_FREQS, dtype=f32) / f32(ROPE_FREQS))).astype(f32)
    ang_r = (row[:, None] * inv).astype(f32)
    ang_c = (col[:, None] * inv).astype(f32)
    zero = np.zeros_like(ang_r)

    def lanes(r_first, r_second, c_first, c_second):
        head = np.concatenate([r_first, r_second, c_first, c_second], axis=-1)
        return np.tile(head, (1, LANES // HEAD_DIM)).astype(f32)

    cr, sr, cc, sc = np.cos(ang_r), np.sin(ang_r), np.cos(ang_c), np.sin(ang_c)
    cos_t = np.ascontiguousarray(np.concatenate([cr, cc], axis=1).T).astype(f32)
    sin_t = np.ascontiguousarray(np.concatenate([sr, sc], axis=1).T).astype(f32)
    return (cos_t, sin_t, lanes(cr, cr, cc, cc), lanes(-sr, zero, -sc, zero),
            lanes(zero, sr, zero, sc))


def _qkv(x, mod, g, w_q_t, w_k, w_v_t, tables, mod_row):
    B, L, _ = x.shape
    tm = min(L, 1024)
    rope = tables is not None
    in_specs = [
        pl.BlockSpec((1, tm, D_MODEL), lambda b, i: (b, i, 0)),
        pl.BlockSpec((1, 1, 6 * D_MODEL), lambda b, i: (mod_row(b), 0, 0)),
        _const_spec((1, D_MODEL)),
        _const_spec(w_q_t.shape), _const_spec(w_k.shape), _const_spec(w_v_t.shape),
    ]
    args = [x, mod, g, w_q_t, w_k, w_v_t]
    if rope:
        in_specs += [pl.BlockSpec((2 * ROPE_FREQS, tm), lambda b, i: (0, i))] * 2
        in_specs += [pl.BlockSpec((tm, LANES), lambda b, i: (i, 0))] * 3
        args += list(tables)
    return pl.pallas_call(
        functools.partial(_qkv_kernel, rope=rope),
        grid=(B, L // tm),
        in_specs=in_specs,
        out_specs=[
            pl.BlockSpec((1, tm // BLOCK, Q_DIM, BLOCK), lambda b, i: (b, i, 0, 0)),
            pl.BlockSpec((1, tm, KV_DIM), lambda b, i: (b, i, 0)),
            pl.BlockSpec((1, tm // BLOCK, KV_DIM, BLOCK), lambda b, i: (b, i, 0, 0)),
        ],
        out_shape=[
            jax.ShapeDtypeStruct((B, L // BLOCK, Q_DIM, BLOCK), BF16),
            jax.ShapeDtypeStruct((B, L, KV_DIM), BF16),
            jax.ShapeDtypeStruct((B, L // BLOCK, KV_DIM, BLOCK), BF16),
        ],
        compiler_params=pltpu.CompilerParams(
            dimension_semantics=("arbitrary", "arbitrary"), vmem_limit_bytes=VMEM_LIMIT),
        name="qkv_rope" if rope else "qkv_ctx",
    )(*args)


def _attn_kernel(*refs, window, nq, n_ctx):
    if window:
        (sink_ref, qt_ref, kp_ref, kc_ref, kn_ref, vp_ref, vc_ref, vn_ref,
         kx_ref, vx_ref, ot_ref, kbuf, vtbuf) = refs
    else:
        sink_ref, qt_ref, kx_ref, vx_ref, ot_ref, kbuf, vtbuf = refs
    width = HEADS_PER_DOT * BLOCK
    kbuf[0:n_ctx, :] = kx_ref[0]
    for j in range(n_ctx // BLOCK):
        vtbuf[:, j * BLOCK:(j + 1) * BLOCK] = vx_ref[0, j]
    if window:
        for j, (kr, vr) in enumerate(((kp_ref, vp_ref), (kc_ref, vc_ref), (kn_ref, vn_ref))):
            kbuf[n_ctx + j * BLOCK:n_ctx + (j + 1) * BLOCK, :] = kr[0]
            vtbuf[:, n_ctx + j * BLOCK:n_ctx + (j + 1) * BLOCK] = vr[0, 0]
        n = pl.program_id(1)
        kj = lax.broadcasted_iota(jnp.int32, (BLOCK, width), 0)
        qi = lax.broadcasted_iota(jnp.int32, (BLOCK, width), 1) % BLOCK
        bias_prev = jnp.where((kj >= qi) & (n > 0), 0.0, NEG_INF).astype(F32)
        bias_next = jnp.where((kj <= qi) & (n < nq - 1), 0.0, NEG_INF).astype(F32)
    lane = lax.broadcasted_iota(jnp.int32, (1, width), 1)
    ones_rows = jnp.ones((16, kbuf.shape[0]), BF16)
    for h0 in range(0, N_HEADS, HEADS_PER_DOT):
        cols = []
        for h in range(h0, h0 + HEADS_PER_DOT):
            g = h // Q_PER_KV
            parts = [qt_ref[0, 0, h * HEAD_DIM:(h + 1) * HEAD_DIM, :]]
            if g > 0:
                parts.insert(0, jnp.zeros((g * HEAD_DIM, BLOCK), BF16))
            if g < N_KV_HEADS - 1:
                parts.append(jnp.zeros(((N_KV_HEADS - 1 - g) * HEAD_DIM, BLOCK), BF16))
            cols.append(jnp.concatenate(parts, axis=0) if len(parts) > 1 else parts[0])
        q_exp = jnp.concatenate(cols, axis=1) if len(cols) > 1 else cols[0]
        sink = jnp.full((1, width), sink_ref[h0], F32)
        for r in range(1, HEADS_PER_DOT):
            sink = jnp.where(lane >= r * BLOCK, sink_ref[h0 + r], sink)
        sink = sink * LOG2E
        s = jnp.dot(kbuf[...], q_exp, preferred_element_type=F32)
        if window:
            s = jnp.concatenate([
                s[:n_ctx], s[n_ctx:n_ctx + BLOCK] + bias_prev,
                s[n_ctx + BLOCK:n_ctx + 2 * BLOCK], s[n_ctx + 2 * BLOCK:] + bias_next], axis=0)
        mx = jnp.maximum(jnp.max(s, axis=0, keepdims=True), sink)
        p = jnp.exp2(s - mx).astype(BF16)
        sink_p = jnp.exp2(sink - mx)
        gw = Q_PER_KV * BLOCK
        for g0 in range(0, HEADS_PER_DOT // Q_PER_KV):
            g = h0 // Q_PER_KV + g0
            gl = slice(g0 * gw, (g0 + 1) * gw)
            v_ext = jnp.concatenate([vtbuf[g * HEAD_DIM:(g + 1) * HEAD_DIM, :], ones_rows], axis=0)
            pv = jnp.dot(v_ext, p[:, gl], preferred_element_type=F32)
            o_t = pv[:HEAD_DIM, :] / (pv[HEAD_DIM:HEAD_DIM + 1, :] + sink_p[:, gl])
            for r in range(Q_PER_KV):
                h = g * Q_PER_KV + r
                ot_ref[0, 0, h * HEAD_DIM:(h + 1) * HEAD_DIM, :] = (
                    o_t[:, r * BLOCK:(r + 1) * BLOCK].astype(BF16))


def _attn(q_t, k, v_t, kx, vx_t, sinks):
    B, nq = q_t.shape[:2]
    C = kx.shape[1]
    window = k is not None
    smem = pl.BlockSpec(memory_space=pltpu.SMEM)
    q_spec = pl.BlockSpec((1, 1, Q_DIM, BLOCK), lambda b, n: (b, n, 0, 0))
    kx_spec = pl.BlockSpec((1, C, KV_DIM), lambda b, n: (b, 0, 0))
    vx_spec = pl.BlockSpec((1, C // BLOCK, KV_DIM, BLOCK), lambda b, n: (b, 0, 0, 0))
    if window:
        prev = lambda n: jnp.maximum(n - 1, 0)
        nxt = lambda n: jnp.minimum(n + 1, nq - 1)
        k_spec = lambda f: pl.BlockSpec((1, BLOCK, KV_DIM), lambda b, n: (b, f(n), 0))
        v_spec = lambda f: pl.BlockSpec((1, 1, KV_DIM, BLOCK), lambda b, n: (b, f(n), 0, 0))
        same = lambda n: n
        in_specs = [smem, q_spec, k_spec(prev), k_spec(same), k_spec(nxt),
                    v_spec(prev), v_spec(same), v_spec(nxt), kx_spec, vx_spec]
        args = (sinks, q_t, k, k, k, v_t, v_t, v_t, kx, vx_t)
        n_keys = C + 3 * BLOCK
    else:
        in_specs = [smem, q_spec, kx_spec, vx_spec]
        args = (sinks, q_t, kx, vx_t)
        n_keys = C
    return pl.pallas_call(
        functools.partial(_attn_kernel, window=window, nq=nq, n_ctx=C),
        grid=(B, nq),
        in_specs=in_specs,
        out_specs=q_spec,
        out_shape=jax.ShapeDtypeStruct((B, nq, Q_DIM, BLOCK), BF16),
        scratch_shapes=[pltpu.VMEM((n_keys, KV_DIM), BF16), pltpu.VMEM((KV_DIM, n_keys), BF16)],
        compiler_params=pltpu.CompilerParams(
            dimension_semantics=("arbitrary", "arbitrary"), vmem_limit_bytes=VMEM_LIMIT),
        name="attn_window" if window else "attn_ctx",
    )(*args)


def _mix_ffn_kernel(*refs, ssm, final):
    if ssm:
        (x_ref, mod_ref, yf_ref, yb_ref, z_ref, xs_ref, dsk_ref, ng_ref,
         wo_ref, gf_ref, wg_ref, wu_ref, wd_ref) = refs[:13]
        rest = refs[13:]
    else:
        (x_ref, mod_ref, o_ref_in, wo_ref, gf_ref, wg_ref, wu_ref, wd_ref) = refs[:8]
        rest = refs[8:]
    if final:
        fg_ref, out_ref = rest
    else:
        (out_ref,) = rest
    m = mod_ref[0]
    if ssm:
        mix = None
        for g in range(SSM_GROUPS):
            gs = slice(g * GROUP_DIM, (g + 1) * GROUP_DIM)
            yg = ((yf_ref[0, :, gs].astype(F32) + yb_ref[0, :, gs].astype(F32)
                   + dsk_ref[:, gs] * xs_ref[0, :, gs].astype(F32))
                  * _silu(z_ref[0, :, gs].astype(F32)))
            ms = jnp.mean(yg * yg, axis=-1, keepdims=True)
            part = (yg * lax.rsqrt(ms + EPS) * ng_ref[:, gs]).astype(BF16)
            term = jnp.dot(part, wo_ref[gs, :], preferred_element_type=F32)
            mix = term if mix is None else mix + term
    else:
        o_t = jnp.concatenate([o_ref_in[0, j] for j in range(o_ref_in.shape[1])], axis=1)
        mix = lax.dot_general(o_t, wo_ref[...], _TN, preferred_element_type=F32)
    x1 = x_ref[0] + _mod_chunk(m, 2) * mix
    h2 = _rms_mod(x1, gf_ref[...], _mod_chunk(m, 4), _mod_chunk(m, 3)).astype(BF16)
    gate = jnp.dot(h2, wg_ref[...], preferred_element_type=F32)
    up = jnp.dot(h2, wu_ref[...], preferred_element_type=F32)
    act = (_silu(gate) * up).astype(BF16)
    x2 = x1 + _mod_chunk(m, 5) * jnp.dot(act, wd_ref[...], preferred_element_type=F32)
    if final:
        ms = jnp.mean(x2 * x2, axis=-1, keepdims=True)
        x2 = x2 * lax.rsqrt(ms + EPS) * fg_ref[...]
    out_ref[0] = x2


def _mix_ffn(x, mod, mod_row, mixer_in, w_mix, g_ffn, w_gate, w_up, w_down, layer,
             ssm_norm_g=None, d_skip=None, final_g=None):
    B, L, _ = x.shape
    tm = min(L, 512)
    ssm = ssm_norm_g is not None
    final = final_g is not None
    row = lambda width: pl.BlockSpec((1, tm, width), lambda b, i: (b, i, 0))
    in_specs = [row(D_MODEL), pl.BlockSpec((1, 1, 6 * D_MODEL), lambda b, i: (mod_row(b), 0, 0))]
    args = [x, mod]
    if ssm:
        in_specs += [row(D_INNER)] * 4 + [_const_spec((1, D_INNER))] * 2
        args += list(mixer_in) + [d_skip, ssm_norm_g]
    else:
        in_specs += [pl.BlockSpec((1, tm // BLOCK, Q_DIM, BLOCK), lambda b, i: (b, i, 0, 0))]
        args += [mixer_in]
    in_specs += [_layer_spec(w_mix, 0), _const_spec((1, D_MODEL)), _layer_spec(w_gate, layer),
                 _layer_spec(w_up, layer), _layer_spec(w_down, layer)]
    args += [w_mix, g_ffn, w_gate, w_up, w_down]
    if final:
        in_specs += [_const_spec((1, D_MODEL))]
        args += [final_g]
    return pl.pallas_call(
        functools.partial(_mix_ffn_kernel, ssm=ssm, final=final),
        grid=(B, L // tm),
        in_specs=in_specs,
        out_specs=row(D_MODEL),
        out_shape=jax.ShapeDtypeStruct((B, L, D_MODEL), F32),
        compiler_params=pltpu.CompilerParams(
            dimension_semantics=("arbitrary", "arbitrary"), vmem_limit_bytes=VMEM_LIMIT_FFN),
        name="ssm_out_ffn" if ssm else "attn_out_ffn",
    )(*args)


def _inproj_kernel(x_ref, xp_ref, xn_ref, mod_ref, g_ref, w_ref, wdt_ref, dtb_ref, cw_ref, cb_ref,
                   z_ref, xa_ref, dt_ref, *, n_tiles):
    i = pl.program_id(1)
    m = mod_ref[0]
    g, scale, shift = g_ref[...], _mod_chunk(m, 1), _mod_chunk(m, 0)
    tm = x_ref.shape[1]
    h = _rms_mod(x_ref[0], g, scale, shift).astype(BF16)
    x_ext = jnp.concatenate([xp_ref[0], x_ref[0], xn_ref[0]], axis=0)
    h_ext = _rms_mod(x_ext, g, scale, shift).astype(BF16)
    z_ref[0] = jnp.dot(h, w_ref[:, :D_INNER], preferred_element_type=F32).astype(BF16)
    raw = jnp.dot(h, wdt_ref[...], preferred_element_type=F32) + dtb_ref[...]
    dt_ref[0] = jnp.maximum(raw, 0.0) + jnp.log1p(jnp.exp(-jnp.abs(raw)))
    keep_prev = jnp.where(i > 0, 1.0, 0.0)
    keep_next = jnp.where(i < n_tiles - 1, 1.0, 0.0)
    cstep = 512
    for j in range(CONV_DIM // cstep):
        sl = slice(j * cstep, (j + 1) * cstep)
        u = jnp.dot(h_ext, w_ref[:, D_INNER + j * cstep:D_INNER + (j + 1) * cstep],
                    preferred_element_type=F32)
        u = jnp.concatenate([u[0:SUBLANES] * keep_prev, u[SUBLANES:SUBLANES + tm],
                             u[SUBLANES + tm:] * keep_next], axis=0)
        rows = tm + 2 * SUBLANES
        below = pltpu.roll(u, 1, 0)[SUBLANES:SUBLANES + tm]
        above = pltpu.roll(u, rows - 1, 0)[SUBLANES:SUBLANES + tm]
        conv = (below * cw_ref[0:1, sl] + u[SUBLANES:SUBLANES + tm] * cw_ref[1:2, sl]
                + above * cw_ref[2:3, sl] + cb_ref[:, sl])
        xa_ref[0, :, sl] = _silu(conv).astype(BF16)


def _inproj(x, mod, mod_row, *, g, w_in, w_dt, dt_bias, conv_w, conv_b):
    B, L, _ = x.shape
    tm = min(L, 256)
    n_tiles = L // tm
    per = tm // SUBLANES
    row = lambda width: pl.BlockSpec((1, tm, width), lambda b, i: (b, i, 0))
    return pl.pallas_call(
        functools.partial(_inproj_kernel, n_tiles=n_tiles),
        grid=(B, n_tiles),
        in_specs=[row(D_MODEL),
                  pl.BlockSpec((1, SUBLANES, D_MODEL), lambda b, i: (b, jnp.maximum(i * per - 1, 0), 0)),
                  pl.BlockSpec((1, SUBLANES, D_MODEL),
                               lambda b, i: (b, jnp.minimum((i + 1) * per, L // SUBLANES - 1), 0)),
                  pl.BlockSpec((1, 1, 6 * D_MODEL), lambda b, i: (mod_row(b), 0, 0)),
                  _const_spec((1, D_MODEL)), _const_spec(w_in.shape), _const_spec(w_dt.shape),
                  _const_spec((1, 2 * LANES)), _const_spec((8, CONV_DIM)), _const_spec((1, CONV_DIM))],
        out_specs=[row(D_INNER), row(CONV_DIM), row(2 * LANES)],
        out_shape=[jax.ShapeDtypeStruct((B, L, D_INNER), BF16),
                   jax.ShapeDtypeStruct((B, L, CONV_DIM), BF16),
                   jax.ShapeDtypeStruct((B, L, 2 * LANES), F32)],
        compiler_params=pltpu.CompilerParams(
            dimension_semantics=("arbitrary", "arbitrary"), vmem_limit_bytes=VMEM_LIMIT),
        name="ssm_inproj",
    )(x, x, x, mod, g, w_in, w_dt, dt_bias, conv_w, conv_b)


def _split3(v):
    hi = v.astype(BF16)
    r1 = v - hi.astype(F32)
    mid = r1.astype(BF16)
    lo = (r1 - mid.astype(F32)).astype(BF16)
    return hi, mid, lo


def _ssd_kernel(*refs, reverse, nc, has_init, want_y):
    (xa_ref, dt_ref, alog_ref, e64_ref), rest = refs[:4], refs[4:]
    if has_init:
        h0_ref, rest = rest[0], rest[1:]
    if want_y:
        y_ref, rest = rest[0], rest[1:]
    hfin_ref, state, ce, tbuf = rest
    c = pl.program_id(1)
    last = 0 if reverse else CHUNK - 1

    @pl.when(c == 0)
    def _():
        state[...] = h0_ref[0] if has_init else jnp.zeros(state.shape, F32)

    lane = lax.broadcasted_iota(jnp.int32, (CHUNK, LANES), 1)
    row = lax.broadcasted_iota(jnp.int32, (CHUNK, LANES), 0)
    lane_valid = (lane, (lane >= row) if reverse else (lane <= row))
    _prepare_decay(dt_ref, alog_ref, e64_ref, ce, tbuf, lane_valid, last)
    _scan_chunk(xa_ref, y_ref if want_y else None, state, ce, tbuf, lane_valid, last)

    @pl.when(c == nc - 1)
    def _():
        hfin_ref[0] = state[...]


def _prepare_decay(dt_ref, alog_ref, e64_ref, ce, tbuf, lane_valid, last):
    lane, valid = lane_valid
    dt = jnp.where(lane < SSM_HEADS, dt_ref[0], 0.0)
    a = dt * (-LOG2E * jnp.exp(alog_ref[0]))
    tri = jnp.where(valid, 1.0, 0.0).astype(BF16)
    a_hi, a_mid, a_lo = _split3(a)
    cum = (jnp.dot(tri, a_hi, preferred_element_type=F32)
           + jnp.dot(tri, a_mid, preferred_element_type=F32)
           + jnp.dot(tri, a_lo, preferred_element_type=F32))
    c_hi, c_mid, c_lo = _split3(cum)
    pieces = (c_hi.astype(F32) + pltpu.roll(c_mid.astype(F32), SSM_HEADS, 1)
              + pltpu.roll(c_lo.astype(F32), 2 * SSM_HEADS, 1)).astype(BF16)
    ce[...] = jnp.dot(pieces, e64_ref[...], preferred_element_type=F32)
    tb = (cum + pltpu.roll(dt, SSM_HEADS, 1)).T
    cum_t = tb[0:SSM_HEADS, :]
    dt_t = tb[SSM_HEADS:2 * SSM_HEADS, :]
    tbuf[0:SSM_HEADS, :] = cum_t - jnp.log2(dt_t)
    tbuf[SSM_HEADS:2 * SSM_HEADS, :] = jnp.exp2(cum_t[:, last:last + 1] - cum_t) * dt_t


def _scan_chunk(xa_ref, y_ref, state, ce, tbuf, lane_valid, last):
    lane, valid = lane_valid
    want_y = y_ref is not None
    lane_g = lax.broadcasted_iota(jnp.int32, (CHUNK, GROUP_DIM), 1)
    head_mask = [jnp.where((lane_g >= hl * SSM_HEAD_DIM) & (lane_g < (hl + 1) * SSM_HEAD_DIM),
                           1.0, 0.0).astype(BF16) for hl in range(HEADS_PER_GROUP)]
    low_half = lane < SSM_HEAD_DIM
    for g in range(SSM_GROUPS):
        gs = slice(g * GROUP_DIM, (g + 1) * GROUP_DIM)
        b_g = xa_ref[0, :, D_INNER + g * D_STATE:D_INNER + (g + 1) * D_STATE]
        c_g = xa_ref[0, :, D_INNER + BC_DIM + g * D_STATE:D_INNER + BC_DIM + (g + 1) * D_STATE]
        b_t = b_g.astype(F32).T
        x_g = xa_ref[0, :, gs]
        if want_y:
            cb_mat = lax.dot_general(c_g, b_g, _NT, preferred_element_type=F32)
            y_g = (jnp.dot(c_g, state[g].astype(BF16), preferred_element_type=F32)
                   * jnp.exp2(ce[:, gs]))
        st = jnp.zeros((D_STATE, GROUP_DIM), F32)
        for hp in range(HEADS_PER_GROUP // 2):
            pair = g * (HEADS_PER_GROUP // 2) + hp
            lhs_y, lhs_s, rhs = [], [], []
            if want_y:
                ce_pair = ce[:, pair * LANES:(pair + 1) * LANES]
                swapped = pltpu.roll(ce_pair, SSM_HEAD_DIM, 1)
            for hh in range(2):
                hl = 2 * hp + hh
                h = g * HEADS_PER_GROUP + hl
                if want_y:
                    own = low_half if hh == 0 else jnp.logical_not(low_half)
                    cum_i = jnp.where(own, ce_pair, swapped)
                    dec = jnp.exp2(jnp.where(valid, cum_i - tbuf[h:h + 1, :], NEG_INF))
                    lhs_y.append((dec * cb_mat).astype(BF16))
                lhs_s.append((b_t * tbuf[SSM_HEADS + h:SSM_HEADS + h + 1, :]).astype(BF16))
                rhs.append(x_g * head_mask[hl])
            rhs = jnp.concatenate(rhs, axis=0)
            if want_y:
                y_g = y_g + jnp.dot(jnp.concatenate(lhs_y, axis=1), rhs, preferred_element_type=F32)
            st = st + jnp.dot(jnp.concatenate(lhs_s, axis=1), rhs, preferred_element_type=F32)
        if want_y:
            y_ref[0, :, gs] = y_g.astype(BF16)
        state[g] = state[g] * jnp.exp2(ce[last:last + 1, gs]) + st


def _ssd(xact, dt, a_log, e64, h0, direction, want_y):
    B, L, _ = xact.shape
    nc = L // CHUNK
    reverse = direction == 1
    has_init = h0 is not None
    chunk = (lambda c: nc - 1 - c) if reverse else (lambda c: c)
    st_shape = (SSM_GROUPS, D_STATE, GROUP_DIM)
    st_spec = pl.BlockSpec((1,) + st_shape, lambda b, c: (b, 0, 0, 0))
    in_specs = [
        pl.BlockSpec((1, CHUNK, CONV_DIM), lambda b, c: (b, chunk(c), 0)),
        pl.BlockSpec((1, CHUNK, LANES), lambda b, c: (b, chunk(c), direction)),
        pl.BlockSpec((1, 1, LANES), lambda b, c: (direction, 0, 0)),
        _const_spec(e64.shape),
    ]
    args = [xact, dt, a_log, e64]
    if has_init:
        in_specs.append(st_spec)
        args.append(h0)
    out_specs, out_shape = [], []
    if want_y:
        out_specs.append(pl.BlockSpec((1, CHUNK, D_INNER), lambda b, c: (b, chunk(c), 0)))
        out_shape.append(jax.ShapeDtypeStruct((B, L, D_INNER), BF16))
    out_specs.append(st_spec)
    out_shape.append(jax.ShapeDtypeStruct((B,) + st_shape, F32))
    scratch = [pltpu.VMEM(st_shape, F32), pltpu.VMEM((CHUNK, D_INNER), F32),
               pltpu.VMEM((2 * SSM_HEADS, CHUNK), F32)]
    return pl.pallas_call(
        functools.partial(_ssd_kernel, reverse=reverse, nc=nc, has_init=has_init, want_y=want_y),
        grid=(B, nc),
        in_specs=in_specs,
        out_specs=out_specs,
        out_shape=out_shape,
        scratch_shapes=scratch,
        compiler_params=pltpu.CompilerParams(
            dimension_semantics=("arbitrary", "arbitrary"), vmem_limit_bytes=VMEM_LIMIT),
        name="ssd_bwd" if reverse else "ssd_fwd",
    )(*args)


def _expand_matrix(width):
    k = np.arange(LANES)[:, None]
    col_head = (np.arange(SSM_HEADS * width) // width)[None, :]
    return jnp.asarray((k % SSM_HEADS == col_head) & (k < 3 * SSM_HEADS), dtype=BF16)


def kernel(x, c, ctx, c_ctx, ada_w, ada_b, norm_mix_g, norm_ffn_g, attn_w_qkv, attn_w_o, attn_sinks,
           ssm_w_in, ssm_conv_w, ssm_conv_b, ssm_dt_bias, ssm_A_log, ssm_D, ssm_norm_g, ssm_w_out,
           ffn_w_gate, ffn_w_up, ffn_w_down, final_norm_g):
    B, L, _ = x.shape
    assert x.shape == (B, L, D_MODEL) and B <= CTX_ROW and L % 1024 == 0
    bf = lambda w: w.astype(BF16)
    row2 = lambda v: v.reshape(1, -1)

    cc = jnp.zeros((MOD_ROWS, D_MODEL), F32).at[:B].set(c).at[CTX_ROW].set(c_ctx)
    mod = _adaln(cc, ada_w, ada_b)
    mods = [mod[i].reshape(MOD_ROWS, 1, 6 * D_MODEL) for i in range(DEPTH)]
    x_row = lambda b: b
    ctx_row = lambda b: CTX_ROW

    w_qkv = attn_w_qkv[0]
    w_q_t = bf(w_qkv[:, :Q_DIM].T)
    w_k = bf(w_qkv[:, Q_DIM:Q_DIM + KV_DIM])
    w_v_t = bf(w_qkv[:, Q_DIM + KV_DIM:].T)
    g_mix = row2(norm_mix_g[0])
    q, k, v = _qkv(x, mods[0], g_mix, w_q_t, w_k, w_v_t, _rope_tables(L), x_row)
    qc, kc, vc = _qkv(ctx, mods[0], g_mix, w_q_t, w_k, w_v_t, None, ctx_row)
    sinks = attn_sinks[0].astype(F32)
    o = _attn(q, k, v, kc, vc, sinks)
    oc = _attn(qc, None, None, kc, vc, sinks)
    w_gate, w_up, w_down = bf(ffn_w_gate), bf(ffn_w_up), bf(ffn_w_down)
    ffn0 = (bf(attn_w_o), row2(norm_ffn_g[0]), w_gate, w_up, w_down, 0)
    x = _mix_ffn(x, mods[0], x_row, o, *ffn0)
    ctx = _mix_ffn(ctx, mods[0], ctx_row, oc, *ffn0)

    w_in = ssm_w_in[0]
    w_dt_raw = w_in[:, D_INNER + CONV_DIM:]
    w_dt = jnp.zeros((D_MODEL, 2 * LANES), F32)
    w_dt = w_dt.at[:, :SSM_HEADS].set(w_dt_raw[:, :SSM_HEADS])
    w_dt = bf(w_dt.at[:, LANES:LANES + SSM_HEADS].set(w_dt_raw[:, SSM_HEADS:]))
    dt_bias = jnp.zeros((2, LANES), F32).at[:, :SSM_HEADS].set(ssm_dt_bias[0]).reshape(1, 2 * LANES)
    a_log = jnp.zeros((2, 1, LANES), F32).at[:, 0, :SSM_HEADS].set(ssm_A_log[0])
    conv_w = jnp.zeros((8, CONV_DIM), F32).at[:3].set(ssm_conv_w[0])
    conv_b = row2(ssm_conv_b[0])
    d_skip = row2(jnp.repeat(ssm_D[0], SSM_HEAD_DIM))
    e64 = _expand_matrix(SSM_HEAD_DIM)
    g_mix1 = row2(norm_mix_g[1])

    inproj = functools.partial(_inproj, g=g_mix1, w_in=bf(w_in), w_dt=w_dt, dt_bias=dt_bias,
                               conv_w=conv_w, conv_b=conv_b)
    z, xact, dt = inproj(x, mods[1], x_row)
    _, xact_c, dt_c = inproj(ctx, mods[1], ctx_row)
    ssd = functools.partial(_ssd, a_log=a_log, e64=e64)
    (h_f,) = ssd(xact_c, dt_c, h0=None, direction=0, want_y=False)
    (h_b,) = ssd(xact_c, dt_c, h0=None, direction=1, want_y=False)
    y_f, _ = ssd(xact, dt, h0=h_f, direction=0, want_y=True)
    y_b, _ = ssd(xact, dt, h0=h_b, direction=1, want_y=True)
    return _mix_ffn(x, mods[1], x_row, (y_f, y_b, z, xact), bf(ssm_w_out), row2(norm_ffn_g[1]),
                    w_gate, w_up, w_down, 1,
                    ssm_norm_g=row2(ssm_norm_g[0]), d_skip=d_skip, final_g=row2(final_norm_g))
```

```python
import functools
import math

import numpy as np

import jax
import jax.numpy as jnp
from jax import lax
from jax.experimental import pallas as pl
from jax.experimental.pallas import tpu as pltpu

F32 = jnp.float32
BF16 = jnp.bfloat16

D_MODEL = 1024
DEPTH = 2
GRID_W = 64
EPS = 1e-6

HEAD_DIM = 64
N_HEADS = D_MODEL // HEAD_DIM
N_KV_HEADS = N_HEADS // 4
Q_PER_KV = N_HEADS // N_KV_HEADS
Q_DIM = N_HEADS * HEAD_DIM
KV_DIM = N_KV_HEADS * HEAD_DIM
QKV_DIM = Q_DIM + 2 * KV_DIM
WINDOW = 128
BLOCK = 128
ROPE_FREQS = HEAD_DIM // 4
ROPE_BASE = 10000.0

D_INNER = 2 * D_MODEL
SSM_HEAD_DIM = 64
SSM_HEADS = D_INNER // SSM_HEAD_DIM
SSM_GROUPS = 8
HEADS_PER_GROUP = SSM_HEADS // SSM_GROUPS
GROUP_DIM = D_INNER // SSM_GROUPS
D_STATE = 128
CHUNK = 128
BC_DIM = SSM_GROUPS * D_STATE
CONV_DIM = D_INNER + 2 * BC_DIM
D_FF = ((8 * D_MODEL // 3 + 255) // 256) * 256

LANES = 128
SUBLANES = 8
MOD_ROWS = 8
CTX_ROW = 4
VMEM_LIMIT = 48 * 1024 * 1024
VMEM_LIMIT_FFN = 56 * 1024 * 1024
NEG_INF = float("-inf")
LOG2E = math.log2(math.e)
HEADS_PER_DOT = N_HEADS


def _const_spec(shape):
    nd = len(shape)
    return pl.BlockSpec(shape, lambda *_: (0,) * nd, pipeline_mode=pl.Buffered(1))


def _layer_spec(stacked, layer):
    return pl.BlockSpec((None,) + stacked.shape[1:], lambda *_: (layer, 0, 0),
                        pipeline_mode=pl.Buffered(1))


def _silu(v):
    return v * jax.nn.sigmoid(v)


def _rms_mod(x, g, scale, shift):
    ms = jnp.mean(x * x, axis=-1, keepdims=True)
    return (x * lax.rsqrt(ms + EPS) * g) * (1.0 + scale) + shift


def _mod_chunk(m, i):
    return m[:, i * D_MODEL:(i + 1) * D_MODEL]


def _adaln_kernel(c_ref, w_ref, b_ref, o_ref):
    s = _silu(c_ref[...])
    o_ref[0] = jnp.dot(s, w_ref[0], preferred_element_type=F32) + b_ref[0]


def _adaln(cc, ada_w, ada_b):
    tn = 1536
    n = 6 * D_MODEL
    return pl.pallas_call(
        _adaln_kernel,
        grid=(DEPTH, n // tn),
        in_specs=[
            pl.BlockSpec((MOD_ROWS, D_MODEL), lambda i, j: (0, 0)),
            pl.BlockSpec((1, D_MODEL, tn), lambda i, j: (i, 0, j)),
            pl.BlockSpec((1, 1, tn), lambda i, j: (i, 0, j)),
        ],
        out_specs=pl.BlockSpec((1, MOD_ROWS, tn), lambda i, j: (i, 0, j)),
        out_shape=jax.ShapeDtypeStruct((DEPTH, MOD_ROWS, n), F32),
        compiler_params=pltpu.CompilerParams(
            dimension_semantics=("arbitrary", "arbitrary"), vmem_limit_bytes=VMEM_LIMIT),
        name="adaln",
    )(cc, ada_w, ada_b.reshape(DEPTH, 1, n))


_NT = (((1,), (1,)), ((), ()))
_TN = (((0,), (0,)), ((), ()))


def _qkv_kernel(*refs, rope):
    if rope:
        (x_ref, mod_ref, g_ref, wq_ref, wk_ref, wv_ref, cos_t_ref, sin_t_ref,
         cos_ref, sa_ref, sb_ref, qt_ref, k_ref, vt_ref) = refs
    else:
        x_ref, mod_ref, g_ref, wq_ref, wk_ref, wv_ref, qt_ref, k_ref, vt_ref = refs
    m = mod_ref[0]
    h = _rms_mod(x_ref[0], g_ref[...], _mod_chunk(m, 1), _mod_chunk(m, 0)).astype(BF16)
    q_t = lax.dot_general(wq_ref[...], h, _NT, preferred_element_type=F32)
    k = jnp.dot(h, wk_ref[...], preferred_element_type=F32)
    v_t = lax.dot_general(wv_ref[...], h, _NT, preferred_element_type=F32).astype(BF16)
    n_blk = x_ref.shape[1] // BLOCK

    def put(ref, rows, val):
        for j in range(n_blk):
            ref[0, j, rows, :] = val[:, j * BLOCK:(j + 1) * BLOCK]

    put(vt_ref, slice(0, KV_DIM), v_t)
    scale = HEAD_DIM ** -0.5 * LOG2E
    if rope:
        f = ROPE_FREQS
        for head in range(N_HEADS):
            for axis in range(2):
                r0 = head * HEAD_DIM + axis * 2 * f
                cs = cos_t_ref[axis * f:(axis + 1) * f, :]
                sn = sin_t_ref[axis * f:(axis + 1) * f, :]
                x1 = q_t[r0:r0 + f, :]
                x2 = q_t[r0 + f:r0 + 2 * f, :]
                put(qt_ref, slice(r0, r0 + f), ((x1 * cs - x2 * sn) * scale).astype(BF16))
                put(qt_ref, slice(r0 + f, r0 + 2 * f), ((x2 * cs + x1 * sn) * scale).astype(BF16))
        for blk in range(KV_DIM // LANES):
            t = k[:, blk * LANES:(blk + 1) * LANES]
            t = (t * cos_ref[...] + pltpu.roll(t, LANES - f, 1) * sa_ref[...]
                 + pltpu.roll(t, f, 1) * sb_ref[...])
            k_ref[0, :, blk * LANES:(blk + 1) * LANES] = t.astype(BF16)
    else:
        put(qt_ref, slice(0, Q_DIM), (q_t * scale).astype(BF16))
        k_ref[0] = k.astype(BF16)


def _rope_tables(L):
    f32 = np.float32
    rows = L // GRID_W
    row = np.repeat(np.arange(rows, dtype=f32), GRID_W)
    col = np.tile(np.arange(GRID_W, dtype=f32), rows)
    inv = (f32(ROPE_BASE) ** (-np.arange(ROPE_FREQS, dtype=f32) / f32(ROPE_FREQS))).astype(f32)
    ang_r = (row[:, None] * inv).astype(f32)
    ang_c = (col[:, None] * inv).astype(f32)
    zero = np.zeros_like(ang_r)

    def lanes(r_first, r_second, c_first, c_second):
        head = np.concatenate([r_first, r_second, c_first, c_second], axis=-1)
        return np.tile(head, (1, LANES // HEAD_DIM)).astype(f32)

    cr, sr, cc, sc = np.cos(ang_r), np.sin(ang_r), np.cos(ang_c), np.sin(ang_c)
    cos_t = np.ascontiguousarray(np.concatenate([cr, cc], axis=1).T).astype(f32)
    sin_t = np.ascontiguousarray(np.concatenate([sr, sc], axis=1).T).astype(f32)
    return (cos_t, sin_t, lanes(cr, cr, cc, cc), lanes(-sr, zero, -sc, zero),
            lanes(zero, sr, zero, sc))


def _qkv(x, mod, g, w_q_t, w_k, w_v_t, tables, mod_row):
    B, L, _ = x.shape
    tm = min(L, 1024)
    rope = tables is not None
    in_specs = [
        pl.BlockSpec((1, tm, D_MODEL), lambda b, i: (b, i, 0)),
        pl.BlockSpec((1, 1, 6 * D_MODEL), lambda b, i: (mod_row(b), 0, 0)),
        _const_spec((1, D_MODEL)),
        _const_spec(w_q_t.shape), _const_spec(w_k.shape), _const_spec(w_v_t.shape),
    ]
    args = [x, mod, g, w_q_t, w_k, w_v_t]
    if rope:
        in_specs += [pl.BlockSpec((2 * ROPE_FREQS, tm), lambda b, i: (0, i))] * 2
        in_specs += [pl.BlockSpec((tm, LANES), lambda b, i: (i, 0))] * 3
        args += list(tables)
    return pl.pallas_call(
        functools.partial(_qkv_kernel, rope=rope),
        grid=(B, L // tm),
        in_specs=in_specs,
        out_specs=[
            pl.BlockSpec((1, tm // BLOCK, Q_DIM, BLOCK), lambda b, i: (b, i, 0, 0)),
            pl.BlockSpec((1, tm, KV_DIM), lambda b, i: (b, i, 0)),
            pl.BlockSpec((1, tm // BLOCK, KV_DIM, BLOCK), lambda b, i: (b, i, 0, 0)),
        ],
        out_shape=[
            jax.ShapeDtypeStruct((B, L // BLOCK, Q_DIM, BLOCK), BF16),
            jax.ShapeDtypeStruct((B, L, KV_DIM), BF16),
            jax.ShapeDtypeStruct((B, L // BLOCK, KV_DIM, BLOCK), BF16),
        ],
        compiler_params=pltpu.CompilerParams(
            dimension_semantics=("arbitrary", "arbitrary"), vmem_limit_bytes=VMEM_LIMIT),
        name="qkv_rope" if rope else "qkv_ctx",
    )(*args)


def _attn_kernel(*refs, window, nq, n_ctx):
    if window:
        sink_ref, qt_ref, k_ref, v_ref, kx_ref, vx_ref, ot_ref, kbuf, vtbuf = refs
    else:
        sink_ref, qt_ref, kx_ref, vx_ref, ot_ref, kbuf, vtbuf = refs
    width = HEADS_PER_DOT * BLOCK
    kbuf[0:n_ctx, :] = kx_ref[0]
    for j in range(n_ctx // BLOCK):
        vtbuf[:, j * BLOCK:(j + 1) * BLOCK] = vx_ref[0, j]
    if window:
        n = pl.program_id(1)
        for j, blk in enumerate((jnp.maximum(n - 1, 0), n, jnp.minimum(n + 1, nq - 1))):
            rows = pl.ds(pl.multiple_of(blk * BLOCK, BLOCK), BLOCK)
            kbuf[n_ctx + j * BLOCK:n_ctx + (j + 1) * BLOCK, :] = k_ref[0, rows, :]
            vtbuf[:, n_ctx + j * BLOCK:n_ctx + (j + 1) * BLOCK] = v_ref[0, blk]
        kj = lax.broadcasted_iota(jnp.int32, (BLOCK, width), 0)
        qi = lax.broadcasted_iota(jnp.int32, (BLOCK, width), 1) % BLOCK
        bias_prev = jnp.where((kj >= qi) & (n > 0), 0.0, NEG_INF).astype(F32)
        bias_next = jnp.where((kj <= qi) & (n < nq - 1), 0.0, NEG_INF).astype(F32)
    lane = lax.broadcasted_iota(jnp.int32, (1, width), 1)
    ones_rows = jnp.ones((16, kbuf.shape[0]), BF16)
    for h0 in range(0, N_HEADS, HEADS_PER_DOT):
        cols = []
        for h in range(h0, h0 + HEADS_PER_DOT):
            g = h // Q_PER_KV
            parts = [qt_ref[0, 0, h * HEAD_DIM:(h + 1) * HEAD_DIM, :]]
            if g > 0:
                parts.insert(0, jnp.zeros((g * HEAD_DIM, BLOCK), BF16))
            if g < N_KV_HEADS - 1:
                parts.append(jnp.zeros(((N_KV_HEADS - 1 - g) * HEAD_DIM, BLOCK), BF16))
            cols.append(jnp.concatenate(parts, axis=0) if len(parts) > 1 else parts[0])
        q_exp = jnp.concatenate(cols, axis=1) if len(cols) > 1 else cols[0]
        sink = jnp.full((1, width), sink_ref[h0], F32)
        for r in range(1, HEADS_PER_DOT):
            sink = jnp.where(lane >= r * BLOCK, sink_ref[h0 + r], sink)
        sink = sink * LOG2E
        s = jnp.dot(kbuf[...], q_exp, preferred_element_type=F32)
        if window:
            s = jnp.concatenate([
                s[:n_ctx], s[n_ctx:n_ctx + BLOCK] + bias_prev,
                s[n_ctx + BLOCK:n_ctx + 2 * BLOCK], s[n_ctx + 2 * BLOCK:] + bias_next], axis=0)
        mx = jnp.maximum(jnp.max(s, axis=0, keepdims=True), sink)
        p = jnp.exp2(s - mx).astype(BF16)
        sink_p = jnp.exp2(sink - mx)
        gw = Q_PER_KV * BLOCK
        for g0 in range(0, HEADS_PER_DOT // Q_PER_KV):
            g = h0 // Q_PER_KV + g0
            gl = slice(g0 * gw, (g0 + 1) * gw)
            v_ext = jnp.concatenate([vtbuf[g * HEAD_DIM:(g + 1) * HEAD_DIM, :], ones_rows], axis=0)
            pv = jnp.dot(v_ext, p[:, gl], preferred_element_type=F32)
            o_t = pv[:HEAD_DIM, :] / (pv[HEAD_DIM:HEAD_DIM + 1, :] + sink_p[:, gl])
            for r in range(Q_PER_KV):
                h = g * Q_PER_KV + r
                ot_ref[0, 0, h * HEAD_DIM:(h + 1) * HEAD_DIM, :] = (
                    o_t[:, r * BLOCK:(r + 1) * BLOCK].astype(BF16))


def _attn(q_t, k, v_t, kx, vx_t, sinks):
    B, nq = q_t.shape[:2]
    C = kx.shape[1]
    window = k is not None
    smem = pl.BlockSpec(memory_space=pltpu.SMEM)
    q_spec = pl.BlockSpec((1, 1, Q_DIM, BLOCK), lambda b, n: (b, n, 0, 0))
    kx_spec = pl.BlockSpec((1, C, KV_DIM), lambda b, n: (b, 0, 0))
    vx_spec = pl.BlockSpec((1, C // BLOCK, KV_DIM, BLOCK), lambda b, n: (b, 0, 0, 0))
    if window:
        k_spec = pl.BlockSpec((1, nq * BLOCK, KV_DIM), lambda b, n: (b, 0, 0))
        v_spec = pl.BlockSpec((1, nq, KV_DIM, BLOCK), lambda b, n: (b, 0, 0, 0))
        in_specs = [smem, q_spec, k_spec, v_spec, kx_spec, vx_spec]
        args = (sinks, q_t, k, v_t, kx, vx_t)
        n_keys = C + 3 * BLOCK
    else:
        in_specs = [smem, q_spec, kx_spec, vx_spec]
        args = (sinks, q_t, kx, vx_t)
        n_keys = C
    return pl.pallas_call(
        functools.partial(_attn_kernel, window=window, nq=nq, n_ctx=C),
        grid=(B, nq),
        in_specs=in_specs,
        out_specs=q_spec,
        out_shape=jax.ShapeDtypeStruct((B, nq, Q_DIM, BLOCK), BF16),
        scratch_shapes=[pltpu.VMEM((n_keys, KV_DIM), BF16), pltpu.VMEM((KV_DIM, n_keys), BF16)],
        compiler_params=pltpu.CompilerParams(
            dimension_semantics=("arbitrary", "arbitrary"), vmem_limit_bytes=VMEM_LIMIT),
        name="attn_window" if window else "attn_ctx",
    )(*args)


def _mix_ffn_kernel(*refs, ssm, final):
    if ssm:
        (x_ref, mod_ref, yf_ref, yb_ref, z_ref, xs_ref, dsk_ref, ng_ref,
         wo_ref, gf_ref, wg_ref, wu_ref, wd_ref) = refs[:13]
        rest = refs[13:]
    else:
        (x_ref, mod_ref, o_ref_in, wo_ref, gf_ref, wg_ref, wu_ref, wd_ref) = refs[:8]
        rest = refs[8:]
    if final:
        fg_ref, out_ref = rest
    else:
        (out_ref,) = rest
    m = mod_ref[0]
    if ssm:
        mix = None
        for g in range(SSM_GROUPS):
            gs = slice(g * GROUP_DIM, (g + 1) * GROUP_DIM)
            yg = ((yf_ref[0, :, gs].astype(F32) + yb_ref[0, :, gs].astype(F32)
                   + dsk_ref[:, gs] * xs_ref[0, :, gs].astype(F32))
                  * _silu(z_ref[0, :, gs].astype(F32)))
            ms = jnp.mean(yg * yg, axis=-1, keepdims=True)
            part = (yg * lax.rsqrt(ms + EPS) * ng_ref[:, gs]).astype(BF16)
            term = jnp.dot(part, wo_ref[gs, :], preferred_element_type=F32)
            mix = term if mix is None else mix + term
    else:
        o_t = jnp.concatenate([o_ref_in[0, j] for j in range(o_ref_in.shape[1])], axis=1)
        mix = lax.dot_general(o_t, wo_ref[...], _TN, preferred_element_type=F32)
    x1 = x_ref[0] + _mod_chunk(m, 2) * mix
    h2 = _rms_mod(x1, gf_ref[...], _mod_chunk(m, 4), _mod_chunk(m, 3)).astype(BF16)
    gate = jnp.dot(h2, wg_ref[...], preferred_element_type=F32)
    up = jnp.dot(h2, wu_ref[...], preferred_element_type=F32)
    act = (_silu(gate) * up).astype(BF16)
    x2 = x1 + _mod_chunk(m, 5) * jnp.dot(act, wd_ref[...], preferred_element_type=F32)
    if final:
        ms = jnp.mean(x2 * x2, axis=-1, keepdims=True)
        x2 = x2 * lax.rsqrt(ms + EPS) * fg_ref[...]
    out_ref[0] = x2


def _mix_ffn(x, mod, mod_row, mixer_in, w_mix, g_ffn, w_gate, w_up, w_down, layer,
             ssm_norm_g=None, d_skip=None, final_g=None):
    B, L, _ = x.shape
    tm = min(L, 512)
    ssm = ssm_norm_g is not None
    final = final_g is not None
    row = lambda width: pl.BlockSpec((1, tm, width), lambda b, i: (b, i, 0))
    in_specs = [row(D_MODEL), pl.BlockSpec((1, 1, 6 * D_MODEL), lambda b, i: (mod_row(b), 0, 0))]
    args = [x, mod]
    if ssm:
        in_specs += [row(D_INNER)] * 4 + [_const_spec((1, D_INNER))] * 2
        args += list(mixer_in) + [d_skip, ssm_norm_g]
    else:
        in_specs += [pl.BlockSpec((1, tm // BLOCK, Q_DIM, BLOCK), lambda b, i: (b, i, 0, 0))]
        args += [mixer_in]
    in_specs += [_layer_spec(w_mix, 0), _const_spec((1, D_MODEL)), _layer_spec(w_gate, layer),
                 _layer_spec(w_up, layer), _layer_spec(w_down, layer)]
    args += [w_mix, g_ffn, w_gate, w_up, w_down]
    if final:
        in_specs += [_const_spec((1, D_MODEL))]
        args += [final_g]
    return pl.pallas_call(
        functools.partial(_mix_ffn_kernel, ssm=ssm, final=final),
        grid=(B, L // tm),
        in_specs=in_specs,
        out_specs=row(D_MODEL),
        out_shape=jax.ShapeDtypeStruct((B, L, D_MODEL), F32),
        compiler_params=pltpu.CompilerParams(
            dimension_semantics=("arbitrary", "arbitrary"), vmem_limit_bytes=VMEM_LIMIT_FFN),
        name="ssm_out_ffn" if ssm else "attn_out_ffn",
    )(*args)


def _inproj_kernel(x_ref, xp_ref, xn_ref, mod_ref, g_ref, w_ref, wdt_ref, dtb_ref, cw_ref, cb_ref,
                   z_ref, xa_ref, dt_ref, *, n_tiles):
    i = pl.program_id(1)
    m = mod_ref[0]
    g, scale, shift = g_ref[...], _mod_chunk(m, 1), _mod_chunk(m, 0)
    tm = x_ref.shape[1]
    h = _rms_mod(x_ref[0], g, scale, shift).astype(BF16)
    x_ext = jnp.concatenate([xp_ref[0], x_ref[0], xn_ref[0]], axis=0)
    h_ext = _rms_mod(x_ext, g, scale, shift).astype(BF16)
    z_ref[0] = jnp.dot(h, w_ref[:, :D_INNER], preferred_element_type=F32).astype(BF16)
    raw = jnp.dot(h, wdt_ref[...], preferred_element_type=F32) + dtb_ref[...]
    dt_ref[0] = jnp.maximum(raw, 0.0) + jnp.log1p(jnp.exp(-jnp.abs(raw)))
    keep_prev = jnp.where(i > 0, 1.0, 0.0)
    keep_next = jnp.where(i < n_tiles - 1, 1.0, 0.0)
    cstep = 512
    for j in range(CONV_DIM // cstep):
        sl = slice(j * cstep, (j + 1) * cstep)
        u = jnp.dot(h_ext, w_ref[:, D_INNER + j * cstep:D_INNER + (j + 1) * cstep],
                    preferred_element_type=F32)
        u = jnp.concatenate([u[0:SUBLANES] * keep_prev, u[SUBLANES:SUBLANES + tm],
                             u[SUBLANES + tm:] * keep_next], axis=0)
        rows = tm + 2 * SUBLANES
        below = pltpu.roll(u, 1, 0)[SUBLANES:SUBLANES + tm]
        above = pltpu.roll(u, rows - 1, 0)[SUBLANES:SUBLANES + tm]
        conv = (below * cw_ref[0:1, sl] + u[SUBLANES:SUBLANES + tm] * cw_ref[1:2, sl]
                + above * cw_ref[2:3, sl] + cb_ref[:, sl])
        xa_ref[0, :, sl] = _silu(conv).astype(BF16)


def _inproj(x, mod, mod_row, *, g, w_in, w_dt, dt_bias, conv_w, conv_b):
    B, L, _ = x.shape
    tm = min(L, 256)
    n_tiles = L // tm
    per = tm // SUBLANES
    row = lambda width: pl.BlockSpec((1, tm, width), lambda b, i: (b, i, 0))
    return pl.pallas_call(
        functools.partial(_inproj_kernel, n_tiles=n_tiles),
        grid=(B, n_tiles),
        in_specs=[row(D_MODEL),
                  pl.BlockSpec((1, SUBLANES, D_MODEL), lambda b, i: (b, jnp.maximum(i * per - 1, 0), 0)),
                  pl.BlockSpec((1, SUBLANES, D_MODEL),
                               lambda b, i: (b, jnp.minimum((i + 1) * per, L // SUBLANES - 1), 0)),
                  pl.BlockSpec((1, 1, 6 * D_MODEL), lambda b, i: (mod_row(b), 0, 0)),
                  _const_spec((1, D_MODEL)), _const_spec(w_in.shape), _const_spec(w_dt.shape),
                  _const_spec((1, 2 * LANES)), _const_spec((8, CONV_DIM)), _const_spec((1, CONV_DIM))],
        out_specs=[row(D_INNER), row(CONV_DIM), row(2 * LANES)],
        out_shape=[jax.ShapeDtypeStruct((B, L, D_INNER), BF16),
                   jax.ShapeDtypeStruct((B, L, CONV_DIM), BF16),
                   jax.ShapeDtypeStruct((B, L, 2 * LANES), F32)],
        compiler_params=pltpu.CompilerParams(
            dimension_semantics=("arbitrary", "arbitrary"), vmem_limit_bytes=VMEM_LIMIT),
        name="ssm_inproj",
    )(x, x, x, mod, g, w_in, w_dt, dt_bias, conv_w, conv_b)


def _split3(v):
    hi = v.astype(BF16)
    r1 = v - hi.astype(F32)
    mid = r1.astype(BF16)
    lo = (r1 - mid.astype(F32)).astype(BF16)
    return hi, mid, lo


def _ssd_kernel(*refs, reverse, nc, has_init, want_y):
    (xa_ref, dt_ref, alog_ref, e64_ref), rest = refs[:4], refs[4:]
    if has_init:
        h0_ref, rest = rest[0], rest[1:]
    if want_y:
        y_ref, rest = rest[0], rest[1:]
    hfin_ref, state, ce, tbuf = rest
    c = pl.program_id(1)
    last = 0 if reverse else CHUNK - 1

    @pl.when(c == 0)
    def _():
        state[...] = h0_ref[0] if has_init else jnp.zeros(state.shape, F32)

    lane = lax.broadcasted_iota(jnp.int32, (CHUNK, LANES), 1)
    row = lax.broadcasted_iota(jnp.int32, (CHUNK, LANES), 0)
    lane_valid = (lane, (lane >= row) if reverse else (lane <= row))
    _prepare_decay(dt_ref, alog_ref, e64_ref, ce, tbuf, lane_valid, last)
    _scan_chunk(xa_ref, y_ref if want_y else None, state, ce, tbuf, lane_valid, last)

    @pl.when(c == nc - 1)
    def _():
        hfin_ref[0] = state[...]


def _prepare_decay(dt_ref, alog_ref, e64_ref, ce, tbuf, lane_valid, last):
    lane, valid = lane_valid
    dt = jnp.where(lane < SSM_HEADS, dt_ref[0], 0.0)
    a = dt * (-LOG2E * jnp.exp(alog_ref[0]))
    tri = jnp.where(valid, 1.0, 0.0).astype(BF16)
    a_hi, a_mid, a_lo = _split3(a)
    cum = (jnp.dot(tri, a_hi, preferred_element_type=F32)
           + jnp.dot(tri, a_mid, preferred_element_type=F32)
           + jnp.dot(tri, a_lo, preferred_element_type=F32))
    c_hi, c_mid, c_lo = _split3(cum)
    pieces = (c_hi.astype(F32) + pltpu.roll(c_mid.astype(F32), SSM_HEADS, 1)
              + pltpu.roll(c_lo.astype(F32), 2 * SSM_HEADS, 1)).astype(BF16)
    ce[...] = jnp.dot(pieces, e64_ref[...], preferred_element_type=F32)
    tb = (cum + pltpu.roll(dt, SSM_HEADS, 1)).T
    cum_t = tb[0:SSM_HEADS, :]
    dt_t = tb[SSM_HEADS:2 * SSM_HEADS, :]
    tbuf[0:SSM_HEADS, :] = cum_t - jnp.log2(dt_t)
    tbuf[SSM_HEADS:2 * SSM_HEADS, :] = jnp.exp2(cum_t[:, last:last + 1] - cum_t) * dt_t


def _scan_chunk(xa_ref, y_ref, state, ce, tbuf, lane_valid, last):
    lane, valid = lane_valid
    want_y = y_ref is not None
    lane_g = lax.broadcasted_iota(jnp.int32, (CHUNK, GROUP_DIM), 1)
    head_mask = [jnp.where((lane_g >= hl * SSM_HEAD_DIM) & (lane_g < (hl + 1) * SSM_HEAD_DIM),
                           1.0, 0.0).astype(BF16) for hl in range(HEADS_PER_GROUP)]
    low_half = lane < SSM_HEAD_DIM
    for g in range(SSM_GROUPS):
        gs = slice(g * GROUP_DIM, (g + 1) * GROUP_DIM)
        b_g = xa_ref[0, :, D_INNER + g * D_STATE:D_INNER + (g + 1) * D_STATE]
        c_g = xa_ref[0, :, D_INNER + BC_DIM + g * D_STATE:D_INNER + BC_DIM + (g + 1) * D_STATE]
        b_t = b_g.astype(F32).T
        x_g = xa_ref[0, :, gs]
        if want_y:
            cb_mat = lax.dot_general(c_g, b_g, _NT, preferred_element_type=F32)
            y_g = (jnp.dot(c_g, state[g].astype(BF16), preferred_element_type=F32)
                   * jnp.exp2(ce[:, gs]))
        st = jnp.zeros((D_STATE, GROUP_DIM), F32)
        for hp in range(HEADS_PER_GROUP // 2):
            pair = g * (HEADS_PER_GROUP // 2) + hp
            lhs_y, lhs_s, rhs = [], [], []
            if want_y:
                ce_pair = ce[:, pair * LANES:(pair + 1) * LANES]
                swapped = pltpu.roll(ce_pair, SSM_HEAD_DIM, 1)
            for hh in range(2):
                hl = 2 * hp + hh
                h = g * HEADS_PER_GROUP + hl
                if want_y:
                    own = low_half if hh == 0 else jnp.logical_not(low_half)
                    cum_i = jnp.where(own, ce_pair, swapped)
                    dec = jnp.exp2(jnp.where(valid, cum_i - tbuf[h:h + 1, :], NEG_INF))
                    lhs_y.append((dec * cb_mat).astype(BF16))
                lhs_s.append((b_t * tbuf[SSM_HEADS + h:SSM_HEADS + h + 1, :]).astype(BF16))
                rhs.append(x_g * head_mask[hl])
            rhs = jnp.concatenate(rhs, axis=0)
            if want_y:
                y_g = y_g + jnp.dot(jnp.concatenate(lhs_y, axis=1), rhs, preferred_element_type=F32)
            st = st + jnp.dot(jnp.concatenate(lhs_s, axis=1), rhs, preferred_element_type=F32)
        if want_y:
            y_ref[0, :, gs] = y_g.astype(BF16)
        state[g] = state[g] * jnp.exp2(ce[last:last + 1, gs]) + st


def _ssd(xact, dt, a_log, e64, h0, direction, want_y):
    B, L, _ = xact.shape
    nc = L // CHUNK
    reverse = direction == 1
    has_init = h0 is not None
    chunk = (lambda c: nc - 1 - c) if reverse else (lambda c: c)
    st_shape = (SSM_GROUPS, D_STATE, GROUP_DIM)
    st_spec = pl.BlockSpec((1,) + st_shape, lambda b, c: (b, 0, 0, 0))
    in_specs = [
        pl.BlockSpec((1, CHUNK, CONV_DIM), lambda b, c: (b, chunk(c), 0)),
        pl.BlockSpec((1, CHUNK, LANES), lambda b, c: (b, chunk(c), direction)),
        pl.BlockSpec((1, 1, LANES), lambda b, c: (direction, 0, 0)),
        _const_spec(e64.shape),
    ]
    args = [xact, dt, a_log, e64]
    if has_init:
        in_specs.append(st_spec)
        args.append(h0)
    out_specs, out_shape = [], []
    if want_y:
        out_specs.append(pl.BlockSpec((1, CHUNK, D_INNER), lambda b, c: (b, chunk(c), 0)))
        out_shape.append(jax.ShapeDtypeStruct((B, L, D_INNER), BF16))
    out_specs.append(st_spec)
    out_shape.append(jax.ShapeDtypeStruct((B,) + st_shape, F32))
    scratch = [pltpu.VMEM(st_shape, F32), pltpu.VMEM((CHUNK, D_INNER), F32),
               pltpu.VMEM((2 * SSM_HEADS, CHUNK), F32)]
    return pl.pallas_call(
        functools.partial(_ssd_kernel, reverse=reverse, nc=nc, has_init=has_init, want_y=want_y),
        grid=(B, nc),
        in_specs=in_specs,
        out_specs=out_specs,
        out_shape=out_shape,
        scratch_shapes=scratch,
        compiler_params=pltpu.CompilerParams(
            dimension_semantics=("arbitrary", "arbitrary"), vmem_limit_bytes=VMEM_LIMIT),
        name="ssd_bwd" if reverse else "ssd_fwd",
    )(*args)


def _expand_matrix(width):
    k = np.arange(LANES)[:, None]
    col_head = (np.arange(SSM_HEADS * width) // width)[None, :]
    return jnp.asarray((k % SSM_HEADS == col_head) & (k < 3 * SSM_HEADS), dtype=BF16)


def kernel(x, c, ctx, c_ctx, ada_w, ada_b, norm_mix_g, norm_ffn_g, attn_w_qkv, attn_w_o, attn_sinks,
           ssm_w_in, ssm_conv_w, ssm_conv_b, ssm_dt_bias, ssm_A_log, ssm_D, ssm_norm_g, ssm_w_out,
           ffn_w_gate, ffn_w_up, ffn_w_down, final_norm_g):
    B, L, _ = x.shape
    assert x.shape == (B, L, D_MODEL) and B <= CTX_ROW and L % 1024 == 0
    bf = lambda w: w.astype(BF16)
    row2 = lambda v: v.reshape(1, -1)

    cc = jnp.zeros((MOD_ROWS, D_MODEL), F32).at[:B].set(c).at[CTX_ROW].set(c_ctx)
    mod = _adaln(cc, ada_w, ada_b)
    mods = [mod[i].reshape(MOD_ROWS, 1, 6 * D_MODEL) for i in range(DEPTH)]
    x_row = lambda b: b
    ctx_row = lambda b: CTX_ROW

    w_qkv = attn_w_qkv[0]
    w_q_t = bf(w_qkv[:, :Q_DIM].T)
    w_k = bf(w_qkv[:, Q_DIM:Q_DIM + KV_DIM])
    w_v_t = bf(w_qkv[:, Q_DIM + KV_DIM:].T)
    g_mix = row2(norm_mix_g[0])
    q, k, v = _qkv(x, mods[0], g_mix, w_q_t, w_k, w_v_t, _rope_tables(L), x_row)
    qc, kc, vc = _qkv(ctx, mods[0], g_mix, w_q_t, w_k, w_v_t, None, ctx_row)
    sinks = attn_sinks[0].astype(F32)
    o = _attn(q, k, v, kc, vc, sinks)
    oc = _attn(qc, None, None, kc, vc, sinks)
    w_gate, w_up, w_down = bf(ffn_w_gate), bf(ffn_w_up), bf(ffn_w_down)
    ffn0 = (bf(attn_w_o), row2(norm_ffn_g[0]), w_gate, w_up, w_down, 0)
    x = _mix_ffn(x, mods[0], x_row, o, *ffn0)
    ctx = _mix_ffn(ctx, mods[0], ctx_row, oc, *ffn0)

    w_in = ssm_w_in[0]
    w_dt_raw = w_in[:, D_INNER + CONV_DIM:]
    w_dt = jnp.zeros((D_MODEL, 2 * LANES), F32)
    w_dt = w_dt.at[:, :SSM_HEADS].set(w_dt_raw[:, :SSM_HEADS])
    w_dt = bf(w_dt.at[:, LANES:LANES + SSM_HEADS].set(w_dt_raw[:, SSM_HEADS:]))
    dt_bias = jnp.zeros((2, LANES), F32).at[:, :SSM_HEADS].set(ssm_dt_bias[0]).reshape(1, 2 * LANES)
    a_log = jnp.zeros((2, 1, LANES), F32).at[:, 0, :SSM_HEADS].set(ssm_A_log[0])
    conv_w = jnp.zeros((8, CONV_DIM), F32).at[:3].set(ssm_conv_w[0])
    conv_b = row2(ssm_conv_b[0])
    d_skip = row2(jnp.repeat(ssm_D[0], SSM_HEAD_DIM))
    e64 = _expand_matrix(SSM_HEAD_DIM)
    g_mix1 = row2(norm_mix_g[1])

    inproj = functools.partial(_inproj, g=g_mix1, w_in=bf(w_in), w_dt=w_dt, dt_bias=dt_bias,
                               conv_w=conv_w, conv_b=conv_b)
    z, xact, dt = inproj(x, mods[1], x_row)
    _, xact_c, dt_c = inproj(ctx, mods[1], ctx_row)
    ssd = functools.partial(_ssd, a_log=a_log, e64=e64)
    (h_f,) = ssd(xact_c, dt_c, h0=None, direction=0, want_y=False)
    (h_b,) = ssd(xact_c, dt_c, h0=None, direction=1, want_y=False)
    y_f, _ = ssd(xact, dt, h0=h_f, direction=0, want_y=True)
    y_b, _ = ssd(xact, dt, h0=h_b, direction=1, want_y=True)
    return _mix_ffn(x, mods[1], x_row, (y_f, y_b, z, xact), bf(ssm_w_out), row2(norm_ffn_g[1]),
                    w_gate, w_up, w_down, 1,
                    ssm_norm_g=row2(ssm_norm_g[0]), d_skip=d_skip, final_g=row2(final_norm_g))
```

```python
import functools
import math

import numpy as np

import jax
import jax.numpy as jnp
from jax import lax
from jax.experimental import pallas as pl
from jax.experimental.pallas import tpu as pltpu

F32 = jnp.float32
BF16 = jnp.bfloat16

D_MODEL = 1024
DEPTH = 2
GRID_W = 64
EPS = 1e-6

HEAD_DIM = 64
N_HEADS = D_MODEL // HEAD_DIM
N_KV_HEADS = N_HEADS // 4
Q_PER_KV = N_HEADS // N_KV_HEADS
Q_DIM = N_HEADS * HEAD_DIM
KV_DIM = N_KV_HEADS * HEAD_DIM
QKV_DIM = Q_DIM + 2 * KV_DIM
WINDOW = 128
BLOCK = 128
ROPE_FREQS = HEAD_DIM // 4
ROPE_BASE = 10000.0

D_INNER = 2 * D_MODEL
SSM_HEAD_DIM = 64
SSM_HEADS = D_INNER // SSM_HEAD_DIM
SSM_GROUPS = 8
HEADS_PER_GROUP = SSM_HEADS // SSM_GROUPS
GROUP_DIM = D_INNER // SSM_GROUPS
D_STATE = 128
CHUNK = 128
BC_DIM = SSM_GROUPS * D_STATE
CONV_DIM = D_INNER + 2 * BC_DIM
D_FF = ((8 * D_MODEL // 3 + 255) // 256) * 256

LANES = 128
SUBLANES = 8
MOD_ROWS = 8
CTX_ROW = 4
VMEM_LIMIT = 48 * 1024 * 1024
VMEM_LIMIT_FFN = 56 * 1024 * 1024
NEG_INF = float("-inf")
LOG2E = math.log2(math.e)
HEADS_PER_DOT = N_HEADS


def _const_spec(shape):
    nd = len(shape)
    return pl.BlockSpec(shape, lambda *_: (0,) * nd, pipeline_mode=pl.Buffered(1))


def _layer_spec(stacked, layer):
    return pl.BlockSpec((None,) + stacked.shape[1:], lambda *_: (layer, 0, 0),
                        pipeline_mode=pl.Buffered(1))


def _silu(v):
    return v * jax.nn.sigmoid(v)


def _rms_mod(x, g, scale, shift):
    ms = jnp.mean(x * x, axis=-1, keepdims=True)
    return (x * lax.rsqrt(ms + EPS) * g) * (1.0 + scale) + shift


def _mod_chunk(m, i):
    return m[:, i * D_MODEL:(i + 1) * D_MODEL]


def _adaln_kernel(c_ref, w_ref, b_ref, o_ref):
    s = _silu(c_ref[...])
    o_ref[0] = jnp.dot(s, w_ref[0], preferred_element_type=F32) + b_ref[0]


def _adaln(cc, ada_w, ada_b):
    tn = 1536
    n = 6 * D_MODEL
    return pl.pallas_call(
        _adaln_kernel,
        grid=(DEPTH, n // tn),
        in_specs=[
            pl.BlockSpec((MOD_ROWS, D_MODEL), lambda i, j: (0, 0)),
            pl.BlockSpec((1, D_MODEL, tn), lambda i, j: (i, 0, j)),
            pl.BlockSpec((1, 1, tn), lambda i, j: (i, 0, j)),
        ],
        out_specs=pl.BlockSpec((1, MOD_ROWS, tn), lambda i, j: (i, 0, j)),
        out_shape=jax.ShapeDtypeStruct((DEPTH, MOD_ROWS, n), F32),
        compiler_params=pltpu.CompilerParams(
            dimension_semantics=("arbitrary", "arbitrary"), vmem_limit_bytes=VMEM_LIMIT),
        name="adaln",
    )(cc, ada_w, ada_b.reshape(DEPTH, 1, n))


_NT = (((1,), (1,)), ((), ()))
_TN = (((0,), (0,)), ((), ()))


def _qkv_kernel(*refs, rope):
    if rope:
        (x_ref, mod_ref, g_ref, wq_ref, wk_ref, wv_ref, cos_t_ref, sin_t_ref,
         cos_ref, sa_ref, sb_ref, qt_ref, k_ref, vt_ref) = refs
    else:
        x_ref, mod_ref, g_ref, wq_ref, wk_ref, wv_ref, qt_ref, k_ref, vt_ref = refs
    m = mod_ref[0]
    h = _rms_mod(x_ref[0], g_ref[...], _mod_chunk(m, 1), _mod_chunk(m, 0)).astype(BF16)
    q_t = lax.dot_general(wq_ref[...], h, _NT, preferred_element_type=F32)
    k = jnp.dot(h, wk_ref[...], preferred_element_type=F32)
    v_t = lax.dot_general(wv_ref[...], h, _NT, preferred_element_type=F32).astype(BF16)
    n_blk = x_ref.shape[1] // BLOCK

    def put(ref, rows, val):
        for j in range(n_blk):
            ref[0, j, rows, :] = val[:, j * BLOCK:(j + 1) * BLOCK]

    put(vt_ref, slice(0, KV_DIM), v_t)
    scale = HEAD_DIM ** -0.5 * LOG2E
    if rope:
        f = ROPE_FREQS
        for head in range(N_HEADS):
            for axis in range(2):
                r0 = head * HEAD_DIM + axis * 2 * f
                cs = cos_t_ref[axis * f:(axis + 1) * f, :]
                sn = sin_t_ref[axis * f:(axis + 1) * f, :]
                x1 = q_t[r0:r0 + f, :]
                x2 = q_t[r0 + f:r0 + 2 * f, :]
                put(qt_ref, slice(r0, r0 + f), ((x1 * cs - x2 * sn) * scale).astype(BF16))
                put(qt_ref, slice(r0 + f, r0 + 2 * f), ((x2 * cs + x1 * sn) * scale).astype(BF16))
        for blk in range(KV_DIM // LANES):
            t = k[:, blk * LANES:(blk + 1) * LANES]
            t = (t * cos_ref[...] + pltpu.roll(t, LANES - f, 1) * sa_ref[...]
                 + pltpu.roll(t, f, 1) * sb_ref[...])
            k_ref[0, :, blk * LANES:(blk + 1) * LANES] = t.astype(BF16)
    else:
        put(qt_ref, slice(0, Q_DIM), (q_t * scale).astype(BF16))
        k_ref[0] = k.astype(BF16)


def _rope_tables(L):
    f32 = np.float32
    rows = L // GRID_W
    row = np.repeat(np.arange(rows, dtype=f32), GRID_W)
    col = np.tile(np.arange(GRID_W, dtype=f32), rows)
    inv = (f32(ROPE_BASE) ** (-np.arange(ROPE_FREQS, dtype=f32) / f32(ROPE_FREQS))).astype(f32)
    ang_r = (row[:, None] * inv).astype(f32)
    ang_c = (col[:, None] * inv).astype(f32)
    zero = np.zeros_like(ang_r)

    def lanes(r_first, r_second, c_first, c_second):
        head = np.concatenate([r_first, r_second, c_first, c_second], axis=-1)
        return np.tile(head, (1, LANES // HEAD_DIM)).astype(f32)

    cr, sr, cc, sc = np.cos(ang_r), np.sin(ang_r), np.cos(ang_c), np.sin(ang_c)
    cos_t = np.ascontiguousarray(np.concatenate([cr, cc], axis=1).T).astype(f32)
    sin_t = np.ascontiguousarray(np.concatenate([sr, sc], axis=1).T).astype(f32)
    return (cos_t, sin_t, lanes(cr, cr, cc, cc), lanes(-sr, zero, -sc, zero),
            lanes(zero, sr, zero, sc))


def _qkv(x, mod, g, w_q_t, w_k, w_v_t, tables, mod_row):
    B, L, _ = x.shape
    tm = min(L, 2048)
    rope = tables is not None
    in_specs = [
        pl.BlockSpec((1, tm, D_MODEL), lambda b, i: (b, i, 0)),
        pl.BlockSpec((1, 1, 6 * D_MODEL), lambda b, i: (mod_row(b), 0, 0)),
        _const_spec((1, D_MODEL)),
        _const_spec(w_q_t.shape), _const_spec(w_k.shape), _const_spec(w_v_t.shape),
    ]
    args = [x, mod, g, w_q_t, w_k, w_v_t]
    if rope:
        in_specs += [pl.BlockSpec((2 * ROPE_FREQS, tm), lambda b, i: (0, i))] * 2
        in_specs += [pl.BlockSpec((tm, LANES), lambda b, i: (i, 0))] * 3
        args += list(tables)
    return pl.pallas_call(
        functools.partial(_qkv_kernel, rope=rope),
        grid=(B, L // tm),
        in_specs=in_specs,
        out_specs=[
            pl.BlockSpec((1, tm // BLOCK, Q_DIM, BLOCK), lambda b, i: (b, i, 0, 0)),
            pl.BlockSpec((1, tm, KV_DIM), lambda b, i: (b, i, 0)),
            pl.BlockSpec((1, tm // BLOCK, KV_DIM, BLOCK), lambda b, i: (b, i, 0, 0)),
        ],
        out_shape=[
            jax.ShapeDtypeStruct((B, L // BLOCK, Q_DIM, BLOCK), BF16),
            jax.ShapeDtypeStruct((B, L, KV_DIM), BF16),
            jax.ShapeDtypeStruct((B, L // BLOCK, KV_DIM, BLOCK), BF16),
        ],
        compiler_params=pltpu.CompilerParams(
            dimension_semantics=("arbitrary", "arbitrary"), vmem_limit_bytes=VMEM_LIMIT),
        name="qkv_rope" if rope else "qkv_ctx",
    )(*args)


def _attn_kernel(*refs, window, nq, n_ctx):
    if window:
        (sink_ref, qt_ref, kp_ref, kc_ref, kn_ref, vp_ref, vc_ref, vn_ref,
         kx_ref, vx_ref, ot_ref, kbuf, vtbuf) = refs
    else:
        sink_ref, qt_ref, kx_ref, vx_ref, ot_ref, kbuf, vtbuf = refs
    width = HEADS_PER_DOT * BLOCK
    kbuf[0:n_ctx, :] = kx_ref[0]
    for j in range(n_ctx // BLOCK):
        vtbuf[:, j * BLOCK:(j + 1) * BLOCK] = vx_ref[0, j]
    if window:
        for j, (kr, vr) in enumerate(((kp_ref, vp_ref), (kc_ref, vc_ref), (kn_ref, vn_ref))):
            kbuf[n_ctx + j * BLOCK:n_ctx + (j + 1) * BLOCK, :] = kr[0]
            vtbuf[:, n_ctx + j * BLOCK:n_ctx + (j + 1) * BLOCK] = vr[0, 0]
        n = pl.program_id(1)
        kj = lax.broadcasted_iota(jnp.int32, (BLOCK, width), 0)
        qi = lax.broadcasted_iota(jnp.int32, (BLOCK, width), 1) % BLOCK
        bias_prev = jnp.where((kj >= qi) & (n > 0), 0.0, NEG_INF).astype(F32)
        bias_next = jnp.where((kj <= qi) & (n < nq - 1), 0.0, NEG_INF).astype(F32)
    lane = lax.broadcasted_iota(jnp.int32, (1, width), 1)
    ones_rows = jnp.ones((16, kbuf.shape[0]), BF16)
    for h0 in range(0, N_HEADS, HEADS_PER_DOT):
        cols = []
        for h in range(h0, h0 + HEADS_PER_DOT):
            g = h // Q_PER_KV
            parts = [qt_ref[0, 0, h * HEAD_DIM:(h + 1) * HEAD_DIM, :]]
            if g > 0:
                parts.insert(0, jnp.zeros((g * HEAD_DIM, BLOCK), BF16))
            if g < N_KV_HEADS - 1:
                parts.append(jnp.zeros(((N_KV_HEADS - 1 - g) * HEAD_DIM, BLOCK), BF16))
            cols.append(jnp.concatenate(parts, axis=0) if len(parts) > 1 else parts[0])
        q_exp = jnp.concatenate(cols, axis=1) if len(cols) > 1 else cols[0]
        sink = jnp.full((1, width), sink_ref[h0], F32)
        for r in range(1, HEADS_PER_DOT):
            sink = jnp.where(lane >= r * BLOCK, sink_ref[h0 + r], sink)
        sink = sink * LOG2E
        s = jnp.dot(kbuf[...], q_exp, preferred_element_type=F32)
        if window:
            s = jnp.concatenate([
                s[:n_ctx], s[n_ctx:n_ctx + BLOCK] + bias_prev,
                s[n_ctx + BLOCK:n_ctx + 2 * BLOCK], s[n_ctx + 2 * BLOCK:] + bias_next], axis=0)
        mx = jnp.maximum(jnp.max(s, axis=0, keepdims=True), sink)
        p = jnp.exp2(s - mx).astype(BF16)
        sink_p = jnp.exp2(sink - mx)
        gw = Q_PER_KV * BLOCK
        for g0 in range(0, HEADS_PER_DOT // Q_PER_KV):
            g = h0 // Q_PER_KV + g0
            gl = slice(g0 * gw, (g0 + 1) * gw)
            v_ext = jnp.concatenate([vtbuf[g * HEAD_DIM:(g + 1) * HEAD_DIM, :], ones_rows], axis=0)
            pv = jnp.dot(v_ext, p[:, gl], preferred_element_type=F32)
            o_t = pv[:HEAD_DIM, :] / (pv[HEAD_DIM:HEAD_DIM + 1, :] + sink_p[:, gl])
            for r in range(Q_PER_KV):
                h = g * Q_PER_KV + r
                ot_ref[0, 0, h * HEAD_DIM:(h + 1) * HEAD_DIM, :] = (
                    o_t[:, r * BLOCK:(r + 1) * BLOCK].astype(BF16))


def _attn(q_t, k, v_t, kx, vx_t, sinks):
    B, nq = q_t.shape[:2]
    C = kx.shape[1]
    window = k is not None
    smem = pl.BlockSpec(memory_space=pltpu.SMEM)
    q_spec = pl.BlockSpec((1, 1, Q_DIM, BLOCK), lambda b, n: (b, n, 0, 0))
    kx_spec = pl.BlockSpec((1, C, KV_DIM), lambda b, n: (b, 0, 0))
    vx_spec = pl.BlockSpec((1, C // BLOCK, KV_DIM, BLOCK), lambda b, n: (b, 0, 0, 0))
    if window:
        prev = lambda n: jnp.maximum(n - 1, 0)
        nxt = lambda n: jnp.minimum(n + 1, nq - 1)
        k_spec = lambda f: pl.BlockSpec((1, BLOCK, KV_DIM), lambda b, n: (b, f(n), 0))
        v_spec = lambda f: pl.BlockSpec((1, 1, KV_DIM, BLOCK), lambda b, n: (b, f(n), 0, 0))
        same = lambda n: n
        in_specs = [smem, q_spec, k_spec(prev), k_spec(same), k_spec(nxt),
                    v_spec(prev), v_spec(same), v_spec(nxt), kx_spec, vx_spec]
        args = (sinks, q_t, k, k, k, v_t, v_t, v_t, kx, vx_t)
        n_keys = C + 3 * BLOCK
    else:
        in_specs = [smem, q_spec, kx_spec, vx_spec]
        args = (sinks, q_t, kx, vx_t)
        n_keys = C
    return pl.pallas_call(
        functools.partial(_attn_kernel, window=window, nq=nq, n_ctx=C),
        grid=(B, nq),
        in_specs=in_specs,
        out_specs=q_spec,
        out_shape=jax.ShapeDtypeStruct((B, nq, Q_DIM, BLOCK), BF16),
        scratch_shapes=[pltpu.VMEM((n_keys, KV_DIM), BF16), pltpu.VMEM((KV_DIM, n_keys), BF16)],
        compiler_params=pltpu.CompilerParams(
            dimension_semantics=("arbitrary", "arbitrary"), vmem_limit_bytes=VMEM_LIMIT),
        name="attn_window" if window else "attn_ctx",
    )(*args)


def _mix_ffn_kernel(*refs, ssm, final):
    if ssm:
        (x_ref, mod_ref, yf_ref, yb_ref, z_ref, xs_ref, dsk_ref, ng_ref,
         wo_ref, gf_ref, wg_ref, wu_ref, wd_ref) = refs[:13]
        rest = refs[13:]
    else:
        (x_ref, mod_ref, o_ref_in, wo_ref, gf_ref, wg_ref, wu_ref, wd_ref) = refs[:8]
        rest = refs[8:]
    if final:
        fg_ref, out_ref = rest
    else:
        (out_ref,) = rest
    m = mod_ref[0]
    if ssm:
        mix = None
        for g in range(SSM_GROUPS):
            gs = slice(g * GROUP_DIM, (g + 1) * GROUP_DIM)
            yg = ((yf_ref[0, :, gs].astype(F32) + yb_ref[0, :, gs].astype(F32)
                   + dsk_ref[:, gs] * xs_ref[0, :, gs].astype(F32))
                  * _silu(z_ref[0, :, gs].astype(F32)))
            ms = jnp.mean(yg * yg, axis=-1, keepdims=True)
            part = (yg * lax.rsqrt(ms + EPS) * ng_ref[:, gs]).astype(BF16)
            term = jnp.dot(part, wo_ref[gs, :], preferred_element_type=F32)
            mix = term if mix is None else mix + term
    else:
        o_t = jnp.concatenate([o_ref_in[0, j] for j in range(o_ref_in.shape[1])], axis=1)
        mix = lax.dot_general(o_t, wo_ref[...], _TN, preferred_element_type=F32)
    x1 = x_ref[0] + _mod_chunk(m, 2) * mix
    h2 = _rms_mod(x1, gf_ref[...], _mod_chunk(m, 4), _mod_chunk(m, 3)).astype(BF16)
    gate = jnp.dot(h2, wg_ref[...], preferred_element_type=F32)
    up = jnp.dot(h2, wu_ref[...], preferred_element_type=F32)
    act = (_silu(gate) * up).astype(BF16)
    x2 = x1 + _mod_chunk(m, 5) * jnp.dot(act, wd_ref[...], preferred_element_type=F32)
    if final:
        ms = jnp.mean(x2 * x2, axis=-1, keepdims=True)
        x2 = x2 * lax.rsqrt(ms + EPS) * fg_ref[...]
    out_ref[0] = x2


def _mix_ffn(x, mod, mod_row, mixer_in, w_mix, g_ffn, w_gate, w_up, w_down, layer,
             ssm_norm_g=None, d_skip=None, final_g=None):
    B, L, _ = x.shape
    ssm = ssm_norm_g is not None
    tm = min(L, 512 if ssm else 1024)
    final = final_g is not None
    row = lambda width: pl.BlockSpec((1, tm, width), lambda b, i: (b, i, 0))
    in_specs = [row(D_MODEL), pl.BlockSpec((1, 1, 6 * D_MODEL), lambda b, i: (mod_row(b), 0, 0))]
    args = [x, mod]
    if ssm:
        in_specs += [row(D_INNER)] * 4 + [_const_spec((1, D_INNER))] * 2
        args += list(mixer_in) + [d_skip, ssm_norm_g]
    else:
        in_specs += [pl.BlockSpec((1, tm // BLOCK, Q_DIM, BLOCK), lambda b, i: (b, i, 0, 0))]
        args += [mixer_in]
    in_specs += [_layer_spec(w_mix, 0), _const_spec((1, D_MODEL)), _layer_spec(w_gate, layer),
                 _layer_spec(w_up, layer), _layer_spec(w_down, layer)]
    args += [w_mix, g_ffn, w_gate, w_up, w_down]
    if final:
        in_specs += [_const_spec((1, D_MODEL))]
        args += [final_g]
    return pl.pallas_call(
        functools.partial(_mix_ffn_kernel, ssm=ssm, final=final),
        grid=(B, L // tm),
        in_specs=in_specs,
        out_specs=row(D_MODEL),
        out_shape=jax.ShapeDtypeStruct((B, L, D_MODEL), F32),
        compiler_params=pltpu.CompilerParams(
            dimension_semantics=("arbitrary", "arbitrary"), vmem_limit_bytes=VMEM_LIMIT_FFN),
        name="ssm_out_ffn" if ssm else "attn_out_ffn",
    )(*args)


def _inproj_kernel(x_ref, xp_ref, xn_ref, mod_ref, g_ref, w_ref, wdt_ref, dtb_ref, cw_ref, cb_ref,
                   z_ref, xa_ref, dt_ref, *, n_tiles):
    i = pl.program_id(1)
    m = mod_ref[0]
    g, scale, shift = g_ref[...], _mod_chunk(m, 1), _mod_chunk(m, 0)
    tm = x_ref.shape[1]
    h = _rms_mod(x_ref[0], g, scale, shift).astype(BF16)
    x_ext = jnp.concatenate([xp_ref[0], x_ref[0], xn_ref[0]], axis=0)
    h_ext = _rms_mod(x_ext, g, scale, shift).astype(BF16)
    z_ref[0] = jnp.dot(h, w_ref[:, :D_INNER], preferred_element_type=F32).astype(BF16)
    raw = jnp.dot(h, wdt_ref[...], preferred_element_type=F32) + dtb_ref[...]
    dt_ref[0] = jnp.maximum(raw, 0.0) + jnp.log1p(jnp.exp(-jnp.abs(raw)))
    keep_prev = jnp.where(i > 0, 1.0, 0.0)
    keep_next = jnp.where(i < n_tiles - 1, 1.0, 0.0)
    cstep = 512
    for j in range(CONV_DIM // cstep):
        sl = slice(j * cstep, (j + 1) * cstep)
        u = jnp.dot(h_ext, w_ref[:, D_INNER + j * cstep:D_INNER + (j + 1) * cstep],
                    preferred_element_type=F32)
        u = jnp.concatenate([u[0:SUBLANES] * keep_prev, u[SUBLANES:SUBLANES + tm],
                             u[SUBLANES + tm:] * keep_next], axis=0)
        rows = tm + 2 * SUBLANES
        below = pltpu.roll(u, 1, 0)[SUBLANES:SUBLANES + tm]
        above = pltpu.roll(u, rows - 1, 0)[SUBLANES:SUBLANES + tm]
        conv = (below * cw_ref[0:1, sl] + u[SUBLANES:SUBLANES + tm] * cw_ref[1:2, sl]
                + above * cw_ref[2:3, sl] + cb_ref[:, sl])
        xa_ref[0, :, sl] = _silu(conv).astype(BF16)


def _inproj(x, mod, mod_row, *, g, w_in, w_dt, dt_bias, conv_w, conv_b):
    B, L, _ = x.shape
    tm = min(L, 256)
    n_tiles = L // tm
    per = tm // SUBLANES
    row = lambda width: pl.BlockSpec((1, tm, width), lambda b, i: (b, i, 0))
    return pl.pallas_call(
        functools.partial(_inproj_kernel, n_tiles=n_tiles),
        grid=(B, n_tiles),
        in_specs=[row(D_MODEL),
                  pl.BlockSpec((1, SUBLANES, D_MODEL), lambda b, i: (b, jnp.maximum(i * per - 1, 0), 0)),
                  pl.BlockSpec((1, SUBLANES, D_MODEL),
                               lambda b, i: (b, jnp.minimum((i + 1) * per, L // SUBLANES - 1), 0)),
                  pl.BlockSpec((1, 1, 6 * D_MODEL), lambda b, i: (mod_row(b), 0, 0)),
                  _const_spec((1, D_MODEL)), _const_spec(w_in.shape), _const_spec(w_dt.shape),
                  _const_spec((1, 2 * LANES)), _const_spec((8, CONV_DIM)), _const_spec((1, CONV_DIM))],
        out_specs=[row(D_INNER), row(CONV_DIM), row(2 * LANES)],
        out_shape=[jax.ShapeDtypeStruct((B, L, D_INNER), BF16),
                   jax.ShapeDtypeStruct((B, L, CONV_DIM), BF16),
                   jax.ShapeDtypeStruct((B, L, 2 * LANES), F32)],
        compiler_params=pltpu.CompilerParams(
            dimension_semantics=("arbitrary", "arbitrary"), vmem_limit_bytes=VMEM_LIMIT),
        name="ssm_inproj",
    )(x, x, x, mod, g, w_in, w_dt, dt_bias, conv_w, conv_b)


def _split3(v):
    hi = v.astype(BF16)
    r1 = v - hi.astype(F32)
    mid = r1.astype(BF16)
    lo = (r1 - mid.astype(F32)).astype(BF16)
    return hi, mid, lo


def _ssd_kernel(*refs, reverse, nc, has_init, want_y):
    (xa_ref, dt_ref, alog_ref, e64_ref), rest = refs[:4], refs[4:]
    if has_init:
        h0_ref, rest = rest[0], rest[1:]
    if want_y:
        y_ref, rest = rest[0], rest[1:]
    hfin_ref, state, ce, tbuf = rest
    c = pl.program_id(1)
    last = 0 if reverse else CHUNK - 1

    @pl.when(c == 0)
    def _():
        state[...] = h0_ref[0] if has_init else jnp.zeros(state.shape, F32)

    lane = lax.broadcasted_iota(jnp.int32, (CHUNK, LANES), 1)
    row = lax.broadcasted_iota(jnp.int32, (CHUNK, LANES), 0)
    lane_valid = (lane, (lane >= row) if reverse else (lane <= row))
    _prepare_decay(dt_ref, alog_ref, e64_ref, ce, tbuf, lane_valid, last)
    _scan_chunk(xa_ref, y_ref if want_y else None, state, ce, tbuf, lane_valid, last)

    @pl.when(c == nc - 1)
    def _():
        hfin_ref[0] = state[...]


def _prepare_decay(dt_ref, alog_ref, e64_ref, ce, tbuf, lane_valid, last):
    lane, valid = lane_valid
    dt = jnp.where(lane < SSM_HEADS, dt_ref[0], 0.0)
    a = dt * (-LOG2E * jnp.exp(alog_ref[0]))
    tri = jnp.where(valid, 1.0, 0.0).astype(BF16)
    a_hi, a_mid, a_lo = _split3(a)
    cum = (jnp.dot(tri, a_hi, preferred_element_type=F32)
           + jnp.dot(tri, a_mid, preferred_element_type=F32)
           + jnp.dot(tri, a_lo, preferred_element_type=F32))
    c_hi, c_mid, c_lo = _split3(cum)
    pieces = (c_hi.astype(F32) + pltpu.roll(c_mid.astype(F32), SSM_HEADS, 1)
              + pltpu.roll(c_lo.astype(F32), 2 * SSM_HEADS, 1)).astype(BF16)
    ce[...] = jnp.dot(pieces, e64_ref[...], preferred_element_type=F32)
    tb = (cum + pltpu.roll(dt, SSM_HEADS, 1)).T
    cum_t = tb[0:SSM_HEADS, :]
    dt_t = tb[SSM_HEADS:2 * SSM_HEADS, :]
    tbuf[0:SSM_HEADS, :] = cum_t - jnp.log2(dt_t)
    tbuf[SSM_HEADS:2 * SSM_HEADS, :] = jnp.exp2(cum_t[:, last:last + 1] - cum_t) * dt_t


def _scan_chunk(xa_ref, y_ref, state, ce, tbuf, lane_valid, last):
    lane, valid = lane_valid
    want_y = y_ref is not None
    lane_g = lax.broadcasted_iota(jnp.int32, (CHUNK, GROUP_DIM), 1)
    head_mask = [jnp.where((lane_g >= hl * SSM_HEAD_DIM) & (lane_g < (hl + 1) * SSM_HEAD_DIM),
                           1.0, 0.0).astype(BF16) for hl in range(HEADS_PER_GROUP)]
    low_half = lane < SSM_HEAD_DIM
    for g in range(SSM_GROUPS):
        gs = slice(g * GROUP_DIM, (g + 1) * GROUP_DIM)
        b_g = xa_ref[0, :, D_INNER + g * D_STATE:D_INNER + (g + 1) * D_STATE]
        c_g = xa_ref[0, :, D_INNER + BC_DIM + g * D_STATE:D_INNER + BC_DIM + (g + 1) * D_STATE]
        b_t = b_g.astype(F32).T
        x_g = xa_ref[0, :, gs]
        if want_y:
            cb_mat = lax.dot_general(c_g, b_g, _NT, preferred_element_type=F32)
            y_g = (jnp.dot(c_g, state[g].astype(BF16), preferred_element_type=F32)
                   * jnp.exp2(ce[:, gs]))
        st = jnp.zeros((D_STATE, GROUP_DIM), F32)
        for hp in range(HEADS_PER_GROUP // 2):
            pair = g * (HEADS_PER_GROUP // 2) + hp
            lhs_y, lhs_s, rhs = [], [], []
            if want_y:
                ce_pair = ce[:, pair * LANES:(pair + 1) * LANES]
                swapped = pltpu.roll(ce_pair, SSM_HEAD_DIM, 1)
            for hh in range(2):
                hl = 2 * hp + hh
                h = g * HEADS_PER_GROUP + hl
                if want_y:
                    own = low_half if hh == 0 else jnp.logical_not(low_half)
                    cum_i = jnp.where(own, ce_pair, swapped)
                    dec = jnp.exp2(jnp.where(valid, cum_i - tbuf[h:h + 1, :], NEG_INF))
                    lhs_y.append((dec * cb_mat).astype(BF16))
                lhs_s.append((b_t * tbuf[SSM_HEADS + h:SSM_HEADS + h + 1, :]).astype(BF16))
                rhs.append(x_g * head_mask[hl])
            rhs = jnp.concatenate(rhs, axis=0)
            if want_y:
                y_g = y_g + jnp.dot(jnp.concatenate(lhs_y, axis=1), rhs, preferred_element_type=F32)
            st = st + jnp.dot(jnp.concatenate(lhs_s, axis=1), rhs, preferred_element_type=F32)
        if want_y:
            y_ref[0, :, gs] = y_g.astype(BF16)
        state[g] = state[g] * jnp.exp2(ce[last:last + 1, gs]) + st


def _ssd(xact, dt, a_log, e64, h0, direction, want_y):
    B, L, _ = xact.shape
    nc = L // CHUNK
    reverse = direction == 1
    has_init = h0 is not None
    chunk = (lambda c: nc - 1 - c) if reverse else (lambda c: c)
    st_shape = (SSM_GROUPS, D_STATE, GROUP_DIM)
    st_spec = pl.BlockSpec((1,) + st_shape, lambda b, c: (b, 0, 0, 0))
    in_specs = [
        pl.BlockSpec((1, CHUNK, CONV_DIM), lambda b, c: (b, chunk(c), 0)),
        pl.BlockSpec((1, CHUNK, LANES), lambda b, c: (b, chunk(c), direction)),
        pl.BlockSpec((1, 1, LANES), lambda b, c: (direction, 0, 0)),
        _const_spec(e64.shape),
    ]
    args = [xact, dt, a_log, e64]
    if has_init:
        in_specs.append(st_spec)
        args.append(h0)
    out_specs, out_shape = [], []
    if want_y:
        out_specs.append(pl.BlockSpec((1, CHUNK, D_INNER), lambda b, c: (b, chunk(c), 0)))
        out_shape.append(jax.ShapeDtypeStruct((B, L, D_INNER), BF16))
    out_specs.append(st_spec)
    out_shape.append(jax.ShapeDtypeStruct((B,) + st_shape, F32))
    scratch = [pltpu.VMEM(st_shape, F32), pltpu.VMEM((CHUNK, D_INNER), F32),
               pltpu.VMEM((2 * SSM_HEADS, CHUNK), F32)]
    return pl.pallas_call(
        functools.partial(_ssd_kernel, reverse=reverse, nc=nc, has_init=has_init, want_y=want_y),
        grid=(B, nc),
        in_specs=in_specs,
        out_specs=out_specs,
        out_shape=out_shape,
        scratch_shapes=scratch,
        compiler_params=pltpu.CompilerParams(
            dimension_semantics=("arbitrary", "arbitrary"), vmem_limit_bytes=VMEM_LIMIT),
        name="ssd_bwd" if reverse else "ssd_fwd",
    )(*args)


def _expand_matrix(width):
    k = np.arange(LANES)[:, None]
    col_head = (np.arange(SSM_HEADS * width) // width)[None, :]
    return jnp.asarray((k % SSM_HEADS == col_head) & (k < 3 * SSM_HEADS), dtype=BF16)


def kernel(x, c, ctx, c_ctx, ada_w, ada_b, norm_mix_g, norm_ffn_g, attn_w_qkv, attn_w_o, attn_sinks,
           ssm_w_in, ssm_conv_w, ssm_conv_b, ssm_dt_bias, ssm_A_log, ssm_D, ssm_norm_g, ssm_w_out,
           ffn_w_gate, ffn_w_up, ffn_w_down, final_norm_g):
    B, L, _ = x.shape
    assert x.shape == (B, L, D_MODEL) and B <= CTX_ROW and L % 2048 == 0
    bf = lambda w: w.astype(BF16)
    row2 = lambda v: v.reshape(1, -1)

    cc = jnp.zeros((MOD_ROWS, D_MODEL), F32).at[:B].set(c).at[CTX_ROW].set(c_ctx)
    mod = _adaln(cc, ada_w, ada_b)
    mods = [mod[i].reshape(MOD_ROWS, 1, 6 * D_MODEL) for i in range(DEPTH)]
    x_row = lambda b: b
    ctx_row = lambda b: CTX_ROW

    w_qkv = attn_w_qkv[0]
    w_q_t = bf(w_qkv[:, :Q_DIM].T)
    w_k = bf(w_qkv[:, Q_DIM:Q_DIM + KV_DIM])
    w_v_t = bf(w_qkv[:, Q_DIM + KV_DIM:].T)
    g_mix = row2(norm_mix_g[0])
    q, k, v = _qkv(x, mods[0], g_mix, w_q_t, w_k, w_v_t, _rope_tables(L), x_row)
    qc, kc, vc = _qkv(ctx, mods[0], g_mix, w_q_t, w_k, w_v_t, None, ctx_row)
    sinks = attn_sinks[0].astype(F32)
    o = _attn(q, k, v, kc, vc, sinks)
    oc = _attn(qc, None, None, kc, vc, sinks)
    w_gate, w_up, w_down = bf(ffn_w_gate), bf(ffn_w_up), bf(ffn_w_down)
    ffn0 = (bf(attn_w_o), row2(norm_ffn_g[0]), w_gate, w_up, w_down, 0)
    x = _mix_ffn(x, mods[0], x_row, o, *ffn0)
    ctx = _mix_ffn(ctx, mods[0], ctx_row, oc, *ffn0)

    w_in = ssm_w_in[0]
    w_dt_raw = w_in[:, D_INNER + CONV_DIM:]
    w_dt = jnp.zeros((D_MODEL, 2 * LANES), F32)
    w_dt = w_dt.at[:, :SSM_HEADS].set(w_dt_raw[:, :SSM_HEADS])
    w_dt = bf(w_dt.at[:, LANES:LANES + SSM_HEADS].set(w_dt_raw[:, SSM_HEADS:]))
    dt_bias = jnp.zeros((2, LANES), F32).at[:, :SSM_HEADS].set(ssm_dt_bias[0]).reshape(1, 2 * LANES)
    a_log = jnp.zeros((2, 1, LANES), F32).at[:, 0, :SSM_HEADS].set(ssm_A_log[0])
    conv_w = jnp.zeros((8, CONV_DIM), F32).at[:3].set(ssm_conv_w[0])
    conv_b = row2(ssm_conv_b[0])
    d_skip = row2(jnp.repeat(ssm_D[0], SSM_HEAD_DIM))
    e64 = _expand_matrix(SSM_HEAD_DIM)
    g_mix1 = row2(norm_mix_g[1])

    inproj = functools.partial(_inproj, g=g_mix1, w_in=bf(w_in), w_dt=w_dt, dt_bias=dt_bias,
                               conv_w=conv_w, conv_b=conv_b)
    z, xact, dt = inproj(x, mods[1], x_row)
    _, xact_c, dt_c = inproj(ctx, mods[1], ctx_row)
    ssd = functools.partial(_ssd, a_log=a_log, e64=e64)
    (h_f,) = ssd(xact_c, dt_c, h0=None, direction=0, want_y=False)
    (h_b,) = ssd(xact_c, dt_c, h0=None, direction=1, want_y=False)
    y_f, _ = ssd(xact, dt, h0=h_f, direction=0, want_y=True)
    y_b, _ = ssd(xact, dt, h0=h_b, direction=1, want_y=True)
    return _mix_ffn(x, mods[1], x_row, (y_f, y_b, z, xact), bf(ssm_w_out), row2(norm_ffn_g[1]),
                    w_gate, w_up, w_down, 1,
                    ssm_norm_g=row2(ssm_norm_g[0]), d_skip=d_skip, final_g=row2(final_norm_g))
```

```python
import functools
import math

import numpy as np

import jax
import jax.numpy as jnp
from jax import lax
from jax.experimental import pallas as pl
from jax.experimental.pallas import tpu as pltpu

F32 = jnp.float32
BF16 = jnp.bfloat16

D_MODEL = 1024
DEPTH = 2
GRID_W = 64
EPS = 1e-6

HEAD_DIM = 64
N_HEADS = D_MODEL // HEAD_DIM
N_KV_HEADS = N_HEADS // 4
Q_PER_KV = N_HEADS // N_KV_HEADS
Q_DIM = N_HEADS * HEAD_DIM
KV_DIM = N_KV_HEADS * HEAD_DIM
QKV_DIM = Q_DIM + 2 * KV_DIM
WINDOW = 128
BLOCK = 128
ROPE_FREQS = HEAD_DIM // 4
ROPE_BASE = 10000.0

D_INNER = 2 * D_MODEL
SSM_HEAD_DIM = 64
SSM_HEADS = D_INNER // SSM_HEAD_DIM
SSM_GROUPS = 8
HEADS_PER_GROUP = SSM_HEADS // SSM_GROUPS
GROUP_DIM = D_INNER // SSM_GROUPS
D_STATE = 128
CHUNK = 128
BC_DIM = SSM_GROUPS * D_STATE
CONV_DIM = D_INNER + 2 * BC_DIM
D_FF = ((8 * D_MODEL // 3 + 255) // 256) * 256

LANES = 128
SUBLANES = 8
MOD_ROWS = 8
CTX_ROW = 4
VMEM_LIMIT = 48 * 1024 * 1024
VMEM_LIMIT_FFN = 56 * 1024 * 1024
NEG_INF = float("-inf")
LOG2E = math.log2(math.e)
CHUNKS_PER_STEP = 2
HEADS_PER_DOT = N_HEADS


def _const_spec(shape):
    nd = len(shape)
    return pl.BlockSpec(shape, lambda *_: (0,) * nd, pipeline_mode=pl.Buffered(1))


def _layer_spec(stacked, layer):
    return pl.BlockSpec((None,) + stacked.shape[1:], lambda *_: (layer, 0, 0),
                        pipeline_mode=pl.Buffered(1))


def _silu(v):
    return v * jax.nn.sigmoid(v)


def _rms_mod(x, g, scale, shift):
    ms = jnp.mean(x * x, axis=-1, keepdims=True)
    return (x * lax.rsqrt(ms + EPS) * g) * (1.0 + scale) + shift


def _mod_chunk(m, i):
    return m[:, i * D_MODEL:(i + 1) * D_MODEL]


def _adaln_kernel(c_ref, w_ref, b_ref, o_ref):
    s = _silu(c_ref[...])
    o_ref[0] = jnp.dot(s, w_ref[0], preferred_element_type=F32) + b_ref[0]


def _adaln(cc, ada_w, ada_b):
    tn = 1536
    n = 6 * D_MODEL
    return pl.pallas_call(
        _adaln_kernel,
        grid=(DEPTH, n // tn),
        in_specs=[
            pl.BlockSpec((MOD_ROWS, D_MODEL), lambda i, j: (0, 0)),
            pl.BlockSpec((1, D_MODEL, tn), lambda i, j: (i, 0, j)),
            pl.BlockSpec((1, 1, tn), lambda i, j: (i, 0, j)),
        ],
        out_specs=pl.BlockSpec((1, MOD_ROWS, tn), lambda i, j: (i, 0, j)),
        out_shape=jax.ShapeDtypeStruct((DEPTH, MOD_ROWS, n), F32),
        compiler_params=pltpu.CompilerParams(
            dimension_semantics=("arbitrary", "arbitrary"), vmem_limit_bytes=VMEM_LIMIT),
        name="adaln",
    )(cc, ada_w, ada_b.reshape(DEPTH, 1, n))


_NT = (((1,), (1,)), ((), ()))
_TN = (((0,), (0,)), ((), ()))


def _qkv_kernel(*refs, rope):
    if rope:
        (x_ref, mod_ref, g_ref, wq_ref, wk_ref, wv_ref, cos_t_ref, sin_t_ref,
         cos_ref, sa_ref, sb_ref, qt_ref, k_ref, vt_ref) = refs
    else:
        x_ref, mod_ref, g_ref, wq_ref, wk_ref, wv_ref, qt_ref, k_ref, vt_ref = refs
    m = mod_ref[0]
    h = _rms_mod(x_ref[0], g_ref[...], _mod_chunk(m, 1), _mod_chunk(m, 0)).astype(BF16)
    q_t = lax.dot_general(wq_ref[...], h, _NT, preferred_element_type=F32)
    k = jnp.dot(h, wk_ref[...], preferred_element_type=F32)
    v_t = lax.dot_general(wv_ref[...], h, _NT, preferred_element_type=F32).astype(BF16)
    n_blk = x_ref.shape[1] // BLOCK

    def put(ref, rows, val):
        for j in range(n_blk):
            ref[0, j, rows, :] = val[:, j * BLOCK:(j + 1) * BLOCK]

    put(vt_ref, slice(0, KV_DIM), v_t)
    scale = HEAD_DIM ** -0.5 * LOG2E
    if rope:
        f = ROPE_FREQS
        for head in range(N_HEADS):
            for axis in range(2):
                r0 = head * HEAD_DIM + axis * 2 * f
                cs = cos_t_ref[axis * f:(axis + 1) * f, :]
                sn = sin_t_ref[axis * f:(axis + 1) * f, :]
                x1 = q_t[r0:r0 + f, :]
                x2 = q_t[r0 + f:r0 + 2 * f, :]
                put(qt_ref, slice(r0, r0 + f), ((x1 * cs - x2 * sn) * scale).astype(BF16))
                put(qt_ref, slice(r0 + f, r0 + 2 * f), ((x2 * cs + x1 * sn) * scale).astype(BF16))
        for blk in range(KV_DIM // LANES):
            t = k[:, blk * LANES:(blk + 1) * LANES]
            t = (t * cos_ref[...] + pltpu.roll(t, LANES - f, 1) * sa_ref[...]
                 + pltpu.roll(t, f, 1) * sb_ref[...])
            k_ref[0, :, blk * LANES:(blk + 1) * LANES] = t.astype(BF16)
    else:
        put(qt_ref, slice(0, Q_DIM), (q_t * scale).astype(BF16))
        k_ref[0] = k.astype(BF16)


def _rope_tables(L):
    f32 = np.float32
    rows = L // GRID_W
    row = np.repeat(np.arange(rows, dtype=f32), GRID_W)
    col = np.tile(np.arange(GRID_W, dtype=f32), rows)
    inv = (f32(ROPE_BASE) ** (-np.arange(ROPE_FREQS, dtype=f32) / f32(ROPE_FREQS))).astype(f32)
    ang_r = (row[:, None] * inv).astype(f32)
    ang_c = (col[:, None] * inv).astype(f32)
    zero = np.zeros_like(ang_r)

    def lanes(r_first, r_second, c_first, c_second):
        head = np.concatenate([r_first, r_second, c_first, c_second], axis=-1)
        return np.tile(head, (1, LANES // HEAD_DIM)).astype(f32)

    cr, sr, cc, sc = np.cos(ang_r), np.sin(ang_r), np.cos(ang_c), np.sin(ang_c)
    cos_t = np.ascontiguousarray(np.concatenate([cr, cc], axis=1).T).astype(f32)
    sin_t = np.ascontiguousarray(np.concatenate([sr, sc], axis=1).T).astype(f32)
    return (cos_t, sin_t, lanes(cr, cr, cc, cc), lanes(-sr, zero, -sc, zero),
            lanes(zero, sr, zero, sc))


def _qkv(x, mod, g, w_q_t, w_k, w_v_t, tables, mod_row):
    B, L, _ = x.shape
    tm = min(L, 1024)
    rope = tables is not None
    in_specs = [
        pl.BlockSpec((1, tm, D_MODEL), lambda b, i: (b, i, 0)),
        pl.BlockSpec((1, 1, 6 * D_MODEL), lambda b, i: (mod_row(b), 0, 0)),
        _const_spec((1, D_MODEL)),
        _const_spec(w_q_t.shape), _const_spec(w_k.shape), _const_spec(w_v_t.shape),
    ]
    args = [x, mod, g, w_q_t, w_k, w_v_t]
    if rope:
        in_specs += [pl.BlockSpec((2 * ROPE_FREQS, tm), lambda b, i: (0, i))] * 2
        in_specs += [pl.BlockSpec((tm, LANES), lambda b, i: (i, 0))] * 3
        args += list(tables)
    return pl.pallas_call(
        functools.partial(_qkv_kernel, rope=rope),
        grid=(B, L // tm),
        in_specs=in_specs,
        out_specs=[
            pl.BlockSpec((1, tm // BLOCK, Q_DIM, BLOCK), lambda b, i: (b, i, 0, 0)),
            pl.BlockSpec((1, tm, KV_DIM), lambda b, i: (b, i, 0)),
            pl.BlockSpec((1, tm // BLOCK, KV_DIM, BLOCK), lambda b, i: (b, i, 0, 0)),
        ],
        out_shape=[
            jax.ShapeDtypeStruct((B, L // BLOCK, Q_DIM, BLOCK), BF16),
            jax.ShapeDtypeStruct((B, L, KV_DIM), BF16),
            jax.ShapeDtypeStruct((B, L // BLOCK, KV_DIM, BLOCK), BF16),
        ],
        compiler_params=pltpu.CompilerParams(
            dimension_semantics=("arbitrary", "arbitrary"), vmem_limit_bytes=VMEM_LIMIT),
        name="qkv_rope" if rope else "qkv_ctx",
    )(*args)


def _attn_kernel(*refs, window, nq, n_ctx):
    if window:
        (sink_ref, qt_ref, kp_ref, kc_ref, kn_ref, vp_ref, vc_ref, vn_ref,
         kx_ref, vx_ref, ot_ref, kbuf, vtbuf) = refs
    else:
        sink_ref, qt_ref, kx_ref, vx_ref, ot_ref, kbuf, vtbuf = refs
    width = HEADS_PER_DOT * BLOCK
    kbuf[0:n_ctx, :] = kx_ref[0]
    for j in range(n_ctx // BLOCK):
        vtbuf[:, j * BLOCK:(j + 1) * BLOCK] = vx_ref[0, j]
    if window:
        for j, (kr, vr) in enumerate(((kp_ref, vp_ref), (kc_ref, vc_ref), (kn_ref, vn_ref))):
            kbuf[n_ctx + j * BLOCK:n_ctx + (j + 1) * BLOCK, :] = kr[0]
            vtbuf[:, n_ctx + j * BLOCK:n_ctx + (j + 1) * BLOCK] = vr[0, 0]
        n = pl.program_id(1)
        kj = lax.broadcasted_iota(jnp.int32, (BLOCK, width), 0)
        qi = lax.broadcasted_iota(jnp.int32, (BLOCK, width), 1) % BLOCK
        bias_prev = jnp.where((kj >= qi) & (n > 0), 0.0, NEG_INF).astype(F32)
        bias_next = jnp.where((kj <= qi) & (n < nq - 1), 0.0, NEG_INF).astype(F32)
    lane = lax.broadcasted_iota(jnp.int32, (1, width), 1)
    ones_rows = jnp.ones((16, kbuf.shape[0]), BF16)
    for h0 in range(0, N_HEADS, HEADS_PER_DOT):
        cols = []
        for h in range(h0, h0 + HEADS_PER_DOT):
            g = h // Q_PER_KV
            parts = [qt_ref[0, 0, h * HEAD_DIM:(h + 1) * HEAD_DIM, :]]
            if g > 0:
                parts.insert(0, jnp.zeros((g * HEAD_DIM, BLOCK), BF16))
            if g < N_KV_HEADS - 1:
                parts.append(jnp.zeros(((N_KV_HEADS - 1 - g) * HEAD_DIM, BLOCK), BF16))
            cols.append(jnp.concatenate(parts, axis=0) if len(parts) > 1 else parts[0])
        q_exp = jnp.concatenate(cols, axis=1) if len(cols) > 1 else cols[0]
        sink = jnp.full((1, width), sink_ref[h0], F32)
        for r in range(1, HEADS_PER_DOT):
            sink = jnp.where(lane >= r * BLOCK, sink_ref[h0 + r], sink)
        sink = sink * LOG2E
        s = jnp.dot(kbuf[...], q_exp, preferred_element_type=F32)
        if window:
            s = jnp.concatenate([
                s[:n_ctx], s[n_ctx:n_ctx + BLOCK] + bias_prev,
                s[n_ctx + BLOCK:n_ctx + 2 * BLOCK], s[n_ctx + 2 * BLOCK:] + bias_next], axis=0)
        mx = jnp.maximum(jnp.max(s, axis=0, keepdims=True), sink)
        p = jnp.exp2(s - mx).astype(BF16)
        sink_p = jnp.exp2(sink - mx)
        gw = Q_PER_KV * BLOCK
        for g0 in range(0, HEADS_PER_DOT // Q_PER_KV):
            g = h0 // Q_PER_KV + g0
            gl = slice(g0 * gw, (g0 + 1) * gw)
            v_ext = jnp.concatenate([vtbuf[g * HEAD_DIM:(g + 1) * HEAD_DIM, :], ones_rows], axis=0)
            pv = jnp.dot(v_ext, p[:, gl], preferred_element_type=F32)
            o_t = pv[:HEAD_DIM, :] / (pv[HEAD_DIM:HEAD_DIM + 1, :] + sink_p[:, gl])
            for r in range(Q_PER_KV):
                h = g * Q_PER_KV + r
                ot_ref[0, 0, h * HEAD_DIM:(h + 1) * HEAD_DIM, :] = (
                    o_t[:, r * BLOCK:(r + 1) * BLOCK].astype(BF16))


def _attn(q_t, k, v_t, kx, vx_t, sinks):
    B, nq = q_t.shape[:2]
    C = kx.shape[1]
    window = k is not None
    smem = pl.BlockSpec(memory_space=pltpu.SMEM)
    q_spec = pl.BlockSpec((1, 1, Q_DIM, BLOCK), lambda b, n: (b, n, 0, 0))
    kx_spec = pl.BlockSpec((1, C, KV_DIM), lambda b, n: (b, 0, 0))
    vx_spec = pl.BlockSpec((1, C // BLOCK, KV_DIM, BLOCK), lambda b, n: (b, 0, 0, 0))
    if window:
        prev = lambda n: jnp.maximum(n - 1, 0)
        nxt = lambda n: jnp.minimum(n + 1, nq - 1)
        k_spec = lambda f: pl.BlockSpec((1, BLOCK, KV_DIM), lambda b, n: (b, f(n), 0))
        v_spec = lambda f: pl.BlockSpec((1, 1, KV_DIM, BLOCK), lambda b, n: (b, f(n), 0, 0))
        same = lambda n: n
        in_specs = [smem, q_spec, k_spec(prev), k_spec(same), k_spec(nxt),
                    v_spec(prev), v_spec(same), v_spec(nxt), kx_spec, vx_spec]
        args = (sinks, q_t, k, k, k, v_t, v_t, v_t, kx, vx_t)
        n_keys = C + 3 * BLOCK
    else:
        in_specs = [smem, q_spec, kx_spec, vx_spec]
        args = (sinks, q_t, kx, vx_t)
        n_keys = C
    return pl.pallas_call(
        functools.partial(_attn_kernel, window=window, nq=nq, n_ctx=C),
        grid=(B, nq),
        in_specs=in_specs,
        out_specs=q_spec,
        out_shape=jax.ShapeDtypeStruct((B, nq, Q_DIM, BLOCK), BF16),
        scratch_shapes=[pltpu.VMEM((n_keys, KV_DIM), BF16), pltpu.VMEM((KV_DIM, n_keys), BF16)],
        compiler_params=pltpu.CompilerParams(
            dimension_semantics=("arbitrary", "arbitrary"), vmem_limit_bytes=VMEM_LIMIT),
        name="attn_window" if window else "attn_ctx",
    )(*args)


def _mix_ffn_kernel(*refs, ssm, final):
    if ssm:
        (x_ref, mod_ref, yf_ref, yb_ref, z_ref, xs_ref, dsk_ref, ng_ref,
         wo_ref, gf_ref, wg_ref, wu_ref, wd_ref) = refs[:13]
        rest = refs[13:]
    else:
        (x_ref, mod_ref, o_ref_in, wo_ref, gf_ref, wg_ref, wu_ref, wd_ref) = refs[:8]
        rest = refs[8:]
    if final:
        fg_ref, out_ref = rest
    else:
        (out_ref,) = rest
    m = mod_ref[0]
    if ssm:
        mix = None
        for g in range(SSM_GROUPS):
            gs = slice(g * GROUP_DIM, (g + 1) * GROUP_DIM)
            yg = ((yf_ref[0, :, gs].astype(F32) + yb_ref[0, :, gs].astype(F32)
                   + dsk_ref[:, gs] * xs_ref[0, :, gs].astype(F32))
                  * _silu(z_ref[0, :, gs].astype(F32)))
            ms = jnp.mean(yg * yg, axis=-1, keepdims=True)
            part = (yg * lax.rsqrt(ms + EPS) * ng_ref[:, gs]).astype(BF16)
            term = jnp.dot(part, wo_ref[gs, :], preferred_element_type=F32)
            mix = term if mix is None else mix + term
    else:
        o_t = jnp.concatenate([o_ref_in[0, j] for j in range(o_ref_in.shape[1])], axis=1)
        mix = lax.dot_general(o_t, wo_ref[...], _TN, preferred_element_type=F32)
    x1 = x_ref[0] + _mod_chunk(m, 2) * mix
    h2 = _rms_mod(x1, gf_ref[...], _mod_chunk(m, 4), _mod_chunk(m, 3)).astype(BF16)
    gate = jnp.dot(h2, wg_ref[...], preferred_element_type=F32)
    up = jnp.dot(h2, wu_ref[...], preferred_element_type=F32)
    act = (_silu(gate) * up).astype(BF16)
    x2 = x1 + _mod_chunk(m, 5) * jnp.dot(act, wd_ref[...], preferred_element_type=F32)
    if final:
        ms = jnp.mean(x2 * x2, axis=-1, keepdims=True)
        x2 = x2 * lax.rsqrt(ms + EPS) * fg_ref[...]
    out_ref[0] = x2


def _mix_ffn(x, mod, mod_row, mixer_in, w_mix, g_ffn, w_gate, w_up, w_down, layer,
             ssm_norm_g=None, d_skip=None, final_g=None):
    B, L, _ = x.shape
    tm = min(L, 512)
    ssm = ssm_norm_g is not None
    final = final_g is not None
    row = lambda width: pl.BlockSpec((1, tm, width), lambda b, i: (b, i, 0))
    in_specs = [row(D_MODEL), pl.BlockSpec((1, 1, 6 * D_MODEL), lambda b, i: (mod_row(b), 0, 0))]
    args = [x, mod]
    if ssm:
        in_specs += [row(D_INNER)] * 4 + [_const_spec((1, D_INNER))] * 2
        args += list(mixer_in) + [d_skip, ssm_norm_g]
    else:
        in_specs += [pl.BlockSpec((1, tm // BLOCK, Q_DIM, BLOCK), lambda b, i: (b, i, 0, 0))]
        args += [mixer_in]
    in_specs += [_layer_spec(w_mix, 0), _const_spec((1, D_MODEL)), _layer_spec(w_gate, layer),
                 _layer_spec(w_up, layer), _layer_spec(w_down, layer)]
    args += [w_mix, g_ffn, w_gate, w_up, w_down]
    if final:
        in_specs += [_const_spec((1, D_MODEL))]
        args += [final_g]
    return pl.pallas_call(
        functools.partial(_mix_ffn_kernel, ssm=ssm, final=final),
        grid=(B, L // tm),
        in_specs=in_specs,
        out_specs=row(D_MODEL),
        out_shape=jax.ShapeDtypeStruct((B, L, D_MODEL), F32),
        compiler_params=pltpu.CompilerParams(
            dimension_semantics=("arbitrary", "arbitrary"), vmem_limit_bytes=VMEM_LIMIT_FFN),
        name="ssm_out_ffn" if ssm else "attn_out_ffn",
    )(*args)


def _inproj_kernel(x_ref, xp_ref, xn_ref, mod_ref, g_ref, w_ref, wdt_ref, dtb_ref, cw_ref, cb_ref,
                   z_ref, xa_ref, dt_ref, *, n_tiles):
    i = pl.program_id(1)
    m = mod_ref[0]
    g, scale, shift = g_ref[...], _mod_chunk(m, 1), _mod_chunk(m, 0)
    tm = x_ref.shape[1]
    h = _rms_mod(x_ref[0], g, scale, shift).astype(BF16)
    x_ext = jnp.concatenate([xp_ref[0], x_ref[0], xn_ref[0]], axis=0)
    h_ext = _rms_mod(x_ext, g, scale, shift).astype(BF16)
    z_ref[0] = jnp.dot(h, w_ref[:, :D_INNER], preferred_element_type=F32).astype(BF16)
    raw = jnp.dot(h, wdt_ref[...], preferred_element_type=F32) + dtb_ref[...]
    dt_ref[0] = jnp.maximum(raw, 0.0) + jnp.log1p(jnp.exp(-jnp.abs(raw)))
    keep_prev = jnp.where(i > 0, 1.0, 0.0)
    keep_next = jnp.where(i < n_tiles - 1, 1.0, 0.0)
    cstep = 512
    for j in range(CONV_DIM // cstep):
        sl = slice(j * cstep, (j + 1) * cstep)
        u = jnp.dot(h_ext, w_ref[:, D_INNER + j * cstep:D_INNER + (j + 1) * cstep],
                    preferred_element_type=F32)
        u = jnp.concatenate([u[0:SUBLANES] * keep_prev, u[SUBLANES:SUBLANES + tm],
                             u[SUBLANES + tm:] * keep_next], axis=0)
        rows = tm + 2 * SUBLANES
        below = pltpu.roll(u, 1, 0)[SUBLANES:SUBLANES + tm]
        above = pltpu.roll(u, rows - 1, 0)[SUBLANES:SUBLANES + tm]
        conv = (below * cw_ref[0:1, sl] + u[SUBLANES:SUBLANES + tm] * cw_ref[1:2, sl]
                + above * cw_ref[2:3, sl] + cb_ref[:, sl])
        xa_ref[0, :, sl] = _silu(conv).astype(BF16)


def _inproj(x, mod, mod_row, *, g, w_in, w_dt, dt_bias, conv_w, conv_b):
    B, L, _ = x.shape
    tm = min(L, 256)
    n_tiles = L // tm
    per = tm // SUBLANES
    row = lambda width: pl.BlockSpec((1, tm, width), lambda b, i: (b, i, 0))
    return pl.pallas_call(
        functools.partial(_inproj_kernel, n_tiles=n_tiles),
        grid=(B, n_tiles),
        in_specs=[row(D_MODEL),
                  pl.BlockSpec((1, SUBLANES, D_MODEL), lambda b, i: (b, jnp.maximum(i * per - 1, 0), 0)),
                  pl.BlockSpec((1, SUBLANES, D_MODEL),
                               lambda b, i: (b, jnp.minimum((i + 1) * per, L // SUBLANES - 1), 0)),
                  pl.BlockSpec((1, 1, 6 * D_MODEL), lambda b, i: (mod_row(b), 0, 0)),
                  _const_spec((1, D_MODEL)), _const_spec(w_in.shape), _const_spec(w_dt.shape),
                  _const_spec((1, 2 * LANES)), _const_spec((8, CONV_DIM)), _const_spec((1, CONV_DIM))],
        out_specs=[row(D_INNER), row(CONV_DIM), row(2 * LANES)],
        out_shape=[jax.ShapeDtypeStruct((B, L, D_INNER), BF16),
                   jax.ShapeDtypeStruct((B, L, CONV_DIM), BF16),
                   jax.ShapeDtypeStruct((B, L, 2 * LANES), F32)],
        compiler_params=pltpu.CompilerParams(
            dimension_semantics=("arbitrary", "arbitrary"), vmem_limit_bytes=VMEM_LIMIT),
        name="ssm_inproj",
    )(x, x, x, mod, g, w_in, w_dt, dt_bias, conv_w, conv_b)


def _split3(v):
    hi = v.astype(BF16)
    r1 = v - hi.astype(F32)
    mid = r1.astype(BF16)
    lo = (r1 - mid.astype(F32)).astype(BF16)
    return hi, mid, lo


def _ssd_kernel(*refs, reverse, nc, has_init, want_y):
    (xa_ref, dt_ref, alog_ref, e64_ref), rest = refs[:4], refs[4:]
    if has_init:
        h0_ref, rest = rest[0], rest[1:]
    if want_y:
        y_ref, rest = rest[0], rest[1:]
    hfin_ref, state, ce, tbuf = rest
    c = pl.program_id(1)
    last = 0 if reverse else CHUNK - 1

    @pl.when(c == 0)
    def _():
        state[...] = h0_ref[0] if has_init else jnp.zeros(state.shape, F32)

    lane = lax.broadcasted_iota(jnp.int32, (CHUNK, LANES), 1)
    row = lax.broadcasted_iota(jnp.int32, (CHUNK, LANES), 0)
    lane_valid = (lane, (lane >= row) if reverse else (lane <= row))
    subs = range(CHUNKS_PER_STEP - 1, -1, -1) if reverse else range(CHUNKS_PER_STEP)
    for sub in subs:
        rows = slice(sub * CHUNK, (sub + 1) * CHUNK)
        _prepare_decay(dt_ref, rows, alog_ref, e64_ref, ce, tbuf, lane_valid, last)
        _scan_chunk(xa_ref, y_ref if want_y else None, rows, state, ce, tbuf, lane_valid, last)

    @pl.when(c == nc // CHUNKS_PER_STEP - 1)
    def _():
        hfin_ref[0] = state[...]


def _prepare_decay(dt_ref, rows, alog_ref, e64_ref, ce, tbuf, lane_valid, last):
    lane, valid = lane_valid
    dt = jnp.where(lane < SSM_HEADS, dt_ref[0, rows, :], 0.0)
    a = dt * (-LOG2E * jnp.exp(alog_ref[0]))
    tri = jnp.where(valid, 1.0, 0.0).astype(BF16)
    a_hi, a_mid, a_lo = _split3(a)
    cum = (jnp.dot(tri, a_hi, preferred_element_type=F32)
           + jnp.dot(tri, a_mid, preferred_element_type=F32)
           + jnp.dot(tri, a_lo, preferred_element_type=F32))
    c_hi, c_mid, c_lo = _split3(cum)
    pieces = (c_hi.astype(F32) + pltpu.roll(c_mid.astype(F32), SSM_HEADS, 1)
              + pltpu.roll(c_lo.astype(F32), 2 * SSM_HEADS, 1)).astype(BF16)
    ce[...] = jnp.dot(pieces, e64_ref[...], preferred_element_type=F32)
    tb = (cum + pltpu.roll(dt, SSM_HEADS, 1)).T
    cum_t = tb[0:SSM_HEADS, :]
    dt_t = tb[SSM_HEADS:2 * SSM_HEADS, :]
    tbuf[0:SSM_HEADS, :] = cum_t - jnp.log2(dt_t)
    tbuf[SSM_HEADS:2 * SSM_HEADS, :] = jnp.exp2(cum_t[:, last:last + 1] - cum_t) * dt_t


def _scan_chunk(xa_ref, y_ref, rows, state, ce, tbuf, lane_valid, last):
    lane, valid = lane_valid
    want_y = y_ref is not None
    lane_g = lax.broadcasted_iota(jnp.int32, (CHUNK, GROUP_DIM), 1)
    head_mask = [jnp.where((lane_g >= hl * SSM_HEAD_DIM) & (lane_g < (hl + 1) * SSM_HEAD_DIM),
                           1.0, 0.0).astype(BF16) for hl in range(HEADS_PER_GROUP)]
    low_half = lane < SSM_HEAD_DIM
    for g in range(SSM_GROUPS):
        gs = slice(g * GROUP_DIM, (g + 1) * GROUP_DIM)
        b_g = xa_ref[0, rows, D_INNER + g * D_STATE:D_INNER + (g + 1) * D_STATE]
        c_g = xa_ref[0, rows, D_INNER + BC_DIM + g * D_STATE:D_INNER + BC_DIM + (g + 1) * D_STATE]
        b_t = b_g.astype(F32).T
        x_g = xa_ref[0, rows, gs]
        if want_y:
            cb_mat = lax.dot_general(c_g, b_g, _NT, preferred_element_type=F32)
            y_g = (jnp.dot(c_g, state[g].astype(BF16), preferred_element_type=F32)
                   * jnp.exp2(ce[:, gs]))
        st = jnp.zeros((D_STATE, GROUP_DIM), F32)
        for hp in range(HEADS_PER_GROUP // 2):
            pair = g * (HEADS_PER_GROUP // 2) + hp
            lhs_y, lhs_s, rhs = [], [], []
            if want_y:
                ce_pair = ce[:, pair * LANES:(pair + 1) * LANES]
                swapped = pltpu.roll(ce_pair, SSM_HEAD_DIM, 1)
            for hh in range(2):
                hl = 2 * hp + hh
                h = g * HEADS_PER_GROUP + hl
                if want_y:
                    own = low_half if hh == 0 else jnp.logical_not(low_half)
                    cum_i = jnp.where(own, ce_pair, swapped)
                    dec = jnp.exp2(jnp.where(valid, cum_i - tbuf[h:h + 1, :], NEG_INF))
                    lhs_y.append((dec * cb_mat).astype(BF16))
                lhs_s.append((b_t * tbuf[SSM_HEADS + h:SSM_HEADS + h + 1, :]).astype(BF16))
                rhs.append(x_g * head_mask[hl])
            rhs = jnp.concatenate(rhs, axis=0)
            if want_y:
                y_g = y_g + jnp.dot(jnp.concatenate(lhs_y, axis=1), rhs, preferred_element_type=F32)
            st = st + jnp.dot(jnp.concatenate(lhs_s, axis=1), rhs, preferred_element_type=F32)
        if want_y:
            y_ref[0, rows, gs] = y_g.astype(BF16)
        state[g] = state[g] * jnp.exp2(ce[last:last + 1, gs]) + st


def _ssd(xact, dt, a_log, e64, h0, direction, want_y):
    B, L, _ = xact.shape
    nc = L // CHUNK
    reverse = direction == 1
    has_init = h0 is not None
    steps = nc // CHUNKS_PER_STEP
    rows = CHUNKS_PER_STEP * CHUNK
    chunk = (lambda c: steps - 1 - c) if reverse else (lambda c: c)
    st_shape = (SSM_GROUPS, D_STATE, GROUP_DIM)
    st_spec = pl.BlockSpec((1,) + st_shape, lambda b, c: (b, 0, 0, 0))
    in_specs = [
        pl.BlockSpec((1, rows, CONV_DIM), lambda b, c: (b, chunk(c), 0)),
        pl.BlockSpec((1, rows, LANES), lambda b, c: (b, chunk(c), direction)),
        pl.BlockSpec((1, 1, LANES), lambda b, c: (direction, 0, 0)),
        _const_spec(e64.shape),
    ]
    args = [xact, dt, a_log, e64]
    if has_init:
        in_specs.append(st_spec)
        args.append(h0)
    out_specs, out_shape = [], []
    if want_y:
        out_specs.append(pl.BlockSpec((1, rows, D_INNER), lambda b, c: (b, chunk(c), 0)))
        out_shape.append(jax.ShapeDtypeStruct((B, L, D_INNER), BF16))
    out_specs.append(st_spec)
    out_shape.append(jax.ShapeDtypeStruct((B,) + st_shape, F32))
    scratch = [pltpu.VMEM(st_shape, F32), pltpu.VMEM((CHUNK, D_INNER), F32),
               pltpu.VMEM((2 * SSM_HEADS, CHUNK), F32)]
    return pl.pallas_call(
        functools.partial(_ssd_kernel, reverse=reverse, nc=nc, has_init=has_init, want_y=want_y),
        grid=(B, steps),
        in_specs=in_specs,
        out_specs=out_specs,
        out_shape=out_shape,
        scratch_shapes=scratch,
        compiler_params=pltpu.CompilerParams(
            dimension_semantics=("arbitrary", "arbitrary"), vmem_limit_bytes=VMEM_LIMIT),
        name="ssd_bwd" if reverse else "ssd_fwd",
    )(*args)


def _expand_matrix(width):
    k = np.arange(LANES)[:, None]
    col_head = (np.arange(SSM_HEADS * width) // width)[None, :]
    return jnp.asarray((k % SSM_HEADS == col_head) & (k < 3 * SSM_HEADS), dtype=BF16)


def kernel(x, c, ctx, c_ctx, ada_w, ada_b, norm_mix_g, norm_ffn_g, attn_w_qkv, attn_w_o, attn_sinks,
           ssm_w_in, ssm_conv_w, ssm_conv_b, ssm_dt_bias, ssm_A_log, ssm_D, ssm_norm_g, ssm_w_out,
           ffn_w_gate, ffn_w_up, ffn_w_down, final_norm_g):
    B, L, _ = x.shape
    assert x.shape == (B, L, D_MODEL) and B <= CTX_ROW and L % 1024 == 0
    bf = lambda w: w.astype(BF16)
    row2 = lambda v: v.reshape(1, -1)

    cc = jnp.zeros((MOD_ROWS, D_MODEL), F32).at[:B].set(c).at[CTX_ROW].set(c_ctx)
    mod = _adaln(cc, ada_w, ada_b)
    mods = [mod[i].reshape(MOD_ROWS, 1, 6 * D_MODEL) for i in range(DEPTH)]
    x_row = lambda b: b
    ctx_row = lambda b: CTX_ROW

    w_qkv = attn_w_qkv[0]
    w_q_t = bf(w_qkv[:, :Q_DIM].T)
    w_k = bf(w_qkv[:, Q_DIM:Q_DIM + KV_DIM])
    w_v_t = bf(w_qkv[:, Q_DIM + KV_DIM:].T)
    g_mix = row2(norm_mix_g[0])
    q, k, v = _qkv(x, mods[0], g_mix, w_q_t, w_k, w_v_t, _rope_tables(L), x_row)
    qc, kc, vc = _qkv(ctx, mods[0], g_mix, w_q_t, w_k, w_v_t, None, ctx_row)
    sinks = attn_sinks[0].astype(F32)
    o = _attn(q, k, v, kc, vc, sinks)
    oc = _attn(qc, None, None, kc, vc, sinks)
    w_gate, w_up, w_down = bf(ffn_w_gate), bf(ffn_w_up), bf(ffn_w_down)
    ffn0 = (bf(attn_w_o), row2(norm_ffn_g[0]), w_gate, w_up, w_down, 0)
    x = _mix_ffn(x, mods[0], x_row, o, *ffn0)
    ctx = _mix_ffn(ctx, mods[0], ctx_row, oc, *ffn0)

    w_in = ssm_w_in[0]
    w_dt_raw = w_in[:, D_INNER + CONV_DIM:]
    w_dt = jnp.zeros((D_MODEL, 2 * LANES), F32)
    w_dt = w_dt.at[:, :SSM_HEADS].set(w_dt_raw[:, :SSM_HEADS])
    w_dt = bf(w_dt.at[:, LANES:LANES + SSM_HEADS].set(w_dt_raw[:, SSM_HEADS:]))
    dt_bias = jnp.zeros((2, LANES), F32).at[:, :SSM_HEADS].set(ssm_dt_bias[0]).reshape(1, 2 * LANES)
    a_log = jnp.zeros((2, 1, LANES), F32).at[:, 0, :SSM_HEADS].set(ssm_A_log[0])
    conv_w = jnp.zeros((8, CONV_DIM), F32).at[:3].set(ssm_conv_w[0])
    conv_b = row2(ssm_conv_b[0])
    d_skip = row2(jnp.repeat(ssm_D[0], SSM_HEAD_DIM))
    e64 = _expand_matrix(SSM_HEAD_DIM)
    g_mix1 = row2(norm_mix_g[1])

    inproj = functools.partial(_inproj, g=g_mix1, w_in=bf(w_in), w_dt=w_dt, dt_bias=dt_bias,
                               conv_w=conv_w, conv_b=conv_b)
    z, xact, dt = inproj(x, mods[1], x_row)
    _, xact_c, dt_c = inproj(ctx, mods[1], ctx_row)
    ssd = functools.partial(_ssd, a_log=a_log, e64=e64)
    (h_f,) = ssd(xact_c, dt_c, h0=None, direction=0, want_y=False)
    (h_b,) = ssd(xact_c, dt_c, h0=None, direction=1, want_y=False)
    y_f, _ = ssd(xact, dt, h0=h_f, direction=0, want_y=True)
    y_b, _ = ssd(xact, dt, h0=h_b, direction=1, want_y=True)
    return _mix_ffn(x, mods[1], x_row, (y_f, y_b, z, xact), bf(ssm_w_out), row2(norm_ffn_g[1]),
                    w_gate, w_up, w_down, 1,
                    ssm_norm_g=row2(ssm_norm_g[0]), d_skip=d_skip, final_g=row2(final_norm_g))
```

```python
import functools
import math

import numpy as np

import jax
import jax.numpy as jnp
from jax import lax
from jax.experimental import pallas as pl
from jax.experimental.pallas import tpu as pltpu

F32 = jnp.float32
BF16 = jnp.bfloat16

D_MODEL = 1024
DEPTH = 2
GRID_W = 64
EPS = 1e-6

HEAD_DIM = 64
N_HEADS = D_MODEL // HEAD_DIM
N_KV_HEADS = N_HEADS // 4
Q_PER_KV = N_HEADS // N_KV_HEADS
Q_DIM = N_HEADS * HEAD_DIM
KV_DIM = N_KV_HEADS * HEAD_DIM
QKV_DIM = Q_DIM + 2 * KV_DIM
WINDOW = 128
BLOCK = 128
ROPE_FREQS = HEAD_DIM // 4
ROPE_BASE = 10000.0

D_INNER = 2 * D_MODEL
SSM_HEAD_DIM = 64
SSM_HEADS = D_INNER // SSM_HEAD_DIM
SSM_GROUPS = 8
HEADS_PER_GROUP = SSM_HEADS // SSM_GROUPS
GROUP_DIM = D_INNER // SSM_GROUPS
D_STATE = 128
CHUNK = 128
BC_DIM = SSM_GROUPS * D_STATE
CONV_DIM = D_INNER + 2 * BC_DIM
D_FF = ((8 * D_MODEL // 3 + 255) // 256) * 256

LANES = 128
SUBLANES = 8
MOD_ROWS = 8
CTX_ROW = 4
VMEM_LIMIT = 48 * 1024 * 1024
VMEM_LIMIT_FFN = 56 * 1024 * 1024
NEG_INF = float("-inf")
LOG2E = math.log2(math.e)
CHUNKS_PER_STEP = 4
HEADS_PER_DOT = N_HEADS


def _const_spec(shape):
    nd = len(shape)
    return pl.BlockSpec(shape, lambda *_: (0,) * nd, pipeline_mode=pl.Buffered(1))


def _layer_spec(stacked, layer):
    return pl.BlockSpec((None,) + stacked.shape[1:], lambda *_: (layer, 0, 0),
                        pipeline_mode=pl.Buffered(1))


def _silu(v):
    return v * jax.nn.sigmoid(v)


def _rms_mod(x, g, scale, shift):
    ms = jnp.mean(x * x, axis=-1, keepdims=True)
    return (x * lax.rsqrt(ms + EPS) * g) * (1.0 + scale) + shift


def _mod_chunk(m, i):
    return m[:, i * D_MODEL:(i + 1) * D_MODEL]


def _adaln_kernel(c_ref, w_ref, b_ref, o_ref):
    s = _silu(c_ref[...])
    o_ref[0] = jnp.dot(s, w_ref[0], preferred_element_type=F32) + b_ref[0]


def _adaln(cc, ada_w, ada_b):
    tn = 1536
    n = 6 * D_MODEL
    return pl.pallas_call(
        _adaln_kernel,
        grid=(DEPTH, n // tn),
        in_specs=[
            pl.BlockSpec((MOD_ROWS, D_MODEL), lambda i, j: (0, 0)),
            pl.BlockSpec((1, D_MODEL, tn), lambda i, j: (i, 0, j)),
            pl.BlockSpec((1, 1, tn), lambda i, j: (i, 0, j)),
        ],
        out_specs=pl.BlockSpec((1, MOD_ROWS, tn), lambda i, j: (i, 0, j)),
        out_shape=jax.ShapeDtypeStruct((DEPTH, MOD_ROWS, n), F32),
        compiler_params=pltpu.CompilerParams(
            dimension_semantics=("arbitrary", "arbitrary"), vmem_limit_bytes=VMEM_LIMIT),
        name="adaln",
    )(cc, ada_w, ada_b.reshape(DEPTH, 1, n))


_NT = (((1,), (1,)), ((), ()))
_TN = (((0,), (0,)), ((), ()))


def _qkv_kernel(*refs, rope):
    if rope:
        (x_ref, mod_ref, g_ref, wq_ref, wk_ref, wv_ref, cos_t_ref, sin_t_ref,
         cos_ref, sa_ref, sb_ref, qt_ref, k_ref, vt_ref) = refs
    else:
        x_ref, mod_ref, g_ref, wq_ref, wk_ref, wv_ref, qt_ref, k_ref, vt_ref = refs
    m = mod_ref[0]
    h = _rms_mod(x_ref[0], g_ref[...], _mod_chunk(m, 1), _mod_chunk(m, 0)).astype(BF16)
    q_t = lax.dot_general(wq_ref[...], h, _NT, preferred_element_type=F32)
    k = jnp.dot(h, wk_ref[...], preferred_element_type=F32)
    v_t = lax.dot_general(wv_ref[...], h, _NT, preferred_element_type=F32).astype(BF16)
    n_blk = x_ref.shape[1] // BLOCK

    def put(ref, rows, val):
        for j in range(n_blk):
            ref[0, j, rows, :] = val[:, j * BLOCK:(j + 1) * BLOCK]

    put(vt_ref, slice(0, KV_DIM), v_t)
    scale = HEAD_DIM ** -0.5 * LOG2E
    if rope:
        f = ROPE_FREQS
        for head in range(N_HEADS):
            for axis in range(2):
                r0 = head * HEAD_DIM + axis * 2 * f
                cs = cos_t_ref[axis * f:(axis + 1) * f, :]
                sn = sin_t_ref[axis * f:(axis + 1) * f, :]
                x1 = q_t[r0:r0 + f, :]
                x2 = q_t[r0 + f:r0 + 2 * f, :]
                put(qt_ref, slice(r0, r0 + f), ((x1 * cs - x2 * sn) * scale).astype(BF16))
                put(qt_ref, slice(r0 + f, r0 + 2 * f), ((x2 * cs + x1 * sn) * scale).astype(BF16))
        for blk in range(KV_DIM // LANES):
            t = k[:, blk * LANES:(blk + 1) * LANES]
            t = (t * cos_ref[...] + pltpu.roll(t, LANES - f, 1) * sa_ref[...]
                 + pltpu.roll(t, f, 1) * sb_ref[...])
            k_ref[0, :, blk * LANES:(blk + 1) * LANES] = t.astype(BF16)
    else:
        put(qt_ref, slice(0, Q_DIM), (q_t * scale).astype(BF16))
        k_ref[0] = k.astype(BF16)


def _rope_tables(L):
    f32 = np.float32
    rows = L // GRID_W
    row = np.repeat(np.arange(rows, dtype=f32), GRID_W)
    col = np.tile(np.arange(GRID_W, dtype=f32), rows)
    inv = (f32(ROPE_BASE) ** (-np.arange(ROPE_FREQS, dtype=f32) / f32(ROPE_FREQS))).astype(f32)
    ang_r = (row[:, None] * inv).astype(f32)
    ang_c = (col[:, None] * inv).astype(f32)
    zero = np.zeros_like(ang_r)

    def lanes(r_first, r_second, c_first, c_second):
        head = np.concatenate([r_first, r_second, c_first, c_second], axis=-1)
        return np.tile(head, (1, LANES // HEAD_DIM)).astype(f32)

    cr, sr, cc, sc = np.cos(ang_r), np.sin(ang_r), np.cos(ang_c), np.sin(ang_c)
    cos_t = np.ascontiguousarray(np.concatenate([cr, cc], axis=1).T).astype(f32)
    sin_t = np.ascontiguousarray(np.concatenate([sr, sc], axis=1).T).astype(f32)
    return (cos_t, sin_t, lanes(cr, cr, cc, cc), lanes(-sr, zero, -sc, zero),
            lanes(zero, sr, zero, sc))


def _qkv(x, mod, g, w_q_t, w_k, w_v_t, tables, mod_row):
    B, L, _ = x.shape
    tm = min(L, 1024)
    rope = tables is not None
    in_specs = [
        pl.BlockSpec((1, tm, D_MODEL), lambda b, i: (b, i, 0)),
        pl.BlockSpec((1, 1, 6 * D_MODEL), lambda b, i: (mod_row(b), 0, 0)),
        _const_spec((1, D_MODEL)),
        _const_spec(w_q_t.shape), _const_spec(w_k.shape), _const_spec(w_v_t.shape),
    ]
    args = [x, mod, g, w_q_t, w_k, w_v_t]
    if rope:
        in_specs += [pl.BlockSpec((2 * ROPE_FREQS, tm), lambda b, i: (0, i))] * 2
        in_specs += [pl.BlockSpec((tm, LANES), lambda b, i: (i, 0))] * 3
        args += list(tables)
    return pl.pallas_call(
        functools.partial(_qkv_kernel, rope=rope),
        grid=(B, L // tm),
        in_specs=in_specs,
        out_specs=[
            pl.BlockSpec((1, tm // BLOCK, Q_DIM, BLOCK), lambda b, i: (b, i, 0, 0)),
            pl.BlockSpec((1, tm, KV_DIM), lambda b, i: (b, i, 0)),
            pl.BlockSpec((1, tm // BLOCK, KV_DIM, BLOCK), lambda b, i: (b, i, 0, 0)),
        ],
        out_shape=[
            jax.ShapeDtypeStruct((B, L // BLOCK, Q_DIM, BLOCK), BF16),
            jax.ShapeDtypeStruct((B, L, KV_DIM), BF16),
            jax.ShapeDtypeStruct((B, L // BLOCK, KV_DIM, BLOCK), BF16),
        ],
        compiler_params=pltpu.CompilerParams(
            dimension_semantics=("arbitrary", "arbitrary"), vmem_limit_bytes=VMEM_LIMIT),
        name="qkv_rope" if rope else "qkv_ctx",
    )(*args)


def _attn_kernel(*refs, window, nq, n_ctx):
    if window:
        (sink_ref, qt_ref, kp_ref, kc_ref, kn_ref, vp_ref, vc_ref, vn_ref,
         kx_ref, vx_ref, ot_ref, kbuf, vtbuf) = refs
    else:
        sink_ref, qt_ref, kx_ref, vx_ref, ot_ref, kbuf, vtbuf = refs
    width = HEADS_PER_DOT * BLOCK
    kbuf[0:n_ctx, :] = kx_ref[0]
    for j in range(n_ctx // BLOCK):
        vtbuf[:, j * BLOCK:(j + 1) * BLOCK] = vx_ref[0, j]
    if window:
        for j, (kr, vr) in enumerate(((kp_ref, vp_ref), (kc_ref, vc_ref), (kn_ref, vn_ref))):
            kbuf[n_ctx + j * BLOCK:n_ctx + (j + 1) * BLOCK, :] = kr[0]
            vtbuf[:, n_ctx + j * BLOCK:n_ctx + (j + 1) * BLOCK] = vr[0, 0]
        n = pl.program_id(1)
        kj = lax.broadcasted_iota(jnp.int32, (BLOCK, width), 0)
        qi = lax.broadcasted_iota(jnp.int32, (BLOCK, width), 1) % BLOCK
        bias_prev = jnp.where((kj >= qi) & (n > 0), 0.0, NEG_INF).astype(F32)
        bias_next = jnp.where((kj <= qi) & (n < nq - 1), 0.0, NEG_INF).astype(F32)
    lane = lax.broadcasted_iota(jnp.int32, (1, width), 1)
    ones_rows = jnp.ones((16, kbuf.shape[0]), BF16)
    for h0 in range(0, N_HEADS, HEADS_PER_DOT):
        cols = []
        for h in range(h0, h0 + HEADS_PER_DOT):
            g = h // Q_PER_KV
            parts = [qt_ref[0, 0, h * HEAD_DIM:(h + 1) * HEAD_DIM, :]]
            if g > 0:
                parts.insert(0, jnp.zeros((g * HEAD_DIM, BLOCK), BF16))
            if g < N_KV_HEADS - 1:
                parts.append(jnp.zeros(((N_KV_HEADS - 1 - g) * HEAD_DIM, BLOCK), BF16))
            cols.append(jnp.concatenate(parts, axis=0) if len(parts) > 1 else parts[0])
        q_exp = jnp.concatenate(cols, axis=1) if len(cols) > 1 else cols[0]
        sink = jnp.full((1, width), sink_ref[h0], F32)
        for r in range(1, HEADS_PER_DOT):
            sink = jnp.where(lane >= r * BLOCK, sink_ref[h0 + r], sink)
        sink = sink * LOG2E
        s = jnp.dot(kbuf[...], q_exp, preferred_element_type=F32)
        if window:
            s = jnp.concatenate([
                s[:n_ctx], s[n_ctx:n_ctx + BLOCK] + bias_prev,
                s[n_ctx + BLOCK:n_ctx + 2 * BLOCK], s[n_ctx + 2 * BLOCK:] + bias_next], axis=0)
        mx = jnp.maximum(jnp.max(s, axis=0, keepdims=True), sink)
        p = jnp.exp2(s - mx).astype(BF16)
        sink_p = jnp.exp2(sink - mx)
        gw = Q_PER_KV * BLOCK
        for g0 in range(0, HEADS_PER_DOT // Q_PER_KV):
            g = h0 // Q_PER_KV + g0
            gl = slice(g0 * gw, (g0 + 1) * gw)
            v_ext = jnp.concatenate([vtbuf[g * HEAD_DIM:(g + 1) * HEAD_DIM, :], ones_rows], axis=0)
            pv = jnp.dot(v_ext, p[:, gl], preferred_element_type=F32)
            o_t = pv[:HEAD_DIM, :] / (pv[HEAD_DIM:HEAD_DIM + 1, :] + sink_p[:, gl])
            for r in range(Q_PER_KV):
                h = g * Q_PER_KV + r
                ot_ref[0, 0, h * HEAD_DIM:(h + 1) * HEAD_DIM, :] = (
                    o_t[:, r * BLOCK:(r + 1) * BLOCK].astype(BF16))


def _attn(q_t, k, v_t, kx, vx_t, sinks):
    B, nq = q_t.shape[:2]
    C = kx.shape[1]
    window = k is not None
    smem = pl.BlockSpec(memory_space=pltpu.SMEM)
    q_spec = pl.BlockSpec((1, 1, Q_DIM, BLOCK), lambda b, n: (b, n, 0, 0))
    kx_spec = pl.BlockSpec((1, C, KV_DIM), lambda b, n: (b, 0, 0))
    vx_spec = pl.BlockSpec((1, C // BLOCK, KV_DIM, BLOCK), lambda b, n: (b, 0, 0, 0))
    if window:
        prev = lambda n: jnp.maximum(n - 1, 0)
        nxt = lambda n: jnp.minimum(n + 1, nq - 1)
        k_spec = lambda f: pl.BlockSpec((1, BLOCK, KV_DIM), lambda b, n: (b, f(n), 0))
        v_spec = lambda f: pl.BlockSpec((1, 1, KV_DIM, BLOCK), lambda b, n: (b, f(n), 0, 0))
        same = lambda n: n
        in_specs = [smem, q_spec, k_spec(prev), k_spec(same), k_spec(nxt),
                    v_spec(prev), v_spec(same), v_spec(nxt), kx_spec, vx_spec]
        args = (sinks, q_t, k, k, k, v_t, v_t, v_t, kx, vx_t)
        n_keys = C + 3 * BLOCK
    else:
        in_specs = [smem, q_spec, kx_spec, vx_spec]
        args = (sinks, q_t, kx, vx_t)
        n_keys = C
    return pl.pallas_call(
        functools.partial(_attn_kernel, window=window, nq=nq, n_ctx=C),
        grid=(B, nq),
        in_specs=in_specs,
        out_specs=q_spec,
        out_shape=jax.ShapeDtypeStruct((B, nq, Q_DIM, BLOCK), BF16),
        scratch_shapes=[pltpu.VMEM((n_keys, KV_DIM), BF16), pltpu.VMEM((KV_DIM, n_keys), BF16)],
        compiler_params=pltpu.CompilerParams(
            dimension_semantics=("arbitrary", "arbitrary"), vmem_limit_bytes=VMEM_LIMIT),
        name="attn_window" if window else "attn_ctx",
    )(*args)


def _mix_ffn_kernel(*refs, ssm, final):
    if ssm:
        (x_ref, mod_ref, yf_ref, yb_ref, z_ref, xs_ref, dsk_ref, ng_ref,
         wo_ref, gf_ref, wg_ref, wu_ref, wd_ref) = refs[:13]
        rest = refs[13:]
    else:
        (x_ref, mod_ref, o_ref_in, wo_ref, gf_ref, wg_ref, wu_ref, wd_ref) = refs[:8]
        rest = refs[8:]
    if final:
        fg_ref, out_ref = rest
    else:
        (out_ref,) = rest
    m = mod_ref[0]
    if ssm:
        mix = None
        for g in range(SSM_GROUPS):
            gs = slice(g * GROUP_DIM, (g + 1) * GROUP_DIM)
            yg = ((yf_ref[0, :, gs].astype(F32) + yb_ref[0, :, gs].astype(F32)
                   + dsk_ref[:, gs] * xs_ref[0, :, gs].astype(F32))
                  * _silu(z_ref[0, :, gs].astype(F32)))
            ms = jnp.mean(yg * yg, axis=-1, keepdims=True)
            part = (yg * lax.rsqrt(ms + EPS) * ng_ref[:, gs]).astype(BF16)
            term = jnp.dot(part, wo_ref[gs, :], preferred_element_type=F32)
            mix = term if mix is None else mix + term
    else:
        o_t = jnp.concatenate([o_ref_in[0, j] for j in range(o_ref_in.shape[1])], axis=1)
        mix = lax.dot_general(o_t, wo_ref[...], _TN, preferred_element_type=F32)
    x1 = x_ref[0] + _mod_chunk(m, 2) * mix
    h2 = _rms_mod(x1, gf_ref[...], _mod_chunk(m, 4), _mod_chunk(m, 3)).astype(BF16)
    gate = jnp.dot(h2, wg_ref[...], preferred_element_type=F32)
    up = jnp.dot(h2, wu_ref[...], preferred_element_type=F32)
    act = (_silu(gate) * up).astype(BF16)
    x2 = x1 + _mod_chunk(m, 5) * jnp.dot(act, wd_ref[...], preferred_element_type=F32)
    if final:
        ms = jnp.mean(x2 * x2, axis=-1, keepdims=True)
        x2 = x2 * lax.rsqrt(ms + EPS) * fg_ref[...]
    out_ref[0] = x2


def _mix_ffn(x, mod, mod_row, mixer_in, w_mix, g_ffn, w_gate, w_up, w_down, layer,
             ssm_norm_g=None, d_skip=None, final_g=None):
    B, L, _ = x.shape
    tm = min(L, 512)
    ssm = ssm_norm_g is not None
    final = final_g is not None
    row = lambda width: pl.BlockSpec((1, tm, width), lambda b, i: (b, i, 0))
    in_specs = [row(D_MODEL), pl.BlockSpec((1, 1, 6 * D_MODEL), lambda b, i: (mod_row(b), 0, 0))]
    args = [x, mod]
    if ssm:
        in_specs += [row(D_INNER)] * 4 + [_const_spec((1, D_INNER))] * 2
        args += list(mixer_in) + [d_skip, ssm_norm_g]
    else:
        in_specs += [pl.BlockSpec((1, tm // BLOCK, Q_DIM, BLOCK), lambda b, i: (b, i, 0, 0))]
        args += [mixer_in]
    in_specs += [_layer_spec(w_mix, 0), _const_spec((1, D_MODEL)), _layer_spec(w_gate, layer),
                 _layer_spec(w_up, layer), _layer_spec(w_down, layer)]
    args += [w_mix, g_ffn, w_gate, w_up, w_down]
    if final:
        in_specs += [_const_spec((1, D_MODEL))]
        args += [final_g]
    return pl.pallas_call(
        functools.partial(_mix_ffn_kernel, ssm=ssm, final=final),
        grid=(B, L // tm),
        in_specs=in_specs,
        out_specs=row(D_MODEL),
        out_shape=jax.ShapeDtypeStruct((B, L, D_MODEL), F32),
        compiler_params=pltpu.CompilerParams(
            dimension_semantics=("arbitrary", "arbitrary"), vmem_limit_bytes=VMEM_LIMIT_FFN),
        name="ssm_out_ffn" if ssm else "attn_out_ffn",
    )(*args)


def _inproj_kernel(x_ref, xp_ref, xn_ref, mod_ref, g_ref, w_ref, wdt_ref, dtb_ref, cw_ref, cb_ref,
                   z_ref, xa_ref, dt_ref, *, n_tiles):
    i = pl.program_id(1)
    m = mod_ref[0]
    g, scale, shift = g_ref[...], _mod_chunk(m, 1), _mod_chunk(m, 0)
    tm = x_ref.shape[1]
    h = _rms_mod(x_ref[0], g, scale, shift).astype(BF16)
    x_ext = jnp.concatenate([xp_ref[0], x_ref[0], xn_ref[0]], axis=0)
    h_ext = _rms_mod(x_ext, g, scale, shift).astype(BF16)
    z_ref[0] = jnp.dot(h, w_ref[:, :D_INNER], preferred_element_type=F32).astype(BF16)
    raw = jnp.dot(h, wdt_ref[...], preferred_element_type=F32) + dtb_ref[...]
    dt_ref[0] = jnp.maximum(raw, 0.0) + jnp.log1p(jnp.exp(-jnp.abs(raw)))
    keep_prev = jnp.where(i > 0, 1.0, 0.0)
    keep_next = jnp.where(i < n_tiles - 1, 1.0, 0.0)
    cstep = 512
    for j in range(CONV_DIM // cstep):
        sl = slice(j * cstep, (j + 1) * cstep)
        u = jnp.dot(h_ext, w_ref[:, D_INNER + j * cstep:D_INNER + (j + 1) * cstep],
                    preferred_element_type=F32)
        u = jnp.concatenate([u[0:SUBLANES] * keep_prev, u[SUBLANES:SUBLANES + tm],
                             u[SUBLANES + tm:] * keep_next], axis=0)
        rows = tm + 2 * SUBLANES
        below = pltpu.roll(u, 1, 0)[SUBLANES:SUBLANES + tm]
        above = pltpu.roll(u, rows - 1, 0)[SUBLANES:SUBLANES + tm]
        conv = (below * cw_ref[0:1, sl] + u[SUBLANES:SUBLANES + tm] * cw_ref[1:2, sl]
                + above * cw_ref[2:3, sl] + cb_ref[:, sl])
        xa_ref[0, :, sl] = _silu(conv).astype(BF16)


def _inproj(x, mod, mod_row, *, g, w_in, w_dt, dt_bias, conv_w, conv_b):
    B, L, _ = x.shape
    tm = min(L, 256)
    n_tiles = L // tm
    per = tm // SUBLANES
    row = lambda width: pl.BlockSpec((1, tm, width), lambda b, i: (b, i, 0))
    return pl.pallas_call(
        functools.partial(_inproj_kernel, n_tiles=n_tiles),
        grid=(B, n_tiles),
        in_specs=[row(D_MODEL),
                  pl.BlockSpec((1, SUBLANES, D_MODEL), lambda b, i: (b, jnp.maximum(i * per - 1, 0), 0)),
                  pl.BlockSpec((1, SUBLANES, D_MODEL),
                               lambda b, i: (b, jnp.minimum((i + 1) * per, L // SUBLANES - 1), 0)),
                  pl.BlockSpec((1, 1, 6 * D_MODEL), lambda b, i: (mod_row(b), 0, 0)),
                  _const_spec((1, D_MODEL)), _const_spec(w_in.shape), _const_spec(w_dt.shape),
                  _const_spec((1, 2 * LANES)), _const_spec((8, CONV_DIM)), _const_spec((1, CONV_DIM))],
        out_specs=[row(D_INNER), row(CONV_DIM), row(2 * LANES)],
        out_shape=[jax.ShapeDtypeStruct((B, L, D_INNER), BF16),
                   jax.ShapeDtypeStruct((B, L, CONV_DIM), BF16),
                   jax.ShapeDtypeStruct((B, L, 2 * LANES), F32)],
        compiler_params=pltpu.CompilerParams(
            dimension_semantics=("arbitrary", "arbitrary"), vmem_limit_bytes=VMEM_LIMIT),
        name="ssm_inproj",
    )(x, x, x, mod, g, w_in, w_dt, dt_bias, conv_w, conv_b)


def _split3(v):
    hi = v.astype(BF16)
    r1 = v - hi.astype(F32)
    mid = r1.astype(BF16)
    lo = (r1 - mid.astype(F32)).astype(BF16)
    return hi, mid, lo


def _ssd_kernel(*refs, reverse, nc, cps, has_init, want_y):
    (xa_ref, dt_ref, alog_ref, e64_ref), rest = refs[:4], refs[4:]
    if has_init:
        h0_ref, rest = rest[0], rest[1:]
    if want_y:
        y_ref, rest = rest[0], rest[1:]
    hfin_ref, state, ce, tbuf = rest
    c = pl.program_id(1)
    last = 0 if reverse else CHUNK - 1

    @pl.when(c == 0)
    def _():
        state[...] = h0_ref[0] if has_init else jnp.zeros(state.shape, F32)

    lane = lax.broadcasted_iota(jnp.int32, (CHUNK, LANES), 1)
    row = lax.broadcasted_iota(jnp.int32, (CHUNK, LANES), 0)
    lane_valid = (lane, (lane >= row) if reverse else (lane <= row))
    subs = range(cps - 1, -1, -1) if reverse else range(cps)
    for sub in subs:
        rows = slice(sub * CHUNK, (sub + 1) * CHUNK)
        _prepare_decay(dt_ref, rows, alog_ref, e64_ref, ce, tbuf, lane_valid, last)
        _scan_chunk(xa_ref, y_ref if want_y else None, rows, state, ce, tbuf, lane_valid, last)

    @pl.when(c == nc // cps - 1)
    def _():
        hfin_ref[0] = state[...]


def _prepare_decay(dt_ref, rows, alog_ref, e64_ref, ce, tbuf, lane_valid, last):
    lane, valid = lane_valid
    dt = jnp.where(lane < SSM_HEADS, dt_ref[0, rows, :], 0.0)
    a = dt * (-LOG2E * jnp.exp(alog_ref[0]))
    tri = jnp.where(valid, 1.0, 0.0).astype(BF16)
    a_hi, a_mid, a_lo = _split3(a)
    cum = (jnp.dot(tri, a_hi, preferred_element_type=F32)
           + jnp.dot(tri, a_mid, preferred_element_type=F32)
           + jnp.dot(tri, a_lo, preferred_element_type=F32))
    c_hi, c_mid, c_lo = _split3(cum)
    pieces = (c_hi.astype(F32) + pltpu.roll(c_mid.astype(F32), SSM_HEADS, 1)
              + pltpu.roll(c_lo.astype(F32), 2 * SSM_HEADS, 1)).astype(BF16)
    ce[...] = jnp.dot(pieces, e64_ref[...], preferred_element_type=F32)
    tb = (cum + pltpu.roll(dt, SSM_HEADS, 1)).T
    cum_t = tb[0:SSM_HEADS, :]
    dt_t = tb[SSM_HEADS:2 * SSM_HEADS, :]
    tbuf[0:SSM_HEADS, :] = cum_t - jnp.log2(dt_t)
    tbuf[SSM_HEADS:2 * SSM_HEADS, :] = jnp.exp2(cum_t[:, last:last + 1] - cum_t) * dt_t


def _scan_chunk(xa_ref, y_ref, rows, state, ce, tbuf, lane_valid, last):
    lane, valid = lane_valid
    want_y = y_ref is not None
    lane_g = lax.broadcasted_iota(jnp.int32, (CHUNK, GROUP_DIM), 1)
    head_mask = [jnp.where((lane_g >= hl * SSM_HEAD_DIM) & (lane_g < (hl + 1) * SSM_HEAD_DIM),
                           1.0, 0.0).astype(BF16) for hl in range(HEADS_PER_GROUP)]
    low_half = lane < SSM_HEAD_DIM
    for g in range(SSM_GROUPS):
        gs = slice(g * GROUP_DIM, (g + 1) * GROUP_DIM)
        b_g = xa_ref[0, rows, D_INNER + g * D_STATE:D_INNER + (g + 1) * D_STATE]
        c_g = xa_ref[0, rows, D_INNER + BC_DIM + g * D_STATE:D_INNER + BC_DIM + (g + 1) * D_STATE]
        b_t = b_g.astype(F32).T
        x_g = xa_ref[0, rows, gs]
        if want_y:
            cb_mat = lax.dot_general(c_g, b_g, _NT, preferred_element_type=F32)
            y_g = (jnp.dot(c_g, state[g].astype(BF16), preferred_element_type=F32)
                   * jnp.exp2(ce[:, gs]))
        st = jnp.zeros((D_STATE, GROUP_DIM), F32)
        for hp in range(HEADS_PER_GROUP // 2):
            pair = g * (HEADS_PER_GROUP // 2) + hp
            lhs_y, lhs_s, rhs = [], [], []
            if want_y:
                ce_pair = ce[:, pair * LANES:(pair + 1) * LANES]
                swapped = pltpu.roll(ce_pair, SSM_HEAD_DIM, 1)
            for hh in range(2):
                hl = 2 * hp + hh
                h = g * HEADS_PER_GROUP + hl
                if want_y:
                    own = low_half if hh == 0 else jnp.logical_not(low_half)
                    cum_i = jnp.where(own, ce_pair, swapped)
                    dec = jnp.exp2(jnp.where(valid, cum_i - tbuf[h:h + 1, :], NEG_INF))
                    lhs_y.append((dec * cb_mat).astype(BF16))
                lhs_s.append((b_t * tbuf[SSM_HEADS + h:SSM_HEADS + h + 1, :]).astype(BF16))
                rhs.append(x_g * head_mask[hl])
            rhs = jnp.concatenate(rhs, axis=0)
            if want_y:
                y_g = y_g + jnp.dot(jnp.concatenate(lhs_y, axis=1), rhs, preferred_element_type=F32)
            st = st + jnp.dot(jnp.concatenate(lhs_s, axis=1), rhs, preferred_element_type=F32)
        if want_y:
            y_ref[0, rows, gs] = y_g.astype(BF16)
        state[g] = state[g] * jnp.exp2(ce[last:last + 1, gs]) + st


def _ssd(xact, dt, a_log, e64, h0, direction, want_y):
    B, L, _ = xact.shape
    nc = L // CHUNK
    reverse = direction == 1
    has_init = h0 is not None
    cps = min(CHUNKS_PER_STEP, nc)
    steps = nc // cps
    rows = cps * CHUNK
    chunk = (lambda c: steps - 1 - c) if reverse else (lambda c: c)
    st_shape = (SSM_GROUPS, D_STATE, GROUP_DIM)
    st_spec = pl.BlockSpec((1,) + st_shape, lambda b, c: (b, 0, 0, 0))
    in_specs = [
        pl.BlockSpec((1, rows, CONV_DIM), lambda b, c: (b, chunk(c), 0)),
        pl.BlockSpec((1, rows, LANES), lambda b, c: (b, chunk(c), direction)),
        pl.BlockSpec((1, 1, LANES), lambda b, c: (direction, 0, 0)),
        _const_spec(e64.shape),
    ]
    args = [xact, dt, a_log, e64]
    if has_init:
        in_specs.append(st_spec)
        args.append(h0)
    out_specs, out_shape = [], []
    if want_y:
        out_specs.append(pl.BlockSpec((1, rows, D_INNER), lambda b, c: (b, chunk(c), 0)))
        out_shape.append(jax.ShapeDtypeStruct((B, L, D_INNER), BF16))
    out_specs.append(st_spec)
    out_shape.append(jax.ShapeDtypeStruct((B,) + st_shape, F32))
    scratch = [pltpu.VMEM(st_shape, F32), pltpu.VMEM((CHUNK, D_INNER), F32),
               pltpu.VMEM((2 * SSM_HEADS, CHUNK), F32)]
    return pl.pallas_call(
        functools.partial(_ssd_kernel, reverse=reverse, nc=nc, cps=cps, has_init=has_init,
                          want_y=want_y),
        grid=(B, steps),
        in_specs=in_specs,
        out_specs=out_specs,
        out_shape=out_shape,
        scratch_shapes=scratch,
        compiler_params=pltpu.CompilerParams(
            dimension_semantics=("arbitrary", "arbitrary"), vmem_limit_bytes=VMEM_LIMIT),
        name="ssd_bwd" if reverse else "ssd_fwd",
    )(*args)


def _expand_matrix(width):
    k = np.arange(LANES)[:, None]
    col_head = (np.arange(SSM_HEADS * width) // width)[None, :]
    return jnp.asarray((k % SSM_HEADS == col_head) & (k < 3 * SSM_HEADS), dtype=BF16)


def kernel(x, c, ctx, c_ctx, ada_w, ada_b, norm_mix_g, norm_ffn_g, attn_w_qkv, attn_w_o, attn_sinks,
           ssm_w_in, ssm_conv_w, ssm_conv_b, ssm_dt_bias, ssm_A_log, ssm_D, ssm_norm_g, ssm_w_out,
           ffn_w_gate, ffn_w_up, ffn_w_down, final_norm_g):
    B, L, _ = x.shape
    assert x.shape == (B, L, D_MODEL) and B <= CTX_ROW and L % 1024 == 0
    bf = lambda w: w.astype(BF16)
    row2 = lambda v: v.reshape(1, -1)

    cc = jnp.zeros((MOD_ROWS, D_MODEL), F32).at[:B].set(c).at[CTX_ROW].set(c_ctx)
    mod = _adaln(cc, ada_w, ada_b)
    mods = [mod[i].reshape(MOD_ROWS, 1, 6 * D_MODEL) for i in range(DEPTH)]
    x_row = lambda b: b
    ctx_row = lambda b: CTX_ROW

    w_qkv = attn_w_qkv[0]
    w_q_t = bf(w_qkv[:, :Q_DIM].T)
    w_k = bf(w_qkv[:, Q_DIM:Q_DIM + KV_DIM])
    w_v_t = bf(w_qkv[:, Q_DIM + KV_DIM:].T)
    g_mix = row2(norm_mix_g[0])
    q, k, v = _qkv(x, mods[0], g_mix, w_q_t, w_k, w_v_t, _rope_tables(L), x_row)
    qc, kc, vc = _qkv(ctx, mods[0], g_mix, w_q_t, w_k, w_v_t, None, ctx_row)
    sinks = attn_sinks[0].astype(F32)
    o = _attn(q, k, v, kc, vc, sinks)
    oc = _attn(qc, None, None, kc, vc, sinks)
    w_gate, w_up, w_down = bf(ffn_w_gate), bf(ffn_w_up), bf(ffn_w_down)
    ffn0 = (bf(attn_w_o), row2(norm_ffn_g[0]), w_gate, w_up, w_down, 0)
    x = _mix_ffn(x, mods[0], x_row, o, *ffn0)
    ctx = _mix_ffn(ctx, mods[0], ctx_row, oc, *ffn0)

    w_in = ssm_w_in[0]
    w_dt_raw = w_in[:, D_INNER + CONV_DIM:]
    w_dt = jnp.zeros((D_MODEL, 2 * LANES), F32)
    w_dt = w_dt.at[:, :SSM_HEADS].set(w_dt_raw[:, :SSM_HEADS])
    w_dt = bf(w_dt.at[:, LANES:LANES + SSM_HEADS].set(w_dt_raw[:, SSM_HEADS:]))
    dt_bias = jnp.zeros((2, LANES), F32).at[:, :SSM_HEADS].set(ssm_dt_bias[0]).reshape(1, 2 * LANES)
    a_log = jnp.zeros((2, 1, LANES), F32).at[:, 0, :SSM_HEADS].set(ssm_A_log[0])
    conv_w = jnp.zeros((8, CONV_DIM), F32).at[:3].set(ssm_conv_w[0])
    conv_b = row2(ssm_conv_b[0])
    d_skip = row2(jnp.repeat(ssm_D[0], SSM_HEAD_DIM))
    e64 = _expand_matrix(SSM_HEAD_DIM)
    g_mix1 = row2(norm_mix_g[1])

    inproj = functools.partial(_inproj, g=g_mix1, w_in=bf(w_in), w_dt=w_dt, dt_bias=dt_bias,
                               conv_w=conv_w, conv_b=conv_b)
    z, xact, dt = inproj(x, mods[1], x_row)
    _, xact_c, dt_c = inproj(ctx, mods[1], ctx_row)
    ssd = functools.partial(_ssd, a_log=a_log, e64=e64)
    (h_f,) = ssd(xact_c, dt_c, h0=None, direction=0, want_y=False)
    (h_b,) = ssd(xact_c, dt_c, h0=None, direction=1, want_y=False)
    y_f, _ = ssd(xact, dt, h0=h_f, direction=0, want_y=True)
    y_b, _ = ssd(xact, dt, h0=h_b, direction=1, want_y=True)
    return _mix_ffn(x, mods[1], x_row, (y_f, y_b, z, xact), bf(ssm_w_out), row2(norm_ffn_g[1]),
                    w_gate, w_up, w_down, 1,
                    ssm_norm_g=row2(ssm_norm_g[0]), d_skip=d_skip, final_g=row2(final_norm_g))
```
